```python
import jax, jax.numpy as jnp
from jax import lax
import numpy as np

D_MODEL = 1024
BATCH = 8
SEQ = 8192
DEPTH = 1

RET_HEADS = D_MODEL // 256
RET_QK_DIM = D_MODEL
RET_V_DIM = 2 * D_MODEL
RET_HEAD_QK = RET_QK_DIM // RET_HEADS
RET_HEAD_V = RET_V_DIM // RET_HEADS
RET_CHUNK = 128
ROPE_BASE = 10000.0
GN_EPS = 1e-5
CONV_DIM = D_MODEL
CONV_WIDTH = 31
LN_EPS = 1e-5
RMS_EPS = 1e-6
IN_SPLITS = (RET_QK_DIM, RET_QK_DIM, RET_V_DIM, RET_V_DIM, CONV_DIM, CONV_DIM, CONV_DIM, D_MODEL, D_MODEL)
IN_WIDTH = sum(IN_SPLITS)

kernel_name = "hybrid_retention_conformer_gated_block"


def rmsnorm(x, g):
    xf = x.astype(jnp.float32)
    y = xf * lax.rsqrt(jnp.mean(xf * xf, axis=-1, keepdims=True) + RMS_EPS)
    return (y * g.astype(jnp.float32)).astype(x.dtype)


def layernorm(x, g, b):
    xf = x.astype(jnp.float32)
    mu = jnp.mean(xf, axis=-1, keepdims=True)
    var = jnp.mean(jnp.square(xf - mu), axis=-1, keepdims=True)
    y = (xf - mu) * lax.rsqrt(var + LN_EPS)
    return (y * g.astype(jnp.float32) + b.astype(jnp.float32)).astype(x.dtype)


def head_groupnorm(o):
    of = o.astype(jnp.float32)
    mu = jnp.mean(of, axis=-1, keepdims=True)
    var = jnp.mean(jnp.square(of - mu), axis=-1, keepdims=True)
    return (of - mu) * lax.rsqrt(var + GN_EPS)


def rotary(t, positions):
    dh = t.shape[-1]
    half = dh // 2
    inv_freq = ROPE_BASE ** (-jnp.arange(half, dtype=jnp.float32) / half)
    ang = positions.astype(jnp.float32)[..., None] * inv_freq
    cos = jnp.cos(ang)[:, :, None, :]
    sin = jnp.sin(ang)[:, :, None, :]
    t1, t2 = t[..., :half], t[..., half:]
    return jnp.concatenate([t1 * cos - t2 * sin, t1 * sin + t2 * cos], axis=-1)


def retention_chunkwise(q, k, v):
    B, S, H, dk = q.shape
    dv = v.shape[-1]
    C = RET_CHUNK
    N = S // C
    log_g = jnp.log1p(-jnp.exp2(-5.0 - jnp.arange(H, dtype=jnp.float32)))
    idx = jnp.arange(C, dtype=jnp.float32)
    diff = idx[:, None] - idx[None, :]
    causal = diff >= 0
    decay_mask = jnp.where(causal, jnp.exp(log_g[:, None, None] * jnp.where(causal, diff, 0.0)), 0.0)
    xi = jnp.exp(log_g[:, None] * (idx + 1.0))
    zeta = jnp.exp(log_g[:, None] * (C - 1.0 - idx))
    g_chunk = jnp.exp(log_g * C)

    def to_chunks(t):
        return t.reshape(B, N, C, H, t.shape[-1]).transpose(1, 0, 3, 2, 4)

    def step(state, inp):
        qc, kc, vc = inp
        scores = jnp.einsum('bhid,bhjd->bhij', qc, kc) * decay_mask[None]
        inner = jnp.einsum('bhij,bhje->bhie', scores, vc)
        cross = jnp.einsum('bhid,bhde->bhie', qc, state) * xi[None, :, :, None]
        new_state = state * g_chunk[None, :, None, None] + jnp.einsum(
            'bhjd,bhje->bhde', kc * zeta[None, :, :, None], vc)
        return new_state, inner + cross

    state0 = jnp.zeros((B, H, dk, dv), jnp.float32)
    _, o = lax.scan(step, state0, (to_chunks(q), to_chunks(k), to_chunks(v)))
    return o.transpose(1, 0, 3, 2, 4).reshape(B, S, H, dv)


def causal_depthwise_conv(u, w, b):
    out = lax.conv_general_dilated(
        u, w[:, None, :].astype(u.dtype), window_strides=(1,),
        padding=[(CONV_WIDTH - 1, 0)],
        dimension_numbers=('NWC', 'WIO', 'NWC'),
        feature_group_count=u.shape[-1])
    return out + b.astype(u.dtype)


def _fwd_setup_inputs(seed: int = 0) -> dict:
    key = jax.random.key(seed)
    ks = jax.random.split(key, 16)
    f32 = jnp.float32
    x = jax.random.normal(ks[0], (BATCH, SEQ, D_MODEL), f32)
    c = jax.random.normal(ks[1], (BATCH, D_MODEL), f32)
    offsets = jax.random.randint(ks[2], (BATCH, 1), 0, 4096, dtype=jnp.int32)
    positions = offsets + jnp.arange(SEQ, dtype=jnp.int32)[None, :]
    w_ada = jax.random.normal(ks[3], (DEPTH, D_MODEL, 3 * D_MODEL), f32) * (0.1 * D_MODEL ** -0.5)
    b_ada = jax.random.normal(ks[4], (DEPTH, 3 * D_MODEL), f32) * 0.02
    pre_norm_g = 1.0 + 0.05 * jax.random.normal(ks[5], (DEPTH, D_MODEL), f32)
    w_in = jax.random.normal(ks[6], (DEPTH, D_MODEL, IN_WIDTH), f32) * D_MODEL ** -0.5
    conv_w = jax.random.normal(ks[7], (DEPTH, CONV_WIDTH, CONV_DIM), f32) * CONV_WIDTH ** -0.5
    conv_b = jax.random.normal(ks[8], (DEPTH, CONV_DIM), f32) * 0.02
    conv_ln_g = 1.0 + 0.05 * jax.random.normal(ks[9], (DEPTH, CONV_DIM), f32)
    conv_ln_b = jax.random.normal(ks[10], (DEPTH, CONV_DIM), f32) * 0.02
    w_ret_out = jax.random.normal(ks[11], (DEPTH, RET_V_DIM, D_MODEL), f32) * RET_V_DIM ** -0.5
    w_conv_out = jax.random.normal(ks[12], (DEPTH, CONV_DIM, D_MODEL), f32) * CONV_DIM ** -0.5
    w_out = jax.random.normal(ks[13], (DEPTH, D_MODEL, D_MODEL), f32) * D_MODEL ** -0.5
    post_norm_g = 1.0 + 0.05 * jax.random.normal(ks[14], (DEPTH, D_MODEL), f32)
    return {"x": x, "c": c, "positions": positions, "w_ada": w_ada, "b_ada": b_ada,
            "pre_norm_g": pre_norm_g, "w_in": w_in, "conv_w": conv_w, "conv_b": conv_b,
            "conv_ln_g": conv_ln_g, "conv_ln_b": conv_ln_b, "w_ret_out": w_ret_out,
            "w_conv_out": w_conv_out, "w_out": w_out, "post_norm_g": post_norm_g}


def _fwd_reference(x, c, positions, w_ada, b_ada, pre_norm_g, w_in, conv_w, conv_b,
              conv_ln_g, conv_ln_b, w_ret_out, w_conv_out, w_out, post_norm_g):
    B, S, _ = x.shape
    split_at = [int(v) for v in np.cumsum(IN_SPLITS)[:-1]]
    for l in range(DEPTH):
        mod = c @ w_ada[l] + b_ada[l]
        shift, scale, gate = jnp.split(mod, 3, axis=-1)
        h = rmsnorm(x, pre_norm_g[l]) * (1.0 + scale[:, None, :]) + shift[:, None, :]

        proj = h @ w_in[l]
        q, k, v, z_ret, u_val, u_gate, z_conv, g_a, g_b = jnp.split(proj, split_at, axis=-1)

        qh = rotary(q.reshape(B, S, RET_HEADS, RET_HEAD_QK).astype(jnp.float32), positions)
        kh = rotary(k.reshape(B, S, RET_HEADS, RET_HEAD_QK).astype(jnp.float32), positions) * RET_HEAD_QK ** -0.5
        vh = v.reshape(B, S, RET_HEADS, RET_HEAD_V).astype(jnp.float32)
        ret = head_groupnorm(retention_chunkwise(qh, kh, vh)).reshape(B, S, RET_V_DIM).astype(x.dtype)
        y_a = (ret * jax.nn.silu(z_ret)) @ w_ret_out[l]

        a = u_val * jax.nn.sigmoid(u_gate)
        a = causal_depthwise_conv(a, conv_w[l], conv_b[l])
        a = jax.nn.silu(layernorm(a, conv_ln_g[l], conv_ln_b[l]))
        y_b = (a * jax.nn.silu(z_conv)) @ w_conv_out[l]

        merged = jax.nn.sigmoid(g_a) * y_a + jax.nn.sigmoid(g_b) * y_b
        y = merged @ w_out[l]

        x = x + gate[:, None, :] * rmsnorm(y, post_norm_g[l])
    return x


import jax as _jax
import jax.numpy as _jnp

TWIN_FORMAT = 'train_step'
FWD_PARAMS = ['x', 'c', 'positions', 'w_ada', 'b_ada', 'pre_norm_g', 'w_in', 'conv_w', 'conv_b', 'conv_ln_g', 'conv_ln_b', 'w_ret_out', 'w_conv_out', 'w_out', 'post_norm_g']
TWIN_WEIGHTS = ['w_ada', 'b_ada', 'pre_norm_g', 'w_in', 'conv_w', 'conv_b', 'conv_ln_g', 'conv_ln_b', 'w_ret_out', 'w_conv_out', 'w_out', 'post_norm_g']
TWIN_DIFF_INPUT = 'x'
TWIN_INPUTS = ['x', 'c', 'positions', 'w_ada', 'b_ada', 'pre_norm_g', 'w_in', 'conv_w', 'conv_b', 'conv_ln_g', 'conv_ln_b', 'w_ret_out', 'w_conv_out', 'w_out', 'post_norm_g', 'loss_target', 'm_w_ada', 'm_b_ada', 'm_pre_norm_g', 'm_w_in', 'm_conv_w', 'm_conv_b', 'm_conv_ln_g', 'm_conv_ln_b', 'm_w_ret_out', 'm_w_conv_out', 'm_w_out', 'm_post_norm_g', 'v_w_ada', 'v_b_ada', 'v_pre_norm_g', 'v_w_in', 'v_conv_w', 'v_conv_b', 'v_conv_ln_g', 'v_conv_ln_b', 'v_w_ret_out', 'v_w_conv_out', 'v_w_out', 'v_post_norm_g']
TWIN_OUTPUTS = ['loss', 'grad_x', 'grad_w_ada', 'grad_b_ada', 'grad_pre_norm_g', 'grad_w_in', 'grad_conv_w', 'grad_conv_b', 'grad_conv_ln_g', 'grad_conv_ln_b', 'grad_w_ret_out', 'grad_w_conv_out', 'grad_w_out', 'grad_post_norm_g', 'delta_w_ada', 'delta_b_ada', 'delta_pre_norm_g', 'delta_w_in', 'delta_conv_w', 'delta_conv_b', 'delta_conv_ln_g', 'delta_conv_ln_b', 'delta_w_ret_out', 'delta_w_conv_out', 'delta_w_out', 'delta_post_norm_g', 'new_m_w_ada', 'new_m_b_ada', 'new_m_pre_norm_g', 'new_m_w_in', 'new_m_conv_w', 'new_m_conv_b', 'new_m_conv_ln_g', 'new_m_conv_ln_b', 'new_m_w_ret_out', 'new_m_w_conv_out', 'new_m_w_out', 'new_m_post_norm_g', 'new_v_w_ada', 'new_v_b_ada', 'new_v_pre_norm_g', 'new_v_w_in', 'new_v_conv_w', 'new_v_conv_b', 'new_v_conv_ln_g', 'new_v_conv_ln_b', 'new_v_w_ret_out', 'new_v_w_conv_out', 'new_v_w_out', 'new_v_post_norm_g']
TWIN_LEAF_KINDS = {'loss': 'loss', 'grad_x': 'grad_x', 'grad_w_ada': 'grad_w', 'grad_b_ada': 'grad_w', 'grad_pre_norm_g': 'grad_w', 'grad_w_in': 'grad_w', 'grad_conv_w': 'grad_w', 'grad_conv_b': 'grad_w', 'grad_conv_ln_g': 'grad_w', 'grad_conv_ln_b': 'grad_w', 'grad_w_ret_out': 'grad_w', 'grad_w_conv_out': 'grad_w', 'grad_w_out': 'grad_w', 'grad_post_norm_g': 'grad_w', 'delta_w_ada': 'delta_w', 'delta_b_ada': 'delta_w', 'delta_pre_norm_g': 'delta_w', 'delta_w_in': 'delta_w', 'delta_conv_w': 'delta_w', 'delta_conv_b': 'delta_w', 'delta_conv_ln_g': 'delta_w', 'delta_conv_ln_b': 'delta_w', 'delta_w_ret_out': 'delta_w', 'delta_w_conv_out': 'delta_w', 'delta_w_out': 'delta_w', 'delta_post_norm_g': 'delta_w', 'new_m_w_ada': 'new_m', 'new_m_b_ada': 'new_m', 'new_m_pre_norm_g': 'new_m', 'new_m_w_in': 'new_m', 'new_m_conv_w': 'new_m', 'new_m_conv_b': 'new_m', 'new_m_conv_ln_g': 'new_m', 'new_m_conv_ln_b': 'new_m', 'new_m_w_ret_out': 'new_m', 'new_m_w_conv_out': 'new_m', 'new_m_w_out': 'new_m', 'new_m_post_norm_g': 'new_m', 'new_v_w_ada': 'new_v', 'new_v_b_ada': 'new_v', 'new_v_pre_norm_g': 'new_v', 'new_v_w_in': 'new_v', 'new_v_conv_w': 'new_v', 'new_v_conv_b': 'new_v', 'new_v_conv_ln_g': 'new_v', 'new_v_conv_ln_b': 'new_v', 'new_v_w_ret_out': 'new_v', 'new_v_w_conv_out': 'new_v', 'new_v_w_out': 'new_v', 'new_v_post_norm_g': 'new_v'}


def _forward(args):
    return _fwd_reference(*[args[k] for k in FWD_PARAMS])


def _output_shape():
    def fwd():
        inp = _fwd_setup_inputs(0)
        return _fwd_reference(*[inp[k] for k in FWD_PARAMS])
    out = _jax.eval_shape(fwd)
    return out.shape, out.dtype

N_MICROBATCH = 1
ADAM_LR = 0.001
ADAM_B1 = 0.9
ADAM_B2 = 0.999
ADAM_EPS = 1e-08
ADAM_WD = 0.01
ADAM_STEP = 10
PER_EXAMPLE_BATCH_AXIS = {'x': 0, 'c': 0, 'positions': 0, 'loss_target': 0}
SHARED_INPUTS = []
_WEIGHT_DTYPES = {'w_ada': _jnp.float32, 'b_ada': _jnp.float32, 'pre_norm_g': _jnp.float32, 'w_in': _jnp.float32, 'conv_w': _jnp.float32, 'conv_b': _jnp.float32, 'conv_ln_g': _jnp.float32, 'conv_ln_b': _jnp.float32, 'w_ret_out': _jnp.float32, 'w_conv_out': _jnp.float32, 'w_out': _jnp.float32, 'post_norm_g': _jnp.float32}
MOMENT_SCALE = {'w_ada': 1.104615e+00, 'b_ada': 1.271596e+00, 'pre_norm_g': 7.745429e-02, 'w_in': 2.141357e-02, 'conv_w': 1.922711e-02, 'conv_b': 4.013567e-02, 'conv_ln_g': 2.332598e-02, 'conv_ln_b': 2.249184e-02, 'w_ret_out': 3.140127e-02, 'w_conv_out': 1.936956e-02, 'w_out': 3.747806e-02, 'post_norm_g': 7.414467e-01}


def _to_microbatches(a, axis):
    t = _jnp.moveaxis(a, axis, 0)
    t = t.reshape((N_MICROBATCH, t.shape[0] // N_MICROBATCH) + t.shape[1:])
    return _jnp.moveaxis(t, 1, axis + 1)


def setup_inputs(seed: int = 0) -> dict:
    inp = _fwd_setup_inputs(seed)
    key = _jax.random.fold_in(_jax.random.key(seed), 7919)
    shape, _ = _output_shape()
    out = dict(inp)
    out["loss_target"] = _jax.random.normal(_jax.random.fold_in(key, 0), shape, _jnp.float32)
    for i, name in enumerate(TWIN_WEIGHTS):
        w = inp[name].astype(_jnp.float32)
        if MOMENT_SCALE is None:
            s = _jnp.sqrt(_jnp.mean(_jnp.square(w)) + 1e-30)
        else:
            s = MOMENT_SCALE[name]
        km, kv = _jax.random.split(_jax.random.fold_in(key, i + 1))
        out[name] = w
        out["m_" + name] = s * _jax.random.normal(km, w.shape, _jnp.float32)
        out["v_" + name] = (s * s) * _jax.random.uniform(kv, w.shape, _jnp.float32, 0.5, 1.5)
    if N_MICROBATCH > 1:
        for name, axis in PER_EXAMPLE_BATCH_AXIS.items():
            out[name] = _to_microbatches(out[name], axis)
    return {'x': out['x'], 'c': out['c'], 'positions': out['positions'], 'w_ada': out['w_ada'], 'b_ada': out['b_ada'], 'pre_norm_g': out['pre_norm_g'], 'w_in': out['w_in'], 'conv_w': out['conv_w'], 'conv_b': out['conv_b'], 'conv_ln_g': out['conv_ln_g'], 'conv_ln_b': out['conv_ln_b'], 'w_ret_out': out['w_ret_out'], 'w_conv_out': out['w_conv_out'], 'w_out': out['w_out'], 'post_norm_g': out['post_norm_g'], 'loss_target': out['loss_target'], 'm_w_ada': out['m_w_ada'], 'm_b_ada': out['m_b_ada'], 'm_pre_norm_g': out['m_pre_norm_g'], 'm_w_in': out['m_w_in'], 'm_conv_w': out['m_conv_w'], 'm_conv_b': out['m_conv_b'], 'm_conv_ln_g': out['m_conv_ln_g'], 'm_conv_ln_b': out['m_conv_ln_b'], 'm_w_ret_out': out['m_w_ret_out'], 'm_w_conv_out': out['m_w_conv_out'], 'm_w_out': out['m_w_out'], 'm_post_norm_g': out['m_post_norm_g'], 'v_w_ada': out['v_w_ada'], 'v_b_ada': out['v_b_ada'], 'v_pre_norm_g': out['v_pre_norm_g'], 'v_w_in': out['v_w_in'], 'v_conv_w': out['v_conv_w'], 'v_conv_b': out['v_conv_b'], 'v_conv_ln_g': out['v_conv_ln_g'], 'v_conv_ln_b': out['v_conv_ln_b'], 'v_w_ret_out': out['v_w_ret_out'], 'v_w_conv_out': out['v_w_conv_out'], 'v_w_out': out['v_w_out'], 'v_post_norm_g': out['v_post_norm_g']}


def _loss(weights, diff, rest, loss_target):
    with _jax.named_scope("forward"):
        args = {**rest, TWIN_DIFF_INPUT: diff, **{k: w.astype(_WEIGHT_DTYPES[k]) for k, w in weights.items()}}
        y = _forward(args)
    with _jax.named_scope("loss_head"):
        err = _jnp.square(y.astype(_jnp.float32) - loss_target)
        return 0.5 * _jnp.sum(_jnp.mean(err, axis=-1)) if err.ndim else 0.5 * err


def _adamw(w, g, m, v):
    m = ADAM_B1 * m + (1.0 - ADAM_B1) * g
    v = ADAM_B2 * v + (1.0 - ADAM_B2) * _jnp.square(g)
    m_hat = m / (1.0 - ADAM_B1 ** ADAM_STEP)
    v_hat = v / (1.0 - ADAM_B2 ** ADAM_STEP)
    delta = -ADAM_LR * (m_hat / (_jnp.sqrt(v_hat) + ADAM_EPS) + ADAM_WD * w)
    return delta, m, v


def reference(x, c, positions, w_ada, b_ada, pre_norm_g, w_in, conv_w, conv_b, conv_ln_g, conv_ln_b, w_ret_out, w_conv_out, w_out, post_norm_g, loss_target, m_w_ada, m_b_ada, m_pre_norm_g, m_w_in, m_conv_w, m_conv_b, m_conv_ln_g, m_conv_ln_b, m_w_ret_out, m_w_conv_out, m_w_out, m_post_norm_g, v_w_ada, v_b_ada, v_pre_norm_g, v_w_in, v_conv_w, v_conv_b, v_conv_ln_g, v_conv_ln_b, v_w_ret_out, v_w_conv_out, v_w_out, v_post_norm_g):
    given = dict(x=x, c=c, positions=positions, w_ada=w_ada, b_ada=b_ada, pre_norm_g=pre_norm_g, w_in=w_in, conv_w=conv_w, conv_b=conv_b, conv_ln_g=conv_ln_g, conv_ln_b=conv_ln_b, w_ret_out=w_ret_out, w_conv_out=w_conv_out, w_out=w_out, post_norm_g=post_norm_g, loss_target=loss_target, m_w_ada=m_w_ada, m_b_ada=m_b_ada, m_pre_norm_g=m_pre_norm_g, m_w_in=m_w_in, m_conv_w=m_conv_w, m_conv_b=m_conv_b, m_conv_ln_g=m_conv_ln_g, m_conv_ln_b=m_conv_ln_b, m_w_ret_out=m_w_ret_out, m_w_conv_out=m_w_conv_out, m_w_out=m_w_out, m_post_norm_g=m_post_norm_g, v_w_ada=v_w_ada, v_b_ada=v_b_ada, v_pre_norm_g=v_pre_norm_g, v_w_in=v_w_in, v_conv_w=v_conv_w, v_conv_b=v_conv_b, v_conv_ln_g=v_conv_ln_g, v_conv_ln_b=v_conv_ln_b, v_w_ret_out=v_w_ret_out, v_w_conv_out=v_w_conv_out, v_w_out=v_w_out, v_post_norm_g=v_post_norm_g)
    weights = {n: given[n] for n in TWIN_WEIGHTS}
    shared = {n: given[n] for n in SHARED_INPUTS}
    per_example = {n: given[n] for n in ['x', 'c', 'positions']}
    grad_fn = _jax.value_and_grad(_loss, argnums=(0, 1))

    def one_microbatch(ex, loss_target):
        ex = dict(ex)
        diff = ex.pop(TWIN_DIFF_INPUT)
        return grad_fn(weights, diff, {**shared, **ex}, loss_target)

    if N_MICROBATCH == 1:
        loss, (grad_w, grad_x) = one_microbatch(per_example, given["loss_target"])
    else:
        def body(carry, xs):
            loss_sum, grad_sum = carry
            l_k, (gw_k, gx_k) = one_microbatch(xs[0], xs[1])
            with _jax.named_scope("update"):
                return (loss_sum + l_k, _jax.tree.map(_jnp.add, grad_sum, gw_k)), gx_k

        init = (_jnp.zeros((), _jnp.float32), _jax.tree.map(_jnp.zeros_like, weights))
        (loss, grad_w), grad_x = _jax.lax.scan(body, init, (per_example, given["loss_target"]))
    with _jax.named_scope("update"):
        delta_w, new_m, new_v = {}, {}, {}
        for n in TWIN_WEIGHTS:
            delta_w[n], new_m[n], new_v[n] = _adamw(weights[n], grad_w[n], given["m_" + n], given["v_" + n])
    return (loss, grad_x, *[grad_w[n] for n in TWIN_WEIGHTS], *[delta_w[n] for n in TWIN_WEIGHTS],
            *[new_m[n] for n in TWIN_WEIGHTS], *[new_v[n] for n in TWIN_WEIGHTS])
```

```python
import jax
import jax.numpy as jnp
import numpy as np
from jax import lax
from jax.experimental import pallas as pl
from jax.experimental.pallas import tpu as pltpu

F32 = jnp.float32
BF16 = jnp.bfloat16
MESH = pl.DeviceIdType.MESH

N_DEV = 8
D = 1024
HEADS = 4
DK = 256
DV = 512
CHUNK = 128
CONV_W = 31
HALO = 32
IN_W = 11264
W_SHARD = IN_W // N_DEV
ROPE_BASE = 10000.0
GN_EPS = 1e-5
LN_EPS = 1e-5
RMS_EPS = 1e-6
ADAM_LR, ADAM_B1, ADAM_B2, ADAM_EPS, ADAM_WD, ADAM_STEP = 0.001, 0.9, 0.999, 1e-08, 0.01, 10

COL_V, COL_Z, COL_UVAL, COL_UGATE, COL_ZCONV, COL_GA, COL_GB = 2, 4, 6, 7, 8, 9, 10

VMEM_LIMIT = 56 * 1024 * 1024


def _params(**kw):
    return pltpu.CompilerParams(vmem_limit_bytes=VMEM_LIMIT, **kw)


def _sigmoid(z):
    return jax.nn.sigmoid(z)


def _round_bf16(a):
    return a.astype(BF16).astype(F32)


def _dot(a, b):
    return jnp.dot(a, b, preferred_element_type=F32)


def _dot_nt(a, b):
    return lax.dot_general(a, b, (((1,), (1,)), ((), ())), preferred_element_type=F32)


def _dot_tn(a, b):
    return lax.dot_general(a, b, (((0,), (0,)), ((), ())), preferred_element_type=F32)


def _my_place():
    return lax.axis_index("x"), lax.axis_index("y"), lax.axis_index("c")


def _peer(k):
    x, y, c = _my_place()
    px = lax.rem(x + ((k >> 2) & 1), 2)
    py = lax.rem(y + ((k >> 1) & 1), 2)
    pc = lax.rem(c + (k & 1), 2)
    return (px, py, pc), 4 * px + 2 * py + pc


def _exchange(arrs, scatter, name, in_vmem):
    n = len(arrs)

    def body(*refs):
        ins, outs = refs[:n], refs[n:2 * n]
        send_sems, recv_sems, local_sems = refs[2 * n:]
        x, y, c = _my_place()
        me = 4 * x + 2 * y + c
        copies = []
        for i in range(n):
            src = ins[i].at[me] if scatter else ins[i]
            cp = pltpu.make_async_copy(src, outs[i].at[me], local_sems.at[i])
            cp.start()
            copies.append(cp)
        for k in range(1, N_DEV):
            peer, peer_idx = _peer(k)
            for i in range(n):
                src = ins[i].at[peer_idx] if scatter else ins[i]
                cp = pltpu.make_async_remote_copy(
                    src_ref=src, dst_ref=outs[i].at[me], send_sem=send_sems.at[i, k - 1],
                    recv_sem=recv_sems.at[i, k - 1], device_id=peer, device_id_type=MESH)
                cp.start()
                copies.append(cp)
        for cp in copies:
            cp.wait()

    space = pltpu.VMEM if in_vmem else pl.ANY
    out_shape = [jax.ShapeDtypeStruct(a.shape if scatter else (N_DEV,) + a.shape, a.dtype) for a in arrs]
    return pl.pallas_call(
        body, name=name, out_shape=out_shape,
        in_specs=[pl.BlockSpec(memory_space=space)] * n,
        out_specs=[pl.BlockSpec(memory_space=space)] * n,
        scratch_shapes=[pltpu.SemaphoreType.DMA((n, N_DEV - 1)), pltpu.SemaphoreType.DMA((n, N_DEV - 1)),
                        pltpu.SemaphoreType.DMA((n,))],
        compiler_params=_params(has_side_effects=True),
    )(*arrs)


def _rope_tables(positions, S):
    tm = min(S, 1024)

    half = DK // 2
    inv_freq = (ROPE_BASE ** (-jnp.arange(half, dtype=F32) / half)).reshape(1, half)

    def body(pos_ref, f_ref, cos_ref, sin_ref):
        ang = pos_ref[...].astype(F32) * f_ref[...]
        cos_ref[...] = jnp.cos(ang)
        sin_ref[...] = jnp.sin(ang)

    return pl.pallas_call(
        body, name="rope_tables", grid=(S // tm,),
        in_specs=[pl.BlockSpec((tm, 1), lambda i: (i, 0)), pl.BlockSpec((1, half), lambda i: (0, 0))],
        out_specs=[pl.BlockSpec((tm, half), lambda i: (i, 0))] * 2,
        out_shape=[jax.ShapeDtypeStruct((S, half), F32)] * 2,
    )(positions.reshape(S, 1), inv_freq)


def _mod_part(c_all, w_ada, b_part):
    def body(c_ref, w_ref, b_ref, o_ref):
        o_ref[...] = _dot(c_ref[...].astype(BF16), w_ref[...].astype(BF16)) + b_ref[...]

    return pl.pallas_call(body, name="mod_part",
                          out_shape=jax.ShapeDtypeStruct((N_DEV, w_ada.shape[1]), F32))(c_all, w_ada, b_part)


def _sum_devices(g):
    def body(g_ref, o_ref):
        acc = g_ref[0]
        for d in range(1, N_DEV):
            acc = acc + g_ref[d]
        o_ref[...] = acc

    return pl.pallas_call(body, name="sum_devices", out_shape=jax.ShapeDtypeStruct(g.shape[1:], F32))(g)


def _grad_w_ada(c_t, dmod):
    def body(c_ref, d_ref, o_ref):
        acc = c_ref[:, 0:1] * d_ref[0:1, :]
        for b in range(1, N_DEV):
            acc = acc + c_ref[:, b:b + 1] * d_ref[b:b + 1, :]
        o_ref[...] = acc

    return pl.pallas_call(body, name="grad_w_ada",
                          out_shape=jax.ShapeDtypeStruct((c_t.shape[0], dmod.shape[1]), F32))(c_t, dmod)


def _adam_math(w, g, m, v):
    m = ADAM_B1 * m + (1.0 - ADAM_B1) * g
    v = ADAM_B2 * v + (1.0 - ADAM_B2) * (g * g)
    m_hat = m / (1.0 - ADAM_B1 ** ADAM_STEP)
    v_hat = v / (1.0 - ADAM_B2 ** ADAM_STEP)
    delta = -ADAM_LR * (m_hat / (jnp.sqrt(v_hat) + ADAM_EPS) + ADAM_WD * w)
    return delta, m, v


def _adam_small(g, w, m, v, name):
    def body(g_ref, w_ref, m_ref, v_ref, d_ref, nm_ref, nv_ref):
        d_ref[...], nm_ref[...], nv_ref[...] = _adam_math(w_ref[...], g_ref[...], m_ref[...], v_ref[...])

    return pl.pallas_call(body, name=name, out_shape=[jax.ShapeDtypeStruct(w.shape, F32)] * 3)(g, w, m, v)


def _adam_shard(parts, w, m, v, name, tr):
    R, L = w.shape

    def body(p_ref, w_ref, m_ref, v_ref, g_ref, d_ref, nm_ref, nv_ref):
        g = p_ref[0].astype(F32)
        for d in range(1, N_DEV):
            g = g + p_ref[d].astype(F32)
        g_ref[...] = g
        d_ref[...], nm_ref[...], nv_ref[...] = _adam_math(w_ref[...], g, m_ref[...], v_ref[...])

    blk = pl.BlockSpec((tr, L), lambda i: (i, 0))
    return pl.pallas_call(
        body, name=name, grid=(R // tr,),
        in_specs=[pl.BlockSpec((N_DEV, tr, L), lambda i: (0, i, 0)), blk, blk, blk],
        out_specs=[blk] * 4, out_shape=[jax.ShapeDtypeStruct((R, L), F32)] * 4,
        compiler_params=_params(),
    )(parts, w, m, v)


def _inproj_fwd(x, g_pre, scale, shift, w_in_g, tm):
    S = x.shape[0]

    def body(x_ref, g_ref, sc_ref, sh_ref, w_ref, proj_ref, h_ref, h_scr):
        @pl.when(pl.program_id(1) == 0)
        def _():
            xf = x_ref[...]
            r = lax.rsqrt(jnp.mean(xf * xf, axis=-1, keepdims=True) + RMS_EPS)
            h = (xf * r * g_ref[...]) * (1.0 + sc_ref[...]) + sh_ref[...]
            h_scr[...] = h.astype(BF16)
            h_ref[...] = h.astype(BF16)

        proj_ref[...] = _dot(h_scr[...], w_ref[...])

    vec = pl.BlockSpec((1, D), lambda i, j: (0, 0))
    return pl.pallas_call(
        body, name="inproj_fwd", grid=(S // tm, N_DEV),
        in_specs=[pl.BlockSpec((tm, D), lambda i, j: (i, 0)), vec, vec, vec,
                  pl.BlockSpec((None, D, W_SHARD), lambda i, j: (j, 0, 0))],
        out_specs=[pl.BlockSpec((tm, W_SHARD), lambda i, j: (i, j)), pl.BlockSpec((tm, D), lambda i, j: (i, 0))],
        out_shape=[jax.ShapeDtypeStruct((S, IN_W), F32), jax.ShapeDtypeStruct((S, D), BF16)],
        scratch_shapes=[pltpu.VMEM((tm, D), BF16)],
        compiler_params=_params(),
    )(x, g_pre, scale, shift, w_in_g)


def _decay_tables():
    log_g = jnp.log1p(-jnp.exp2(-5.0 - jnp.arange(HEADS, dtype=F32)))
    idx = jnp.arange(CHUNK, dtype=F32)
    diff = idx[:, None] - idx[None, :]
    causal = diff >= 0
    mask = jnp.where(causal, jnp.exp(log_g[:, None, None] * jnp.where(causal, diff, 0.0)), 0.0)
    xi = jnp.exp(log_g[:, None] * (idx + 1.0))[:, :, None]
    zeta = jnp.exp(log_g[:, None] * (CHUNK - 1.0 - idx))[:, :, None]
    g_chunk = jnp.broadcast_to(jnp.exp(log_g * CHUNK)[:, None, None], (HEADS, 1, 128))
    return mask, xi, zeta, g_chunk


def _rot(t, cos, sin):
    t1, t2 = t[:, :DK // 2], t[:, DK // 2:]
    return jnp.concatenate([t1 * cos - t2 * sin, t1 * sin + t2 * cos], axis=-1)


def _rot_back(t, cos, sin):
    t1, t2 = t[:, :DK // 2], t[:, DK // 2:]
    return jnp.concatenate([t1 * cos + t2 * sin, t2 * cos - t1 * sin], axis=-1)


def _silu(z):
    return z * _sigmoid(z)


def _retention_fwd(proj, cos, sin, tables, cpb):
    S = proj.shape[0]
    tb = cpb * CHUNK
    mask, xi, zeta, g_chunk = tables

    def body(q_ref, k_ref, v_ref, z_ref, cos_ref, sin_ref, mask_ref, xi_ref, zeta_ref, gc_ref,
             pa_ref, o_ref, st_ref, state):
        @pl.when(pl.program_id(1) == 0)
        def _():
            state[...] = jnp.zeros_like(state)

        m, xi_c, zeta_c, gc = mask_ref[...], xi_ref[...], zeta_ref[...], gc_ref[0:1, 0:1]
        for ci in range(cpb):
            rows = pl.ds(ci * CHUNK, CHUNK)
            cs, sn = cos_ref[rows, :], sin_ref[rows, :]
            qr = _rot(q_ref[rows, :], cs, sn)
            kr = _rot(k_ref[rows, :], cs, sn) * (DK ** -0.5)
            vb = v_ref[rows, :].astype(BF16)
            qb = qr.astype(BF16)
            st = state[...]
            stb = st.astype(BF16)
            st_ref[ci] = stb
            scores = (_dot_nt(qb, kr.astype(BF16)) * m).astype(BF16)
            o = _dot(scores, vb) + _dot(qb, stb) * xi_c
            state[...] = st * gc + _dot_tn((kr * zeta_c).astype(BF16), vb)
            o_ref[rows, :] = o
            mu = jnp.mean(o, axis=-1, keepdims=True)
            oc = o - mu
            var = jnp.mean(oc * oc, axis=-1, keepdims=True)
            pa_ref[rows, :] = (oc * lax.rsqrt(var + GN_EPS) * _silu(z_ref[rows, :])).astype(BF16)

    tab = lambda shape: pl.BlockSpec((None,) + shape, lambda h, n: (h, 0, 0))
    return pl.pallas_call(
        body, name="retention_fwd", grid=(HEADS, S // tb),
        in_specs=[pl.BlockSpec((tb, DK), lambda h, n: (n, h)),
                  pl.BlockSpec((tb, DK), lambda h, n: (n, HEADS + h)),
                  pl.BlockSpec((tb, DV), lambda h, n: (n, COL_V * 2 + h)),
                  pl.BlockSpec((tb, DV), lambda h, n: (n, COL_Z * 2 + h)),
                  pl.BlockSpec((tb, DK // 2), lambda h, n: (n, 0)),
                  pl.BlockSpec((tb, DK // 2), lambda h, n: (n, 0)),
                  tab((CHUNK, CHUNK)), tab((CHUNK, 1)), tab((CHUNK, 1)), tab((1, 128))],
        out_specs=[pl.BlockSpec((tb, DV), lambda h, n: (n, h)),
                   pl.BlockSpec((tb, DV), lambda h, n: (n, h)),
                   pl.BlockSpec((None, cpb, DK, DV), lambda h, n: (h, n, 0, 0))],
        out_shape=[jax.ShapeDtypeStruct((S, HEADS * DV), BF16), jax.ShapeDtypeStruct((S, HEADS * DV), F32),
                   jax.ShapeDtypeStruct((HEADS, S // CHUNK, DK, DV), BF16)],
        scratch_shapes=[pltpu.VMEM((DK, DV), F32)],
        compiler_params=_params(),
    )(proj, proj, proj, proj, cos, sin, mask, xi, zeta, g_chunk)


def _conv_forward_tile(uv, ug, uv_halo, ug_halo, first, w_ref, ext):
    tm = uv.shape[0]
    halo = _round_bf16(uv_halo * _sigmoid(ug_halo))
    ext[pl.ds(0, HALO), :] = jnp.where(first, 0.0, halo)
    ext[pl.ds(HALO, tm), :] = _round_bf16(uv * _sigmoid(ug))
    acc = w_ref[0:1, :] * ext[pl.ds(HALO - (CONV_W - 1), tm), :]
    for j in range(1, CONV_W):
        acc = acc + w_ref[j:j + 1, :] * ext[pl.ds(HALO - (CONV_W - 1) + j, tm), :]
    return acc


def _layernorm_stats(a):
    mu = jnp.mean(a, axis=-1, keepdims=True)
    ac = a - mu
    var = jnp.mean(ac * ac, axis=-1, keepdims=True)
    rstd = lax.rsqrt(var + LN_EPS)
    return ac * rstd, rstd


def _conv_fwd(proj, conv_w, conv_b, ln_g, ln_b, tm):
    S = proj.shape[0]
    hb = tm // HALO

    def body(uv_ref, ug_ref, uvh_ref, ugh_ref, z_ref, w_ref, b_ref, g_ref, bb_ref, pb_ref, ext):
        first = pl.program_id(0) == 0
        a1 = _conv_forward_tile(uv_ref[...], ug_ref[...], uvh_ref[...], ugh_ref[...], first, w_ref, ext) + b_ref[...]
        xhat, _ = _layernorm_stats(a1)
        a3 = _silu(xhat * g_ref[...] + bb_ref[...])
        pb_ref[...] = (a3 * _silu(z_ref[...])).astype(BF16)

    vec = pl.BlockSpec((1, D), lambda i: (0, 0))
    halo = lambda col: pl.BlockSpec((HALO, D), lambda i: (jnp.maximum(i * hb - 1, 0), col))
    return pl.pallas_call(
        body, name="conv_fwd", grid=(S // tm,),
        in_specs=[pl.BlockSpec((tm, D), lambda i: (i, COL_UVAL)), pl.BlockSpec((tm, D), lambda i: (i, COL_UGATE)),
                  halo(COL_UVAL), halo(COL_UGATE), pl.BlockSpec((tm, D), lambda i: (i, COL_ZCONV)),
                  pl.BlockSpec((CONV_W + 1, D), lambda i: (0, 0)), vec, vec, vec],
        out_specs=pl.BlockSpec((tm, D), lambda i: (i, 0)),
        out_shape=jax.ShapeDtypeStruct((S, D), BF16),
        scratch_shapes=[pltpu.VMEM((HALO + tm, D), F32)],
        compiler_params=_params(),
    )(proj, proj, proj, proj, proj, conv_w, conv_b, ln_g, ln_b)


def _head(p_a, p_b, proj, x, target, gate, g_post, w_ret, w_conv, w_out, tm):
    S = x.shape[0]

    def body(pa_ref, pb_ref, ga_ref, gb_ref, x_ref, t_ref, gate_ref, gp_ref, wr_ref, wc_ref, wo_ref,
             loss_ref, dout_ref, dg_ref, dpa_ref, dpb_ref, mb_ref, dy_ref, dya_ref, dyb_ref, small_ref):
        @pl.when(pl.program_id(0) == 0)
        def _():
            loss_ref[...] = jnp.zeros_like(loss_ref)
            small_ref[...] = jnp.zeros_like(small_ref)

        gate, gp = gate_ref[...], gp_ref[...]
        ya = _dot(pa_ref[...], wr_ref[...])
        yb = _dot(pb_ref[...], wc_ref[...])
        sa, sb = _sigmoid(ga_ref[...]), _sigmoid(gb_ref[...])
        mb = (sa * ya + sb * yb).astype(BF16)
        mb_ref[...] = mb
        y = _dot(mb, wo_ref[...])
        r = lax.rsqrt(jnp.mean(y * y, axis=-1, keepdims=True) + RMS_EPS)
        yhat = y * r
        yn = yhat * gp
        err = (x_ref[...] + gate * yn) - t_ref[...]
        loss_ref[...] += 0.5 * jnp.sum(jnp.mean(err * err, axis=-1, keepdims=True))
        dout = err * (1.0 / D)
        dout_ref[...] = dout
        small_ref[0:1, :] += jnp.sum(dout * yn, axis=0, keepdims=True)
        dyn = dout * gate
        small_ref[1:2, :] += jnp.sum(dyn * yhat, axis=0, keepdims=True)
        dyhat = dyn * gp
        dy = (r * (dyhat - yhat * jnp.mean(dyhat * yhat, axis=-1, keepdims=True))).astype(BF16)
        dy_ref[...] = dy
        dmerged = _dot_nt(dy, wo_ref[...])
        dya = dmerged * sa
        dyb = dmerged * sb
        dg_ref[:, 0:D] = (dya * ya * (1.0 - sa)).astype(BF16)
        dg_ref[:, D:2 * D] = (dyb * yb * (1.0 - sb)).astype(BF16)
        dya, dyb = dya.astype(BF16), dyb.astype(BF16)
        dya_ref[...] = dya
        dyb_ref[...] = dyb
        dpa_ref[...] = _dot_nt(dya, wr_ref[...])
        dpb_ref[...] = _dot_nt(dyb, wc_ref[...])

    vec = pl.BlockSpec((1, D), lambda i: (0, 0))
    tile = lambda w, col=0: pl.BlockSpec((tm, w), lambda i: (i, col))
    whole = lambda a: pl.BlockSpec(a.shape, lambda i: (0, 0), pipeline_mode=pl.Buffered(1))
    act = lambda w: jax.ShapeDtypeStruct((S, w), BF16)
    return pl.pallas_call(
        body, name="head", grid=(S // tm,),
        in_specs=[tile(2 * D), tile(D), tile(D, COL_GA), tile(D, COL_GB), tile(D), tile(D), vec, vec,
                  whole(w_ret), whole(w_conv), whole(w_out)],
        out_specs=[pl.BlockSpec((8, 128), lambda i: (0, 0)), tile(D), tile(2 * D), tile(2 * D), tile(D),
                   tile(D), tile(D), tile(D), tile(D), pl.BlockSpec((8, D), lambda i: (0, 0))],
        out_shape=[jax.ShapeDtypeStruct((8, 128), F32), jax.ShapeDtypeStruct((S, D), F32),
                   act(2 * D), jax.ShapeDtypeStruct((S, 2 * D), F32), jax.ShapeDtypeStruct((S, D), F32),
                   act(D), act(D), act(D), act(D), jax.ShapeDtypeStruct((8, D), F32)],
        compiler_params=_params(),
    )(p_a, p_b, proj, proj, x, target, gate, g_post, w_ret, w_conv, w_out)


def _matmul_tn(a, b, name, bn, tk):
    S, M = a.shape
    N = b.shape[1]

    def body(a_ref, b_ref, o_ref):
        @pl.when(pl.program_id(1) == 0)
        def _():
            o_ref[...] = jnp.zeros_like(o_ref)

        o_ref[...] += _dot_tn(a_ref[...], b_ref[...])

    return pl.pallas_call(
        body, name=name, grid=(N // bn, S // tk),
        in_specs=[pl.BlockSpec((tk, M), lambda j, k: (k, 0)), pl.BlockSpec((tk, bn), lambda j, k: (k, j))],
        out_specs=pl.BlockSpec((M, bn), lambda j, k: (0, j)),
        out_shape=jax.ShapeDtypeStruct((M, N), F32),
        compiler_params=_params(),
    )(a, b)


SMALL_B_ROWS = 40


def _dsilu(z, s):
    return s * (1.0 + z * (1.0 - s))


def _conv_bwd(dp_b, proj, conv_w, conv_b, ln_g, ln_b, tm):
    S = proj.shape[0]
    nt = S // tm
    hb = tm // HALO

    def body(dpb_ref, uv_ref, ug_ref, uvh_ref, ugh_ref, z_ref, w_ref, b_ref, g_ref, bb_ref,
             dc_ref, small_ref, ext, dext, carry):
        i = pl.program_id(0)

        @pl.when(i == 0)
        def _():
            small_ref[...] = jnp.zeros_like(small_ref)
            carry[...] = jnp.zeros_like(carry)

        uv, ug = uv_ref[...], ug_ref[...]
        a1 = _conv_forward_tile(uv, ug, uvh_ref[...], ugh_ref[...], i == nt - 1, w_ref, ext) + b_ref[...]
        xhat, rstd = _layernorm_stats(a1)
        g = g_ref[...]
        a2 = xhat * g + bb_ref[...]
        s2 = _sigmoid(a2)
        zc = z_ref[...]
        sz = _sigmoid(zc)
        dpb = dpb_ref[...]
        dc_ref[:, 2 * D:3 * D] = (dpb * (a2 * s2) * _dsilu(zc, sz)).astype(BF16)
        da2 = dpb * (zc * sz) * _dsilu(a2, s2)
        small_ref[32:33, :] += jnp.sum(da2 * xhat, axis=0, keepdims=True)
        small_ref[33:34, :] += jnp.sum(da2, axis=0, keepdims=True)
        dxhat = da2 * g
        da1 = rstd * (dxhat - jnp.mean(dxhat, axis=-1, keepdims=True)
                      - xhat * jnp.mean(dxhat * xhat, axis=-1, keepdims=True))
        small_ref[31:32, :] += jnp.sum(da1, axis=0, keepdims=True)
        da1 = _round_bf16(da1)
        dext[pl.ds(0, tm), :] = da1
        dext[pl.ds(tm, HALO), :] = carry[...]
        carry[...] = da1[0:HALO, :]
        for j in range(CONV_W):
            small_ref[j:j + 1, :] += jnp.sum(da1 * ext[pl.ds(HALO - (CONV_W - 1) + j, tm), :], axis=0, keepdims=True)
        da0 = w_ref[0:1, :] * dext[pl.ds(CONV_W - 1, tm), :]
        for j in range(1, CONV_W):
            da0 = da0 + w_ref[j:j + 1, :] * dext[pl.ds(CONV_W - 1 - j, tm), :]
        sg = _sigmoid(ug)
        dc_ref[:, 0:D] = (da0 * sg).astype(BF16)
        dc_ref[:, D:2 * D] = (da0 * uv * sg * (1.0 - sg)).astype(BF16)

    vec = pl.BlockSpec((1, D), lambda i: (0, 0))
    tile = lambda col: pl.BlockSpec((tm, D), lambda i: (nt - 1 - i, col))
    halo = lambda col: pl.BlockSpec((HALO, D), lambda i: (jnp.maximum((nt - 1 - i) * hb - 1, 0), col))
    return pl.pallas_call(
        body, name="conv_bwd", grid=(nt,),
        in_specs=[tile(0), tile(COL_UVAL), tile(COL_UGATE), halo(COL_UVAL), halo(COL_UGATE), tile(COL_ZCONV),
                  pl.BlockSpec((CONV_W + 1, D), lambda i: (0, 0)), vec, vec, vec],
        out_specs=[pl.BlockSpec((tm, 3 * D), lambda i: (nt - 1 - i, 0)),
                   pl.BlockSpec((SMALL_B_ROWS, D), lambda i: (0, 0))],
        out_shape=[jax.ShapeDtypeStruct((S, 3 * D), BF16), jax.ShapeDtypeStruct((SMALL_B_ROWS, D), F32)],
        scratch_shapes=[pltpu.VMEM((HALO + tm, D), F32), pltpu.VMEM((tm + HALO, D), F32), pltpu.VMEM((HALO, D), F32)],
        compiler_params=_params(),
    )(dp_b, proj, proj, proj, proj, proj, conv_w, conv_b, ln_g, ln_b)


RET_W = 2 * DK + 2 * DV


def _retention_bwd(dp_a, o, proj, states, cos, sin, tables, cpb):
    S = proj.shape[0]
    tb = cpb * CHUNK
    nb = S // tb
    mask, xi, zeta, g_chunk = tables

    def body(dpa_ref, o_ref, q_ref, k_ref, v_ref, z_ref, st_ref, cos_ref, sin_ref, mask_ref, xi_ref, zeta_ref,
             gc_ref, out_ref, dstate):
        @pl.when(pl.program_id(1) == 0)
        def _():
            dstate[...] = jnp.zeros_like(dstate)

        m, xi_c, zeta_c, gc = mask_ref[...], xi_ref[...], zeta_ref[...], gc_ref[0:1, 0:1]
        for ci in reversed(range(cpb)):
            rows = pl.ds(ci * CHUNK, CHUNK)
            o = o_ref[rows, :]
            oc = o - jnp.mean(o, axis=-1, keepdims=True)
            rstd = lax.rsqrt(jnp.mean(oc * oc, axis=-1, keepdims=True) + GN_EPS)
            rhat = oc * rstd
            z = z_ref[rows, :]
            sz = _sigmoid(z)
            dpa = dpa_ref[rows, :]
            out_ref[rows, 2 * DK + DV:RET_W] = (dpa * rhat * _dsilu(z, sz)).astype(BF16)
            dret = dpa * (z * sz)
            d_o = rstd * (dret - jnp.mean(dret, axis=-1, keepdims=True)
                          - rhat * jnp.mean(dret * rhat, axis=-1, keepdims=True))
            cs, sn = cos_ref[rows, :], sin_ref[rows, :]
            qb = _rot(q_ref[rows, :], cs, sn).astype(BF16)
            kr = _rot(k_ref[rows, :], cs, sn) * (DK ** -0.5)
            kb = kr.astype(BF16)
            kz = (kr * zeta_c).astype(BF16)
            vb = v_ref[rows, :].astype(BF16)
            scores = (_dot_nt(qb, kb) * m).astype(BF16)
            dob = d_o.astype(BF16)
            dox = (d_o * xi_c).astype(BF16)
            dscores = (_dot_nt(dob, vb) * m).astype(BF16)
            dst = dstate[...]
            dstb = dst.astype(BF16)
            dqr = _dot(dscores, kb) + _dot_nt(dox, st_ref[ci])
            dkr = _dot_tn(dscores, qb) + _dot_nt(vb, dstb) * zeta_c
            dv = _dot_tn(scores, dob) + _dot(kz, dstb)
            dstate[...] = dst * gc + _dot_tn(qb, dox)
            out_ref[rows, 0:DK] = _rot_back(dqr, cs, sn).astype(BF16)
            out_ref[rows, DK:2 * DK] = (_rot_back(dkr, cs, sn) * (DK ** -0.5)).astype(BF16)
            out_ref[rows, 2 * DK:2 * DK + DV] = dv.astype(BF16)

    tab = lambda shape: pl.BlockSpec((None,) + shape, lambda h, n: (h, 0, 0))
    blk = lambda w, col0: pl.BlockSpec((tb, w), lambda h, n: (nb - 1 - n, col0 + h))
    return pl.pallas_call(
        body, name="retention_bwd", grid=(HEADS, nb),
        in_specs=[blk(DV, 0), blk(DV, 0), blk(DK, 0), blk(DK, HEADS), blk(DV, COL_V * 2), blk(DV, COL_Z * 2),
                  pl.BlockSpec((None, cpb, DK, DV), lambda h, n: (h, nb - 1 - n, 0, 0)),
                  pl.BlockSpec((tb, DK // 2), lambda h, n: (nb - 1 - n, 0)),
                  pl.BlockSpec((tb, DK // 2), lambda h, n: (nb - 1 - n, 0)),
                  tab((CHUNK, CHUNK)), tab((CHUNK, 1)), tab((CHUNK, 1)), tab((1, 128))],
        out_specs=blk(RET_W, 0),
        out_shape=jax.ShapeDtypeStruct((S, HEADS * RET_W), BF16),
        scratch_shapes=[pltpu.VMEM((DK, DV), F32)],
        compiler_params=_params(),
    )(dp_a, o, proj, proj, proj, proj, states, cos, sin, mask, xi, zeta, g_chunk)


def _to_head_major(w):
    parts = []
    for h in range(HEADS):
        parts += [w[:, h * DK:(h + 1) * DK], w[:, D + h * DK:D + (h + 1) * DK],
                  w[:, 2 * D + h * DV:2 * D + (h + 1) * DV], w[:, 4 * D + h * DV:4 * D + (h + 1) * DV]]
    return jnp.concatenate(parts, axis=1)


def _from_head_major(w):
    starts = (0, DK, 2 * DK, 2 * DK + DV)
    widths = (DK, DK, DV, DV)
    parts = [w[:, h * RET_W + s:h * RET_W + s + n] for s, n in zip(starts, widths) for h in range(HEADS)]
    return jnp.concatenate(parts, axis=1)


def _dh(d_r, d_c, d_g, w_r, w_c, w_g, x, dout, g_pre, scale, tm):
    S = x.shape[0]

    def body(dr_ref, dc_ref, dg_ref, wr_ref, wc_ref, wg_ref, x_ref, dout_ref, g_ref, sc_ref, gx_ref, small_ref):
        @pl.when(pl.program_id(0) == 0)
        def _():
            small_ref[...] = jnp.zeros_like(small_ref)

        dh = _dot_nt(dr_ref[...], wr_ref[...]) + _dot_nt(dc_ref[...], wc_ref[...]) + _dot_nt(dg_ref[...], wg_ref[...])
        xf = x_ref[...]
        r = lax.rsqrt(jnp.mean(xf * xf, axis=-1, keepdims=True) + RMS_EPS)
        xhat = xf * r
        g, sc1 = g_ref[...], 1.0 + sc_ref[...]
        small_ref[0:1, :] += jnp.sum(dh, axis=0, keepdims=True)
        small_ref[1:2, :] += jnp.sum(dh * (xhat * g), axis=0, keepdims=True)
        small_ref[2:3, :] += jnp.sum(dh * sc1 * xhat, axis=0, keepdims=True)
        dxhat = dh * sc1 * g
        gx_ref[...] = dout_ref[...] + r * (dxhat - xhat * jnp.mean(dxhat * xhat, axis=-1, keepdims=True))

    vec = pl.BlockSpec((1, D), lambda i: (0, 0))
    tile = lambda w: pl.BlockSpec((tm, w), lambda i: (i, 0))
    whole = lambda a: pl.BlockSpec(a.shape, lambda i: (0, 0), pipeline_mode=pl.Buffered(1))
    return pl.pallas_call(
        body, name="dh", grid=(S // tm,),
        in_specs=[tile(d_r.shape[1]), tile(d_c.shape[1]), tile(d_g.shape[1]), whole(w_r), whole(w_c), whole(w_g),
                  tile(D), tile(D), vec, vec],
        out_specs=[tile(D), pl.BlockSpec((8, D), lambda i: (0, 0))],
        out_shape=[jax.ShapeDtypeStruct((S, D), F32), jax.ShapeDtypeStruct((8, D), F32)],
        compiler_params=_params(),
    )(d_r, d_c, d_g, w_r, w_c, w_g, x, dout, g_pre, scale)


ROW_GATE, ROW_POST = 0, 1
ROW_CONV_W, ROW_CONV_B, ROW_LN_G, ROW_LN_B = 8, 8 + 31, 8 + 32, 8 + 33
ROW_SHIFT, ROW_SCALE, ROW_PRE = 48, 49, 50


def kernel(x, c, positions, w_ada, b_ada, pre_norm_g, w_in, conv_w, conv_b, conv_ln_g, conv_ln_b, w_ret_out, w_conv_out, w_out, post_norm_g, loss_target, m_w_ada, m_b_ada, m_pre_norm_g, m_w_in, m_conv_w, m_conv_b, m_conv_ln_g, m_conv_ln_b, m_w_ret_out, m_w_conv_out, m_w_out, m_post_norm_g, v_w_ada, v_b_ada, v_pre_norm_g, v_w_in, v_conv_w, v_conv_b, v_conv_ln_g, v_conv_ln_b, v_w_ret_out, v_w_conv_out, v_w_out, v_post_norm_g):
    S = x.shape[1]
    axes = ("x", "y", "c")
    me = 4 * lax.axis_index("x") + 2 * lax.axis_index("y") + lax.axis_index("c")
    x2, target = x[0], loss_target[0]
    ada_w = w_ada.shape[2]
    cw_w = conv_w.shape[2]

    c_g, conv_w_g = _exchange([jnp.pad(c, ((0, 7), (0, 0))), jnp.pad(conv_w[0], ((0, 1), (0, 0)))],
                              False, "gather_c_conv_w", True)
    c_all = c_g[:, 0, :]
    conv_w_full = _round_bf16(conv_w_g.transpose(1, 0, 2).reshape(CONV_W + 1, D))
    b_part = lax.dynamic_slice(b_ada, (0, me * ada_w), (1, ada_w))
    mod_g = _exchange([_mod_part(c_all, w_ada[0], b_part)], False, "gather_mod", True)[0]
    mod = lax.dynamic_index_in_dim(mod_g, me, axis=1, keepdims=False).reshape(1, 3 * D)
    shift, scale, gate = mod[:, :D], mod[:, D:2 * D], mod[:, 2 * D:]

    g_in, g_ret, g_conv, g_out = _exchange(
        [w_in[0].astype(BF16), w_ret_out[0].astype(BF16), w_conv_out[0].astype(BF16), w_out[0].astype(BF16)],
        False, "gather_weights", False)
    w_in_full = g_in.transpose(1, 0, 2).reshape(D, IN_W)
    w_ret_full, w_conv_full, w_out_full = g_ret.reshape(2 * D, D), g_conv.reshape(D, D), g_out.reshape(D, D)

    cos, sin = _rope_tables(positions, S)
    tables = _decay_tables()
    proj, h = _inproj_fwd(x2, pre_norm_g, scale, shift, g_in, tm=min(S, 1024))
    p_a, o, states = _retention_fwd(proj, cos, sin, tables, cpb=4)
    p_b = _conv_fwd(proj, conv_w_full, conv_b, conv_ln_g, conv_ln_b, tm=512)

    loss_sums, dout, d_g, dp_a, dp_b, merged, dy, dy_a, dy_b, small_a = _head(
        p_a, p_b, proj, x2, target, gate, post_norm_g, w_ret_full, w_conv_full, w_out_full, tm=256)
    d_c, small_b = _conv_bwd(dp_b, proj, conv_w_full, conv_b, conv_ln_g, conv_ln_b, tm=512)
    d_r = _retention_bwd(dp_a, o, proj, states, cos, sin, tables, cpb=4)
    grad_x, small_c = _dh(d_r, d_c, d_g, _to_head_major(w_in_full), w_in_full[:, 6 * D:9 * D], w_in_full[:, 9 * D:],
                          x2, dout, pre_norm_g, scale, tm=256)
    dw_out = _matmul_tn(merged, dy, "dw_out", bn=D, tk=512)
    dw_ret = _matmul_tn(p_a, dy_a, "dw_ret_out", bn=D, tk=512)
    dw_conv = _matmul_tn(p_b, dy_b, "dw_conv_out", bn=D, tk=512)
    dw_r = _matmul_tn(h, d_r, "dw_in_ret", bn=RET_W, tk=512)
    dw_c = _matmul_tn(h, d_c, "dw_in_conv", bn=D, tk=512)
    dw_g = _matmul_tn(h, d_g, "dw_in_gate", bn=D, tk=512)
    dw_in = jnp.concatenate([_from_head_major(dw_r), dw_c, dw_g], axis=1)

    r_in, r_ret, r_conv, r_out = _exchange(
        [dw_in.astype(BF16).reshape(D, N_DEV, W_SHARD).transpose(1, 0, 2),
         dw_ret.astype(BF16).reshape(N_DEV, 2 * D // N_DEV, D),
         dw_conv.astype(BF16).reshape(N_DEV, D // N_DEV, D), dw_out.astype(BF16).reshape(N_DEV, D // N_DEV, D)],
        True, "scatter_grads", False)
    big = {"w_in": _adam_shard(r_in, w_in[0], m_w_in[0], v_w_in[0], "adam_w_in", 128),
           "w_ret_out": _adam_shard(r_ret, w_ret_out[0], m_w_ret_out[0], v_w_ret_out[0], "adam_w_ret_out", 128),
           "w_conv_out": _adam_shard(r_conv, w_conv_out[0], m_w_conv_out[0], v_w_conv_out[0], "adam_w_conv_out", 128),
           "w_out": _adam_shard(r_out, w_out[0], m_w_out[0], v_w_out[0], "adam_w_out", 128)}

    sm_g = _exchange([jnp.concatenate([small_a, small_b, small_c], axis=0)], False, "gather_small", True)[0]
    sm = _sum_devices(sm_g)
    row = lambda r: sm[r:r + 1]
    g8 = jnp.concatenate([row(ROW_SHIFT), row(ROW_SCALE), row(ROW_GATE), row(ROW_PRE), row(ROW_CONV_B),
                          row(ROW_LN_G), row(ROW_LN_B), row(ROW_POST)], axis=0)
    stack8 = lambda ba, pre, cb, lg, lb, post: jnp.concatenate([ba.reshape(3, D), pre, cb, lg, lb, post], axis=0)
    d8, m8, v8 = _adam_small(
        g8, stack8(b_ada, pre_norm_g, conv_b, conv_ln_g, conv_ln_b, post_norm_g),
        stack8(m_b_ada, m_pre_norm_g, m_conv_b, m_conv_ln_g, m_conv_ln_b, m_post_norm_g),
        stack8(v_b_ada, v_pre_norm_g, v_conv_b, v_conv_ln_g, v_conv_ln_b, v_post_norm_g), "adam_small")
    unstack8 = lambda a: {"b_ada": a[0:3].reshape(3 * D), "pre_norm_g": a[3], "conv_b": a[4],
                          "conv_ln_g": a[5], "conv_ln_b": a[6], "post_norm_g": a[7]}
    g_small, d_small, m_small, v_small = unstack8(g8), unstack8(d8), unstack8(m8), unstack8(v8)

    pad_row = lambda a: jnp.pad(a[0], ((0, 1), (0, 0)))
    g_cw = lax.dynamic_slice(sm[ROW_CONV_W:ROW_CONV_W + CONV_W + 1], (0, me * cw_w), (CONV_W + 1, cw_w))
    d_cw, m_cw, v_cw = _adam_small(g_cw, pad_row(conv_w), pad_row(m_conv_w), pad_row(v_conv_w), "adam_conv_w")

    dmod_all = jnp.concatenate([sm_g[:, ROW_SHIFT], sm_g[:, ROW_SCALE], sm_g[:, ROW_GATE]], axis=1)
    g_wa = _grad_w_ada(c_all.T, lax.dynamic_slice(dmod_all, (0, me * ada_w), (N_DEV, ada_w)))
    d_wa, m_wa, v_wa = _adam_small(g_wa, w_ada[0], m_w_ada[0], v_w_ada[0], "adam_w_ada")

    grads = {"w_ada": g_wa, "conv_w": g_cw[:CONV_W], **g_small, **{n: t[0] for n, t in big.items()}}
    deltas = {"w_ada": d_wa, "conv_w": d_cw[:CONV_W], **d_small, **{n: t[1] for n, t in big.items()}}
    new_m = {"w_ada": m_wa, "conv_w": m_cw[:CONV_W], **m_small, **{n: t[2] for n, t in big.items()}}
    new_v = {"w_ada": v_wa, "conv_w": v_cw[:CONV_W], **v_small, **{n: t[3] for n, t in big.items()}}
    order = ["w_ada", "b_ada", "pre_norm_g", "w_in", "conv_w", "conv_b", "conv_ln_g", "conv_ln_b", "w_ret_out",
             "w_conv_out", "w_out", "post_norm_g"]
    loss = lax.psum(loss_sums[0, 0], axes)
    out = [loss, grad_x[None]]
    for group in (grads, deltas, new_m, new_v):
        out += [group[n][None] for n in order]
    return tuple(out)
```

```python
import jax
import jax.numpy as jnp
import numpy as np
from jax import lax
from jax.experimental import pallas as pl
from jax.experimental.pallas import tpu as pltpu

F32 = jnp.float32
BF16 = jnp.bfloat16
MESH = pl.DeviceIdType.MESH

N_DEV = 8
D = 1024
HEADS = 4
DK = 256
DV = 512
CHUNK = 128
CONV_W = 31
HALO = 32
IN_W = 11264
W_SHARD = IN_W // N_DEV
ROPE_BASE = 10000.0
GN_EPS = 1e-5
LN_EPS = 1e-5
RMS_EPS = 1e-6
ADAM_LR, ADAM_B1, ADAM_B2, ADAM_EPS, ADAM_WD, ADAM_STEP = 0.001, 0.9, 0.999, 1e-08, 0.01, 10

COL_V, COL_Z, COL_UVAL, COL_UGATE, COL_ZCONV, COL_GA, COL_GB = 2, 4, 6, 7, 8, 9, 10

VMEM_LIMIT = 56 * 1024 * 1024


def _params(**kw):
    return pltpu.CompilerParams(vmem_limit_bytes=VMEM_LIMIT, **kw)


def _sigmoid(z):
    return jax.nn.sigmoid(z)


def _round_bf16(a):
    return a.astype(BF16).astype(F32)


def _dot(a, b):
    return jnp.dot(a, b, preferred_element_type=F32)


def _dot_nt(a, b):
    return lax.dot_general(a, b, (((1,), (1,)), ((), ())), preferred_element_type=F32)


def _dot_tn(a, b):
    return lax.dot_general(a, b, (((0,), (0,)), ((), ())), preferred_element_type=F32)


def _my_place():
    return lax.axis_index("x"), lax.axis_index("y"), lax.axis_index("c")


def _peer(k):
    x, y, c = _my_place()
    px = lax.rem(x + ((k >> 2) & 1), 2)
    py = lax.rem(y + ((k >> 1) & 1), 2)
    pc = lax.rem(c + (k & 1), 2)
    return (px, py, pc), 4 * px + 2 * py + pc


def _exchange(arrs, scatter, name, in_vmem):
    n = len(arrs)

    def body(*refs):
        ins, outs = refs[:n], refs[n:2 * n]
        send_sems, recv_sems, local_sems = refs[2 * n:]
        x, y, c = _my_place()
        me = 4 * x + 2 * y + c
        copies = []
        for i in range(n):
            src = ins[i].at[me] if scatter else ins[i]
            cp = pltpu.make_async_copy(src, outs[i].at[me], local_sems.at[i])
            cp.start()
            copies.append(cp)
        for k in range(1, N_DEV):
            peer, peer_idx = _peer(k)
            for i in range(n):
                src = ins[i].at[peer_idx] if scatter else ins[i]
                cp = pltpu.make_async_remote_copy(
                    src_ref=src, dst_ref=outs[i].at[me], send_sem=send_sems.at[i, k - 1],
                    recv_sem=recv_sems.at[i, k - 1], device_id=peer, device_id_type=MESH)
                cp.start()
                copies.append(cp)
        for cp in copies:
            cp.wait()

    space = pltpu.VMEM if in_vmem else pl.ANY
    out_shape = [jax.ShapeDtypeStruct(a.shape if scatter else (N_DEV,) + a.shape, a.dtype) for a in arrs]
    return pl.pallas_call(
        body, name=name, out_shape=out_shape,
        in_specs=[pl.BlockSpec(memory_space=space)] * n,
        out_specs=[pl.BlockSpec(memory_space=space)] * n,
        scratch_shapes=[pltpu.SemaphoreType.DMA((n, N_DEV - 1)), pltpu.SemaphoreType.DMA((n, N_DEV - 1)),
                        pltpu.SemaphoreType.DMA((n,))],
        compiler_params=_params(has_side_effects=True),
    )(*arrs)


def _remote_copies(ins, lands, send_sems, recv_sems, scatter):
    x, y, c = _my_place()
    me = 4 * x + 2 * y + c
    copies = []
    for k in range(1, N_DEV):
        peer, peer_idx = _peer(k)
        for i in range(len(ins)):
            copies.append(pltpu.make_async_remote_copy(
                src_ref=ins[i].at[peer_idx] if scatter else ins[i], dst_ref=lands[i].at[me],
                send_sem=send_sems.at[(N_DEV - 1) * i + k - 1], recv_sem=recv_sems.at[(N_DEV - 1) * i + k - 1],
                device_id=peer, device_id_type=MESH))
    return copies


_HBM = pl.BlockSpec(memory_space=pltpu.HBM)
_SEM = pl.BlockSpec(memory_space=pltpu.SEMAPHORE)


def _exchange_start(arrs, scatter, name):
    n = len(arrs)
    lands = [lax.empty(a.shape if scatter else (N_DEV,) + a.shape, a.dtype) for a in arrs]

    def body(*refs):
        ins, land_refs, send_sems, recv_sems, token = refs[:n], refs[n:2 * n], refs[2 * n], refs[2 * n + 1], refs[-1]
        for cp in _remote_copies(ins, land_refs, send_sems, recv_sems, scatter):
            cp.start()
        token[...] = jnp.zeros_like(token)

    n_sem = (N_DEV - 1) * n
    hbm = lambda a: pltpu.HBM(a.shape, a.dtype)
    outs = pl.pallas_call(
        body, name=name,
        out_shape=(pltpu.SemaphoreType.DMA((n_sem,)), pltpu.SemaphoreType.DMA((n_sem,)), *[hbm(a) for a in arrs],
                   *[hbm(a) for a in lands], jax.ShapeDtypeStruct((8, 128), F32)),
        in_specs=(_HBM,) * (2 * n), out_specs=(_SEM, _SEM) + (_HBM,) * (2 * n) + (pl.BlockSpec(memory_space=pltpu.VMEM),),
        input_output_aliases={i: 2 + i for i in range(2 * n)},
        compiler_params=pltpu.CompilerParams(has_side_effects=pltpu.SideEffectType.DATAFLOW_SIDE_EFFECTING),
    )(*[pltpu.with_memory_space_constraint(a, pltpu.HBM) for a in list(arrs) + lands])
    return outs[:-1], outs[-1]


def _exchange_wait(handles, after, scatter, name):
    n = (len(handles) - 2) // 2
    send_sems, recv_sems = handles[:2]

    def body(*refs):
        ins, land_refs, s_sems, r_sems = refs[:n], refs[n:2 * n], refs[2 * n], refs[2 * n + 1]
        for cp in _remote_copies(ins, land_refs, s_sems, r_sems, scatter):
            cp.wait_send()
            cp.wait_recv()

    bufs = handles[2:]
    outs = pl.pallas_call(
        body, name=name, out_shape=tuple(pltpu.HBM(a.shape, a.dtype) for a in bufs),
        in_specs=(_HBM,) * (2 * n) + (_SEM, _SEM, pl.BlockSpec(memory_space=pl.ANY)), out_specs=(_HBM,) * (2 * n),
        input_output_aliases={i: i for i in range(2 * n)},
        compiler_params=pltpu.CompilerParams(has_side_effects=pltpu.SideEffectType.DATAFLOW_SIDE_EFFECTING),
    )(*bufs, send_sems, recv_sems, after)
    return outs[:n], outs[n:]


def _rope_tables(positions, S):
    tm = min(S, 1024)

    half = DK // 2
    inv_freq = (ROPE_BASE ** (-jnp.arange(half, dtype=F32) / half)).reshape(1, half)

    def body(pos_ref, f_ref, cos_ref, sin_ref):
        ang = pos_ref[...].astype(F32) * f_ref[...]
        cos_ref[...] = jnp.cos(ang)
        sin_ref[...] = jnp.sin(ang)

    return pl.pallas_call(
        body, name="rope_tables", grid=(S // tm,),
        in_specs=[pl.BlockSpec((tm, 1), lambda i: (i, 0)), pl.BlockSpec((1, half), lambda i: (0, 0))],
        out_specs=[pl.BlockSpec((tm, half), lambda i: (i, 0))] * 2,
        out_shape=[jax.ShapeDtypeStruct((S, half), F32)] * 2,
    )(positions.reshape(S, 1), inv_freq)


def _mod_part(c_all, w_ada, b_part):
    def body(c_ref, w_ref, b_ref, o_ref):
        o_ref[...] = _dot(c_ref[...].astype(BF16), w_ref[...].astype(BF16)) + b_ref[...]

    return pl.pallas_call(body, name="mod_part",
                          out_shape=jax.ShapeDtypeStruct((N_DEV, w_ada.shape[1]), F32))(c_all, w_ada, b_part)


def _sum_devices(g):
    def body(g_ref, o_ref):
        acc = g_ref[0]
        for d in range(1, N_DEV):
            acc = acc + g_ref[d]
        o_ref[...] = acc

    return pl.pallas_call(body, name="sum_devices", out_shape=jax.ShapeDtypeStruct(g.shape[1:], F32))(g)


def _grad_w_ada(c_t, dmod):
    def body(c_ref, d_ref, o_ref):
        acc = c_ref[:, 0:1] * d_ref[0:1, :]
        for b in range(1, N_DEV):
            acc = acc + c_ref[:, b:b + 1] * d_ref[b:b + 1, :]
        o_ref[...] = acc

    return pl.pallas_call(body, name="grad_w_ada",
                          out_shape=jax.ShapeDtypeStruct((c_t.shape[0], dmod.shape[1]), F32))(c_t, dmod)


def _adam_math(w, g, m, v):
    m = ADAM_B1 * m + (1.0 - ADAM_B1) * g
    v = ADAM_B2 * v + (1.0 - ADAM_B2) * (g * g)
    m_hat = m / (1.0 - ADAM_B1 ** ADAM_STEP)
    v_hat = v / (1.0 - ADAM_B2 ** ADAM_STEP)
    delta = -ADAM_LR * (m_hat / (jnp.sqrt(v_hat) + ADAM_EPS) + ADAM_WD * w)
    return delta, m, v


def _adam_small(g, w, m, v, name):
    def body(g_ref, w_ref, m_ref, v_ref, d_ref, nm_ref, nv_ref):
        d_ref[...], nm_ref[...], nv_ref[...] = _adam_math(w_ref[...], g_ref[...], m_ref[...], v_ref[...])

    return pl.pallas_call(body, name=name, out_shape=[jax.ShapeDtypeStruct(w.shape, F32)] * 3)(g, w, m, v)


def _adam_shard(parts, w, m, v, name, tr):
    R, L = w.shape

    def body(p_ref, w_ref, m_ref, v_ref, g_ref, d_ref, nm_ref, nv_ref):
        g = p_ref[0].astype(F32)
        for d in range(1, N_DEV):
            g = g + p_ref[d].astype(F32)
        g_ref[...] = g
        d_ref[...], nm_ref[...], nv_ref[...] = _adam_math(w_ref[...], g, m_ref[...], v_ref[...])

    blk = pl.BlockSpec((tr, L), lambda i: (i, 0))
    return pl.pallas_call(
        body, name=name, grid=(R // tr,),
        in_specs=[pl.BlockSpec((N_DEV, tr, L), lambda i: (0, i, 0)), blk, blk, blk],
        out_specs=[blk] * 4, out_shape=[jax.ShapeDtypeStruct((R, L), F32)] * 4,
        compiler_params=_params(),
    )(parts, w, m, v)


def _inproj_fwd(x, g_pre, scale, shift, w_in_g, tm):
    S = x.shape[0]

    def body(x_ref, g_ref, sc_ref, sh_ref, w_ref, proj_ref, h_ref, h_scr):
        @pl.when(pl.program_id(1) == 0)
        def _():
            xf = x_ref[...]
            r = lax.rsqrt(jnp.mean(xf * xf, axis=-1, keepdims=True) + RMS_EPS)
            h = (xf * r * g_ref[...]) * (1.0 + sc_ref[...]) + sh_ref[...]
            h_scr[...] = h.astype(BF16)
            h_ref[...] = h.astype(BF16)

        proj_ref[...] = _dot(h_scr[...], w_ref[...])

    vec = pl.BlockSpec((1, D), lambda i, j: (0, 0))
    return pl.pallas_call(
        body, name="inproj_fwd", grid=(S // tm, N_DEV),
        in_specs=[pl.BlockSpec((tm, D), lambda i, j: (i, 0)), vec, vec, vec,
                  pl.BlockSpec((None, D, W_SHARD), lambda i, j: (j, 0, 0))],
        out_specs=[pl.BlockSpec((tm, W_SHARD), lambda i, j: (i, j)), pl.BlockSpec((tm, D), lambda i, j: (i, 0))],
        out_shape=[jax.ShapeDtypeStruct((S, IN_W), F32), jax.ShapeDtypeStruct((S, D), BF16)],
        scratch_shapes=[pltpu.VMEM((tm, D), BF16)],
        compiler_params=_params(),
    )(x, g_pre, scale, shift, w_in_g)


def _decay_tables():
    log_g = jnp.log1p(-jnp.exp2(-5.0 - jnp.arange(HEADS, dtype=F32)))
    idx = jnp.arange(CHUNK, dtype=F32)
    diff = idx[:, None] - idx[None, :]
    causal = diff >= 0
    mask = jnp.where(causal, jnp.exp(log_g[:, None, None] * jnp.where(causal, diff, 0.0)), 0.0)
    xi = jnp.exp(log_g[:, None] * (idx + 1.0))[:, :, None]
    zeta = jnp.exp(log_g[:, None] * (CHUNK - 1.0 - idx))[:, :, None]
    g_chunk = jnp.broadcast_to(jnp.exp(log_g * CHUNK)[:, None, None], (HEADS, 1, 128))
    return mask, xi, zeta, g_chunk


def _rot(t, cos, sin):
    t1, t2 = t[:, :DK // 2], t[:, DK // 2:]
    return jnp.concatenate([t1 * cos - t2 * sin, t1 * sin + t2 * cos], axis=-1)


def _rot_back(t, cos, sin):
    t1, t2 = t[:, :DK // 2], t[:, DK // 2:]
    return jnp.concatenate([t1 * cos + t2 * sin, t2 * cos - t1 * sin], axis=-1)


def _silu(z):
    return z * _sigmoid(z)


def _retention_fwd(proj, cos, sin, tables, cpb):
    S = proj.shape[0]
    tb = cpb * CHUNK
    mask, xi, zeta, g_chunk = tables

    def body(q_ref, k_ref, v_ref, z_ref, cos_ref, sin_ref, mask_ref, xi_ref, zeta_ref, gc_ref,
             pa_ref, o_ref, st_ref, state):
        @pl.when(pl.program_id(1) == 0)
        def _():
            state[...] = jnp.zeros_like(state)

        m, xi_c, zeta_c, gc = mask_ref[...], xi_ref[...], zeta_ref[...], gc_ref[0:1, 0:1]
        for ci in range(cpb):
            rows = pl.ds(ci * CHUNK, CHUNK)
            cs, sn = cos_ref[rows, :], sin_ref[rows, :]
            qr = _rot(q_ref[rows, :], cs, sn)
            kr = _rot(k_ref[rows, :], cs, sn) * (DK ** -0.5)
            vb = v_ref[rows, :].astype(BF16)
            qb = qr.astype(BF16)
            st = state[...]
            stb = st.astype(BF16)
            st_ref[ci] = stb
            scores = (_dot_nt(qb, kr.astype(BF16)) * m).astype(BF16)
            o = _dot(scores, vb) + _dot(qb, stb) * xi_c
            state[...] = st * gc + _dot_tn((kr * zeta_c).astype(BF16), vb)
            o_ref[rows, :] = o
            mu = jnp.mean(o, axis=-1, keepdims=True)
            oc = o - mu
            var = jnp.mean(oc * oc, axis=-1, keepdims=True)
            pa_ref[rows, :] = (oc * lax.rsqrt(var + GN_EPS) * _silu(z_ref[rows, :])).astype(BF16)

    tab = lambda shape: pl.BlockSpec((None,) + shape, lambda h, n: (h, 0, 0))
    return pl.pallas_call(
        body, name="retention_fwd", grid=(HEADS, S // tb),
        in_specs=[pl.BlockSpec((tb, DK), lambda h, n: (n, h)),
                  pl.BlockSpec((tb, DK), lambda h, n: (n, HEADS + h)),
                  pl.BlockSpec((tb, DV), lambda h, n: (n, COL_V * 2 + h)),
                  pl.BlockSpec((tb, DV), lambda h, n: (n, COL_Z * 2 + h)),
                  pl.BlockSpec((tb, DK // 2), lambda h, n: (n, 0)),
                  pl.BlockSpec((tb, DK // 2), lambda h, n: (n, 0)),
                  tab((CHUNK, CHUNK)), tab((CHUNK, 1)), tab((CHUNK, 1)), tab((1, 128))],
        out_specs=[pl.BlockSpec((tb, DV), lambda h, n: (n, h)),
                   pl.BlockSpec((tb, DV), lambda h, n: (n, h)),
                   pl.BlockSpec((None, cpb, DK, DV), lambda h, n: (h, n, 0, 0))],
        out_shape=[jax.ShapeDtypeStruct((S, HEADS * DV), BF16), jax.ShapeDtypeStruct((S, HEADS * DV), F32),
                   jax.ShapeDtypeStruct((HEADS, S // CHUNK, DK, DV), BF16)],
        scratch_shapes=[pltpu.VMEM((DK, DV), F32)],
        compiler_params=_params(),
    )(proj, proj, proj, proj, cos, sin, mask, xi, zeta, g_chunk)


SUBLANES = 8
FWD_ROWS = 16
BWD_ROWS = 8


def _shifted_copies(src, dst):
    rows = dst.shape[1]
    src[pl.ds(rows, SUBLANES), :] = jnp.zeros((SUBLANES, src.shape[1]), src.dtype)
    for s in range(SUBLANES):
        dst[s] = src[pl.ds(s, rows), :]


def _layernorm_stats(a):
    mu = jnp.mean(a, axis=-1, keepdims=True)
    ac = a - mu
    var = jnp.mean(ac * ac, axis=-1, keepdims=True)
    rstd = lax.rsqrt(var + LN_EPS)
    return ac * rstd, rstd


def _conv_fwd(proj, conv_w, conv_b, ln_g, ln_b, tm):
    S = proj.shape[0]
    hb = tm // HALO

    def body(uv_ref, ug_ref, uvh_ref, ugh_ref, z_ref, w_ref, b_ref, g_ref, bb_ref, pb_ref, a1_ref, ext, ex):
        halo = _round_bf16(uvh_ref[...] * _sigmoid(ugh_ref[...]))
        ext[pl.ds(0, HALO), :] = jnp.where(pl.program_id(0) == 0, 0.0, halo)
        ext[pl.ds(HALO, tm), :] = _round_bf16(uv_ref[...] * _sigmoid(ug_ref[...]))
        _shifted_copies(ext, ex)
        bias = b_ref[...]

        def rows_body(rb, carry):
            r0 = pl.multiple_of(rb * FWD_ROWS, FWD_ROWS)
            groups = range(FWD_ROWS // SUBLANES)
            acc = [jnp.zeros((SUBLANES, D), F32) for _ in groups]
            for j in range(CONV_W):
                a, s = divmod(HALO - (CONV_W - 1) + j, SUBLANES)
                wj = w_ref[j]
                for k in groups:
                    acc[k] = acc[k] + wj * ex[s, pl.ds(r0 + (a + k) * SUBLANES, SUBLANES), :]
            for k in groups:
                a1_ref[pl.ds(r0 + k * SUBLANES, SUBLANES), :] = acc[k] + bias
            return carry

        lax.fori_loop(0, tm // FWD_ROWS, rows_body, 0)
        xhat, _ = _layernorm_stats(a1_ref[...])
        a3 = _silu(xhat * g_ref[...] + bb_ref[...])
        pb_ref[...] = (a3 * _silu(z_ref[...])).astype(BF16)

    vec = pl.BlockSpec((1, D), lambda i: (0, 0))
    halo = lambda col: pl.BlockSpec((HALO, D), lambda i: (jnp.maximum(i * hb - 1, 0), col))
    return pl.pallas_call(
        body, name="conv_fwd", grid=(S // tm,),
        in_specs=[pl.BlockSpec((tm, D), lambda i: (i, COL_UVAL)), pl.BlockSpec((tm, D), lambda i: (i, COL_UGATE)),
                  halo(COL_UVAL), halo(COL_UGATE), pl.BlockSpec((tm, D), lambda i: (i, COL_ZCONV)),
                  pl.BlockSpec((CONV_W + 1, SUBLANES, D), lambda i: (0, 0, 0)), vec, vec, vec],
        out_specs=[pl.BlockSpec((tm, D), lambda i: (i, 0))] * 2,
        out_shape=[jax.ShapeDtypeStruct((S, D), BF16), jax.ShapeDtypeStruct((S, D), F32)],
        scratch_shapes=[pltpu.VMEM((HALO + tm + SUBLANES, D), F32), pltpu.VMEM((SUBLANES, HALO + tm, D), F32)],
        compiler_params=_params(),
    )(proj, proj, proj, proj, proj, conv_w, conv_b, ln_g, ln_b)


def _head(p_a, p_b, proj, x, target, gate, g_post, w_ret, w_conv, w_out, tm):
    S = x.shape[0]

    def body(pa_ref, pb_ref, ga_ref, gb_ref, x_ref, t_ref, gate_ref, gp_ref, wr_ref, wc_ref, wo_ref,
             loss_ref, dout_ref, dg_ref, dpa_ref, dpb_ref, mb_ref, dy_ref, dya_ref, dyb_ref, small_ref):
        @pl.when(pl.program_id(0) == 0)
        def _():
            loss_ref[...] = jnp.zeros_like(loss_ref)
            small_ref[...] = jnp.zeros_like(small_ref)

        gate, gp = gate_ref[...], gp_ref[...]
        ya = _dot(pa_ref[...], wr_ref[...])
        yb = _dot(pb_ref[...], wc_ref[...])
        sa, sb = _sigmoid(ga_ref[...]), _sigmoid(gb_ref[...])
        mb = (sa * ya + sb * yb).astype(BF16)
        mb_ref[...] = mb
        y = _dot(mb, wo_ref[...])
        r = lax.rsqrt(jnp.mean(y * y, axis=-1, keepdims=True) + RMS_EPS)
        yhat = y * r
        yn = yhat * gp
        err = (x_ref[...] + gate * yn) - t_ref[...]
        loss_ref[...] += 0.5 * jnp.sum(jnp.mean(err * err, axis=-1, keepdims=True))
        dout = err * (1.0 / D)
        dout_ref[...] = dout
        small_ref[0:1, :] += jnp.sum(dout * yn, axis=0, keepdims=True)
        dyn = dout * gate
        small_ref[1:2, :] += jnp.sum(dyn * yhat, axis=0, keepdims=True)
        dyhat = dyn * gp
        dy = (r * (dyhat - yhat * jnp.mean(dyhat * yhat, axis=-1, keepdims=True))).astype(BF16)
        dy_ref[...] = dy
        dmerged = _dot_nt(dy, wo_ref[...])
        dya = dmerged * sa
        dyb = dmerged * sb
        dg_ref[:, 0:D] = (dya * ya * (1.0 - sa)).astype(BF16)
        dg_ref[:, D:2 * D] = (dyb * yb * (1.0 - sb)).astype(BF16)
        dya, dyb = dya.astype(BF16), dyb.astype(BF16)
        dya_ref[...] = dya
        dyb_ref[...] = dyb
        dpa_ref[...] = _dot_nt(dya, wr_ref[...])
        dpb_ref[...] = _dot_nt(dyb, wc_ref[...])

    vec = pl.BlockSpec((1, D), lambda i: (0, 0))
    tile = lambda w, col=0: pl.BlockSpec((tm, w), lambda i: (i, col))
    whole = lambda a: pl.BlockSpec(a.shape, lambda i: (0, 0), pipeline_mode=pl.Buffered(1))
    act = lambda w: jax.ShapeDtypeStruct((S, w), BF16)
    return pl.pallas_call(
        body, name="head", grid=(S // tm,),
        in_specs=[tile(2 * D), tile(D), tile(D, COL_GA), tile(D, COL_GB), tile(D), tile(D), vec, vec,
                  whole(w_ret), whole(w_conv), whole(w_out)],
        out_specs=[pl.BlockSpec((8, 128), lambda i: (0, 0)), tile(D), tile(2 * D), tile(2 * D), tile(D),
                   tile(D), tile(D), tile(D), tile(D), pl.BlockSpec((8, D), lambda i: (0, 0))],
        out_shape=[jax.ShapeDtypeStruct((8, 128), F32), jax.ShapeDtypeStruct((S, D), F32),
                   act(2 * D), jax.ShapeDtypeStruct((S, 2 * D), F32), jax.ShapeDtypeStruct((S, D), F32),
                   act(D), act(D), act(D), act(D), jax.ShapeDtypeStruct((8, D), F32)],
        compiler_params=_params(),
    )(p_a, p_b, proj, proj, x, target, gate, g_post, w_ret, w_conv, w_out)


def _matmul_tn(a, b, name, bn, tk):
    S, M = a.shape
    N = b.shape[1]

    def body(a_ref, b_ref, o_ref):
        @pl.when(pl.program_id(1) == 0)
        def _():
            o_ref[...] = jnp.zeros_like(o_ref)

        o_ref[...] += _dot_tn(a_ref[...], b_ref[...])

    return pl.pallas_call(
        body, name=name, grid=(N // bn, S // tk),
        in_specs=[pl.BlockSpec((tk, M), lambda j, k: (k, 0)), pl.BlockSpec((tk, bn), lambda j, k: (k, j))],
        out_specs=pl.BlockSpec((M, bn), lambda j, k: (0, j)),
        out_shape=jax.ShapeDtypeStruct((M, N), F32),
        compiler_params=_params(),
    )(a, b)


SMALL_B_ROWS = 40


def _dsilu(z, s):
    return s * (1.0 + z * (1.0 - s))


def _conv_bwd(dp_b, proj, a1, conv_w, ln_g, ln_b, tm):
    S = proj.shape[0]
    nt = S // tm

    def body(dpb_ref, uv_ref, ug_ref, z_ref, a1_ref, w_ref, g_ref, bb_ref,
             dc_ref, small_ref, dext, dx, carry, a0_buf, da0_buf, dw_acc):
        @pl.when(pl.program_id(0) == 0)
        def _():
            small_ref[...] = jnp.zeros_like(small_ref)
            carry[...] = jnp.zeros_like(carry)

        xhat, rstd = _layernorm_stats(a1_ref[...])
        g = g_ref[...]
        a2 = xhat * g + bb_ref[...]
        s2 = _sigmoid(a2)
        zc = z_ref[...]
        sz = _sigmoid(zc)
        dpb = dpb_ref[...]
        dc_ref[:, 2 * D:3 * D] = (dpb * (a2 * s2) * _dsilu(zc, sz)).astype(BF16)
        da2 = dpb * (zc * sz) * _dsilu(a2, s2)
        small_ref[32:33, :] += jnp.sum(da2 * xhat, axis=0, keepdims=True)
        small_ref[33:34, :] += jnp.sum(da2, axis=0, keepdims=True)
        dxhat = da2 * g
        da1 = rstd * (dxhat - jnp.mean(dxhat, axis=-1, keepdims=True)
                      - xhat * jnp.mean(dxhat * xhat, axis=-1, keepdims=True))
        small_ref[31:32, :] += jnp.sum(da1, axis=0, keepdims=True)
        da1 = _round_bf16(da1)
        dext[pl.ds(0, tm), :] = da1
        dext[pl.ds(tm, HALO), :] = carry[...]
        carry[...] = da1[0:HALO, :]
        _shifted_copies(dext, dx)
        a0_buf[...] = _round_bf16(uv_ref[...] * _sigmoid(ug_ref[...]))
        dw_acc[...] = jnp.zeros_like(dw_acc)

        def rows_body(rb, c):
            r0 = pl.multiple_of(rb * BWD_ROWS, BWD_ROWS)
            groups = range(BWD_ROWS // SUBLANES)
            a0 = [a0_buf[pl.ds(r0 + k * SUBLANES, SUBLANES), :] for k in groups]
            acc = [jnp.zeros((SUBLANES, D), F32) for _ in groups]
            for j in range(CONV_W):
                a, s = divmod(CONV_W - 1 - j, SUBLANES)
                wj = w_ref[j]
                dw = dw_acc[j]
                for k in groups:
                    win = dx[s, pl.ds(r0 + (a + k) * SUBLANES, SUBLANES), :]
                    acc[k] = acc[k] + wj * win
                    dw = dw + a0[k] * win
                dw_acc[j] = dw
            for k in groups:
                da0_buf[pl.ds(r0 + k * SUBLANES, SUBLANES), :] = acc[k]
            return c

        lax.fori_loop(0, tm // BWD_ROWS, rows_body, 0)
        small_ref[0:CONV_W + 1, :] += jnp.sum(dw_acc[...], axis=1)
        da0 = da0_buf[...]
        sg = _sigmoid(ug_ref[...])
        dc_ref[:, 0:D] = (da0 * sg).astype(BF16)
        dc_ref[:, D:2 * D] = (da0 * uv_ref[...] * sg * (1.0 - sg)).astype(BF16)

    vec = pl.BlockSpec((1, D), lambda i: (0, 0))
    tile = lambda col: pl.BlockSpec((tm, D), lambda i: (nt - 1 - i, col))
    return pl.pallas_call(
        body, name="conv_bwd", grid=(nt,),
        in_specs=[tile(0), tile(COL_UVAL), tile(COL_UGATE), tile(COL_ZCONV), tile(0),
                  pl.BlockSpec((CONV_W + 1, SUBLANES, D), lambda i: (0, 0, 0)), vec, vec],
        out_specs=[pl.BlockSpec((tm, 3 * D), lambda i: (nt - 1 - i, 0)),
                   pl.BlockSpec((SMALL_B_ROWS, D), lambda i: (0, 0))],
        out_shape=[jax.ShapeDtypeStruct((S, 3 * D), BF16), jax.ShapeDtypeStruct((SMALL_B_ROWS, D), F32)],
        scratch_shapes=[pltpu.VMEM((tm + HALO + SUBLANES, D), F32), pltpu.VMEM((SUBLANES, tm + HALO, D), F32),
                        pltpu.VMEM((HALO, D), F32), pltpu.VMEM((tm, D), F32), pltpu.VMEM((tm, D), F32),
                        pltpu.VMEM((CONV_W + 1, SUBLANES, D), F32)],
        compiler_params=_params(),
    )(dp_b, proj, proj, proj, a1, conv_w, ln_g, ln_b)


RET_W = 2 * DK + 2 * DV


def _retention_bwd(dp_a, o, proj, states, cos, sin, tables, cpb):
    S = proj.shape[0]
    tb = cpb * CHUNK
    nb = S // tb
    mask, xi, zeta, g_chunk = tables

    def body(dpa_ref, o_ref, q_ref, k_ref, v_ref, z_ref, st_ref, cos_ref, sin_ref, mask_ref, xi_ref, zeta_ref,
             gc_ref, out_ref, dstate):
        @pl.when(pl.program_id(1) == 0)
        def _():
            dstate[...] = jnp.zeros_like(dstate)

        m, xi_c, zeta_c, gc = mask_ref[...], xi_ref[...], zeta_ref[...], gc_ref[0:1, 0:1]
        for ci in reversed(range(cpb)):
            rows = pl.ds(ci * CHUNK, CHUNK)
            o = o_ref[rows, :]
            oc = o - jnp.mean(o, axis=-1, keepdims=True)
            rstd = lax.rsqrt(jnp.mean(oc * oc, axis=-1, keepdims=True) + GN_EPS)
            rhat = oc * rstd
            z = z_ref[rows, :]
            sz = _sigmoid(z)
            dpa = dpa_ref[rows, :]
            out_ref[rows, 2 * DK + DV:RET_W] = (dpa * rhat * _dsilu(z, sz)).astype(BF16)
            dret = dpa * (z * sz)
            d_o = rstd * (dret - jnp.mean(dret, axis=-1, keepdims=True)
                          - rhat * jnp.mean(dret * rhat, axis=-1, keepdims=True))
            cs, sn = cos_ref[rows, :], sin_ref[rows, :]
            qb = _rot(q_ref[rows, :], cs, sn).astype(BF16)
            kr = _rot(k_ref[rows, :], cs, sn) * (DK ** -0.5)
            kb = kr.astype(BF16)
            kz = (kr * zeta_c).astype(BF16)
            vb = v_ref[rows, :].astype(BF16)
            scores = (_dot_nt(qb, kb) * m).astype(BF16)
            dob = d_o.astype(BF16)
            dox = (d_o * xi_c).astype(BF16)
            dscores = (_dot_nt(dob, vb) * m).astype(BF16)
            dst = dstate[...]
            dstb = dst.astype(BF16)
            dqr = _dot(dscores, kb) + _dot_nt(dox, st_ref[ci])
            dkr = _dot_tn(dscores, qb) + _dot_nt(vb, dstb) * zeta_c
            dv = _dot_tn(scores, dob) + _dot(kz, dstb)
            dstate[...] = dst * gc + _dot_tn(qb, dox)
            out_ref[rows, 0:DK] = _rot_back(dqr, cs, sn).astype(BF16)
            out_ref[rows, DK:2 * DK] = (_rot_back(dkr, cs, sn) * (DK ** -0.5)).astype(BF16)
            out_ref[rows, 2 * DK:2 * DK + DV] = dv.astype(BF16)

    tab = lambda shape: pl.BlockSpec((None,) + shape, lambda h, n: (h, 0, 0))
    blk = lambda w, col0: pl.BlockSpec((tb, w), lambda h, n: (nb - 1 - n, col0 + h))
    return pl.pallas_call(
        body, name="retention_bwd", grid=(HEADS, nb),
        in_specs=[blk(DV, 0), blk(DV, 0), blk(DK, 0), blk(DK, HEADS), blk(DV, COL_V * 2), blk(DV, COL_Z * 2),
                  pl.BlockSpec((None, cpb, DK, DV), lambda h, n: (h, nb - 1 - n, 0, 0)),
                  pl.BlockSpec((tb, DK // 2), lambda h, n: (nb - 1 - n, 0)),
                  pl.BlockSpec((tb, DK // 2), lambda h, n: (nb - 1 - n, 0)),
                  tab((CHUNK, CHUNK)), tab((CHUNK, 1)), tab((CHUNK, 1)), tab((1, 128))],
        out_specs=blk(RET_W, 0),
        out_shape=jax.ShapeDtypeStruct((S, HEADS * RET_W), BF16),
        scratch_shapes=[pltpu.VMEM((DK, DV), F32)],
        compiler_params=_params(),
    )(dp_a, o, proj, proj, proj, proj, states, cos, sin, mask, xi, zeta, g_chunk)


def _to_head_major(w):
    parts = []
    for h in range(HEADS):
        parts += [w[:, h * DK:(h + 1) * DK], w[:, D + h * DK:D + (h + 1) * DK],
                  w[:, 2 * D + h * DV:2 * D + (h + 1) * DV], w[:, 4 * D + h * DV:4 * D + (h + 1) * DV]]
    return jnp.concatenate(parts, axis=1)


def _from_head_major(w):
    starts = (0, DK, 2 * DK, 2 * DK + DV)
    widths = (DK, DK, DV, DV)
    parts = [w[:, h * RET_W + s:h * RET_W + s + n] for s, n in zip(starts, widths) for h in range(HEADS)]
    return jnp.concatenate(parts, axis=1)


def _dh(d_r, d_c, d_g, w_r, w_c, w_g, x, dout, g_pre, scale, tm):
    S = x.shape[0]

    def body(dr_ref, dc_ref, dg_ref, wr_ref, wc_ref, wg_ref, x_ref, dout_ref, g_ref, sc_ref, gx_ref, small_ref):
        @pl.when(pl.program_id(0) == 0)
        def _():
            small_ref[...] = jnp.zeros_like(small_ref)

        dh = _dot_nt(dr_ref[...], wr_ref[...]) + _dot_nt(dc_ref[...], wc_ref[...]) + _dot_nt(dg_ref[...], wg_ref[...])
        xf = x_ref[...]
        r = lax.rsqrt(jnp.mean(xf * xf, axis=-1, keepdims=True) + RMS_EPS)
        xhat = xf * r
        g, sc1 = g_ref[...], 1.0 + sc_ref[...]
        small_ref[0:1, :] += jnp.sum(dh, axis=0, keepdims=True)
        small_ref[1:2, :] += jnp.sum(dh * (xhat * g), axis=0, keepdims=True)
        small_ref[2:3, :] += jnp.sum(dh * sc1 * xhat, axis=0, keepdims=True)
        dxhat = dh * sc1 * g
        gx_ref[...] = dout_ref[...] + r * (dxhat - xhat * jnp.mean(dxhat * xhat, axis=-1, keepdims=True))

    vec = pl.BlockSpec((1, D), lambda i: (0, 0))
    tile = lambda w: pl.BlockSpec((tm, w), lambda i: (i, 0))
    whole = lambda a: pl.BlockSpec(a.shape, lambda i: (0, 0), pipeline_mode=pl.Buffered(1))
    return pl.pallas_call(
        body, name="dh", grid=(S // tm,),
        in_specs=[tile(d_r.shape[1]), tile(d_c.shape[1]), tile(d_g.shape[1]), whole(w_r), whole(w_c), whole(w_g),
                  tile(D), tile(D), vec, vec],
        out_specs=[tile(D), pl.BlockSpec((8, D), lambda i: (0, 0))],
        out_shape=[jax.ShapeDtypeStruct((S, D), F32), jax.ShapeDtypeStruct((8, D), F32)],
        compiler_params=_params(),
    )(d_r, d_c, d_g, w_r, w_c, w_g, x, dout, g_pre, scale)


ROW_GATE, ROW_POST = 0, 1
ROW_CONV_W, ROW_CONV_B, ROW_LN_G, ROW_LN_B = 8, 8 + 31, 8 + 32, 8 + 33
ROW_SHIFT, ROW_SCALE, ROW_PRE = 48, 49, 50


def kernel(x, c, positions, w_ada, b_ada, pre_norm_g, w_in, conv_w, conv_b, conv_ln_g, conv_ln_b, w_ret_out, w_conv_out, w_out, post_norm_g, loss_target, m_w_ada, m_b_ada, m_pre_norm_g, m_w_in, m_conv_w, m_conv_b, m_conv_ln_g, m_conv_ln_b, m_w_ret_out, m_w_conv_out, m_w_out, m_post_norm_g, v_w_ada, v_b_ada, v_pre_norm_g, v_w_in, v_conv_w, v_conv_b, v_conv_ln_g, v_conv_ln_b, v_w_ret_out, v_w_conv_out, v_w_out, v_post_norm_g):
    S = x.shape[1]
    axes = ("x", "y", "c")
    me = 4 * lax.axis_index("x") + 2 * lax.axis_index("y") + lax.axis_index("c")
    x2, target = x[0], loss_target[0]
    ada_w = w_ada.shape[2]
    cw_w = conv_w.shape[2]

    own_in = w_in[0].astype(BF16)
    h_in, tok_in = _exchange_start([own_in], False, "gather_w_in_start")
    own_rest = [(w_ret_out[0] + tok_in[0:1, 0:1]).astype(BF16), w_conv_out[0].astype(BF16), w_out[0].astype(BF16)]
    h_rest, tok_rest = _exchange_start(own_rest, False, "gather_w_rest_start")

    c_g, conv_w_g = _exchange([jnp.pad(c + tok_rest[0:1, 0:1], ((0, 7), (0, 0))), jnp.pad(conv_w[0], ((0, 1), (0, 0)))],
                              False, "gather_c_conv_w", True)
    c_all = c_g[:, 0, :]
    conv_w_full = _round_bf16(conv_w_g.transpose(1, 0, 2).reshape(CONV_W + 1, D))
    conv_w_full = jnp.broadcast_to(conv_w_full[:, None, :], (CONV_W + 1, SUBLANES, D))
    b_part = lax.dynamic_slice(b_ada, (0, me * ada_w), (1, ada_w))
    mod_g = _exchange([_mod_part(c_all, w_ada[0], b_part)], False, "gather_mod", True)[0]
    mod = lax.dynamic_index_in_dim(mod_g, me, axis=1, keepdims=False).reshape(1, 3 * D)
    shift, scale, gate = mod[:, :D], mod[:, D:2 * D], mod[:, 2 * D:]

    def gathered(handles, after, name):
        sent, zones = _exchange_wait(handles, after, False, name)
        return [lax.dynamic_update_slice(z, own[None], (me,) + (0,) * own.ndim) for z, own in zip(zones, sent)]

    def scattered(handles, after, name):
        sent, zones = _exchange_wait(handles, after, True, name)
        return [lax.dynamic_update_slice(z, lax.dynamic_index_in_dim(p, me, axis=0), (me,) + (0,) * (p.ndim - 1))
                for z, p in zip(zones, sent)]

    cos, sin = _rope_tables(positions, S)
    tables = _decay_tables()
    (g_in,) = gathered(h_in, mod, "gather_w_in_wait")
    w_in_full = g_in.transpose(1, 0, 2).reshape(D, IN_W)
    proj, h = _inproj_fwd(x2, pre_norm_g, scale, shift, g_in, tm=min(S, 1024))
    p_a, o, states = _retention_fwd(proj, cos, sin, tables, cpb=4)
    p_b, a1 = _conv_fwd(proj, conv_w_full, conv_b, conv_ln_g, conv_ln_b, tm=256)
    g_ret, g_conv, g_out = gathered(h_rest, p_b, "gather_w_rest_wait")
    w_ret_full, w_conv_full, w_out_full = g_ret.reshape(2 * D, D), g_conv.reshape(D, D), g_out.reshape(D, D)

    loss_sums, dout, d_g, dp_a, dp_b, merged, dy, dy_a, dy_b, small_a = _head(
        p_a, p_b, proj, x2, target, gate, post_norm_g, w_ret_full, w_conv_full, w_out_full, tm=256)
    parts_rest = [_matmul_tn(p_a, dy_a, "dw_ret_out", bn=D, tk=512).astype(BF16).reshape(N_DEV, 2 * D // N_DEV, D),
                  _matmul_tn(p_b, dy_b, "dw_conv_out", bn=D, tk=512).astype(BF16).reshape(N_DEV, D // N_DEV, D),
                  _matmul_tn(merged, dy, "dw_out", bn=D, tk=512).astype(BF16).reshape(N_DEV, D // N_DEV, D)]
    h_s_rest, tok_s_rest = _exchange_start(parts_rest, True, "scatter_rest_start")
    d_c, small_b = _conv_bwd(dp_b, proj, a1, conv_w_full, conv_ln_g + tok_s_rest[0:1, 0:1], conv_ln_b, tm=256)
    d_r = _retention_bwd(dp_a, o, proj, states, cos, sin, tables, cpb=4)
    dw_r = _matmul_tn(h, d_r, "dw_in_ret", bn=RET_W, tk=512)
    dw_c = _matmul_tn(h, d_c, "dw_in_conv", bn=D, tk=512)
    dw_g = _matmul_tn(h, d_g, "dw_in_gate", bn=D, tk=512)
    dw_in = jnp.concatenate([_from_head_major(dw_r), dw_c, dw_g], axis=1)
    parts_in = dw_in.astype(BF16).reshape(D, N_DEV, W_SHARD).transpose(1, 0, 2)
    h_s_in, tok_s_in = _exchange_start([parts_in], True, "scatter_w_in_start")
    grad_x, small_c = _dh(d_r, d_c, d_g, _to_head_major(w_in_full), w_in_full[:, 6 * D:9 * D], w_in_full[:, 9 * D:],
                          x2, dout, pre_norm_g + tok_s_in[0:1, 0:1], scale, tm=256)

    r_ret, r_conv, r_out = scattered(h_s_rest, small_c, "scatter_rest_wait")
    (r_in,) = scattered(h_s_in, small_c, "scatter_w_in_wait")
    big = {"w_in": _adam_shard(r_in, w_in[0], m_w_in[0], v_w_in[0], "adam_w_in", 128),
           "w_ret_out": _adam_shard(r_ret, w_ret_out[0], m_w_ret_out[0], v_w_ret_out[0], "adam_w_ret_out", 128),
           "w_conv_out": _adam_shard(r_conv, w_conv_out[0], m_w_conv_out[0], v_w_conv_out[0], "adam_w_conv_out", 128),
           "w_out": _adam_shard(r_out, w_out[0], m_w_out[0], v_w_out[0], "adam_w_out", 128)}

    sm_g = _exchange([jnp.concatenate([small_a, small_b, small_c], axis=0)], False, "gather_small", True)[0]
    sm = _sum_devices(sm_g)
    row = lambda r: sm[r:r + 1]
    g8 = jnp.concatenate([row(ROW_SHIFT), row(ROW_SCALE), row(ROW_GATE), row(ROW_PRE), row(ROW_CONV_B),
                          row(ROW_LN_G), row(ROW_LN_B), row(ROW_POST)], axis=0)
    stack8 = lambda ba, pre, cb, lg, lb, post: jnp.concatenate([ba.reshape(3, D), pre, cb, lg, lb, post], axis=0)
    d8, m8, v8 = _adam_small(
        g8, stack8(b_ada, pre_norm_g, conv_b, conv_ln_g, conv_ln_b, post_norm_g),
        stack8(m_b_ada, m_pre_norm_g, m_conv_b, m_conv_ln_g, m_conv_ln_b, m_post_norm_g),
        stack8(v_b_ada, v_pre_norm_g, v_conv_b, v_conv_ln_g, v_conv_ln_b, v_post_norm_g), "adam_small")
    unstack8 = lambda a: {"b_ada": a[0:3].reshape(3 * D), "pre_norm_g": a[3], "conv_b": a[4],
                          "conv_ln_g": a[5], "conv_ln_b": a[6], "post_norm_g": a[7]}
    g_small, d_small, m_small, v_small = unstack8(g8), unstack8(d8), unstack8(m8), unstack8(v8)

    pad_row = lambda a: jnp.pad(a[0], ((0, 1), (0, 0)))
    g_cw = lax.dynamic_slice(sm[ROW_CONV_W:ROW_CONV_W + CONV_W + 1], (0, me * cw_w), (CONV_W + 1, cw_w))
    d_cw, m_cw, v_cw = _adam_small(g_cw, pad_row(conv_w), pad_row(m_conv_w), pad_row(v_conv_w), "adam_conv_w")

    dmod_all = jnp.concatenate([sm_g[:, ROW_SHIFT], sm_g[:, ROW_SCALE], sm_g[:, ROW_GATE]], axis=1)
    g_wa = _grad_w_ada(c_all.T, lax.dynamic_slice(dmod_all, (0, me * ada_w), (N_DEV, ada_w)))
    d_wa, m_wa, v_wa = _adam_small(g_wa, w_ada[0], m_w_ada[0], v_w_ada[0], "adam_w_ada")

    grads = {"w_ada": g_wa, "conv_w": g_cw[:CONV_W], **g_small, **{n: t[0] for n, t in big.items()}}
    deltas = {"w_ada": d_wa, "conv_w": d_cw[:CONV_W], **d_small, **{n: t[1] for n, t in big.items()}}
    new_m = {"w_ada": m_wa, "conv_w": m_cw[:CONV_W], **m_small, **{n: t[2] for n, t in big.items()}}
    new_v = {"w_ada": v_wa, "conv_w": v_cw[:CONV_W], **v_small, **{n: t[3] for n, t in big.items()}}
    order = ["w_ada", "b_ada", "pre_norm_g", "w_in", "conv_w", "conv_b", "conv_ln_g", "conv_ln_b", "w_ret_out",
             "w_conv_out", "w_out", "post_norm_g"]
    loss = lax.psum(loss_sums[0, 0], axes)
    out = [loss, grad_x[None]]
    for group in (grads, deltas, new_m, new_v):
        out += [group[n][None] for n in order]
    return tuple(out)
```

```python
import jax
import jax.numpy as jnp
import numpy as np
from jax import lax
from jax.experimental import pallas as pl
from jax.experimental.pallas import tpu as pltpu

F32 = jnp.float32
BF16 = jnp.bfloat16
MESH = pl.DeviceIdType.MESH

N_DEV = 8
D = 1024
HEADS = 4
DK = 256
DV = 512
CHUNK = 128
CONV_W = 31
HALO = 32
IN_W = 11264
W_SHARD = IN_W // N_DEV
ROPE_BASE = 10000.0
GN_EPS = 1e-5
LN_EPS = 1e-5
RMS_EPS = 1e-6
ADAM_LR, ADAM_B1, ADAM_B2, ADAM_EPS, ADAM_WD, ADAM_STEP = 0.001, 0.9, 0.999, 1e-08, 0.01, 10

COL_V, COL_Z, COL_UVAL, COL_UGATE, COL_ZCONV, COL_GA, COL_GB = 2, 4, 6, 7, 8, 9, 10

VMEM_LIMIT = 56 * 1024 * 1024


def _params(**kw):
    return pltpu.CompilerParams(vmem_limit_bytes=VMEM_LIMIT, **kw)


def _sigmoid(z):
    return jax.nn.sigmoid(z)


def _round_bf16(a):
    return a.astype(BF16).astype(F32)


def _dot(a, b):
    return jnp.dot(a, b, preferred_element_type=F32)


def _dot_nt(a, b):
    return lax.dot_general(a, b, (((1,), (1,)), ((), ())), preferred_element_type=F32)


def _dot_tn(a, b):
    return lax.dot_general(a, b, (((0,), (0,)), ((), ())), preferred_element_type=F32)


def _my_place():
    return lax.axis_index("x"), lax.axis_index("y"), lax.axis_index("c")


def _peer(k):
    x, y, c = _my_place()
    px = lax.rem(x + ((k >> 2) & 1), 2)
    py = lax.rem(y + ((k >> 1) & 1), 2)
    pc = lax.rem(c + (k & 1), 2)
    return (px, py, pc), 4 * px + 2 * py + pc


def _exchange(arrs, scatter, name, in_vmem):
    n = len(arrs)

    def body(*refs):
        ins, outs = refs[:n], refs[n:2 * n]
        send_sems, recv_sems, local_sems = refs[2 * n:]
        x, y, c = _my_place()
        me = 4 * x + 2 * y + c
        copies = []
        for i in range(n):
            src = ins[i].at[me] if scatter else ins[i]
            cp = pltpu.make_async_copy(src, outs[i].at[me], local_sems.at[i])
            cp.start()
            copies.append(cp)
        for k in range(1, N_DEV):
            peer, peer_idx = _peer(k)
            for i in range(n):
                src = ins[i].at[peer_idx] if scatter else ins[i]
                cp = pltpu.make_async_remote_copy(
                    src_ref=src, dst_ref=outs[i].at[me], send_sem=send_sems.at[i, k - 1],
                    recv_sem=recv_sems.at[i, k - 1], device_id=peer, device_id_type=MESH)
                cp.start()
                copies.append(cp)
        for cp in copies:
            cp.wait()

    space = pltpu.VMEM if in_vmem else pl.ANY
    out_shape = [jax.ShapeDtypeStruct(a.shape if scatter else (N_DEV,) + a.shape, a.dtype) for a in arrs]
    return pl.pallas_call(
        body, name=name, out_shape=out_shape,
        in_specs=[pl.BlockSpec(memory_space=space)] * n,
        out_specs=[pl.BlockSpec(memory_space=space)] * n,
        scratch_shapes=[pltpu.SemaphoreType.DMA((n, N_DEV - 1)), pltpu.SemaphoreType.DMA((n, N_DEV - 1)),
                        pltpu.SemaphoreType.DMA((n,))],
        compiler_params=_params(has_side_effects=True),
    )(*arrs)


def _remote_copies(ins, lands, send_sems, recv_sems, scatter):
    x, y, c = _my_place()
    me = 4 * x + 2 * y + c
    copies = []
    for k in range(1, N_DEV):
        peer, peer_idx = _peer(k)
        for i in range(len(ins)):
            copies.append(pltpu.make_async_remote_copy(
                src_ref=ins[i].at[peer_idx] if scatter else ins[i], dst_ref=lands[i].at[me],
                send_sem=send_sems.at[(N_DEV - 1) * i + k - 1], recv_sem=recv_sems.at[(N_DEV - 1) * i + k - 1],
                device_id=peer, device_id_type=MESH))
    return copies


_HBM = pl.BlockSpec(memory_space=pltpu.HBM)
_SEM = pl.BlockSpec(memory_space=pltpu.SEMAPHORE)


def _exchange_start(arrs, scatter, name):
    n = len(arrs)
    me = 4 * lax.axis_index("x") + 2 * lax.axis_index("y") + lax.axis_index("c")
    lands = []
    for a in arrs:
        own = lax.dynamic_index_in_dim(a, me, axis=0) if scatter else a[None]
        zone = lax.empty(a.shape if scatter else (N_DEV,) + a.shape, a.dtype)
        lands.append(lax.dynamic_update_slice(zone, own, (me,) + (0,) * (own.ndim - 1)))

    def body(*refs):
        ins, land_refs, send_sems, recv_sems, token = refs[:n], refs[n:2 * n], refs[2 * n], refs[2 * n + 1], refs[-1]
        for cp in _remote_copies(ins, land_refs, send_sems, recv_sems, scatter):
            cp.start()
        token[...] = jnp.zeros_like(token)

    n_sem = (N_DEV - 1) * n
    hbm = lambda a: pltpu.HBM(a.shape, a.dtype)
    outs = pl.pallas_call(
        body, name=name,
        out_shape=(pltpu.SemaphoreType.DMA((n_sem,)), pltpu.SemaphoreType.DMA((n_sem,)), *[hbm(a) for a in arrs],
                   *[hbm(a) for a in lands], jax.ShapeDtypeStruct((8, 128), F32)),
        in_specs=(_HBM,) * (2 * n), out_specs=(_SEM, _SEM) + (_HBM,) * (2 * n) + (pl.BlockSpec(memory_space=pltpu.VMEM),),
        input_output_aliases={i: 2 + i for i in range(2 * n)},
        compiler_params=pltpu.CompilerParams(has_side_effects=pltpu.SideEffectType.DATAFLOW_SIDE_EFFECTING),
    )(*[pltpu.with_memory_space_constraint(a, pltpu.HBM) for a in list(arrs) + lands])
    return outs[0], outs[1], list(outs[2:2 + n]), list(outs[2 + n:2 + 2 * n]), outs[-1]


def _exchange_wait(send_sems, recv_sems, sent, zones, after, scatter, name):
    n = len(sent)

    def body(*refs):
        ins, land_refs, s_sems, r_sems = refs[:n], refs[n:2 * n], refs[2 * n], refs[2 * n + 1]
        for cp in _remote_copies(ins, land_refs, s_sems, r_sems, scatter):
            cp.wait_send()
            cp.wait_recv()

    bufs = list(sent) + list(zones)
    outs = pl.pallas_call(
        body, name=name, out_shape=tuple(pltpu.HBM(a.shape, a.dtype) for a in bufs),
        in_specs=(_HBM,) * (2 * n) + (_SEM, _SEM, pl.BlockSpec(memory_space=pl.ANY)), out_specs=(_HBM,) * (2 * n),
        input_output_aliases={i: i for i in range(2 * n)},
        compiler_params=pltpu.CompilerParams(has_side_effects=pltpu.SideEffectType.DATAFLOW_SIDE_EFFECTING),
    )(*bufs, send_sems, recv_sems, after)
    return list(outs[n:])


def _gather_sends_wait(send_sems, sent, after, name):
    def body(sent_ref, s_sems, after_ref, out_ref, token):
        for k in range(1, N_DEV):
            peer, _ = _peer(k)
            pltpu.make_async_remote_copy(src_ref=sent_ref, dst_ref=sent_ref, send_sem=s_sems.at[k - 1],
                                         recv_sem=s_sems.at[k - 1], device_id=peer, device_id_type=MESH).wait_send()
        token[...] = jnp.zeros_like(token)

    return pl.pallas_call(
        body, name=name, out_shape=(pltpu.HBM(sent.shape, sent.dtype), jax.ShapeDtypeStruct((8, 128), F32)),
        in_specs=(_HBM, _SEM, pl.BlockSpec(memory_space=pl.ANY)),
        out_specs=(_HBM, pl.BlockSpec(memory_space=pltpu.VMEM)), input_output_aliases={0: 0},
        compiler_params=pltpu.CompilerParams(has_side_effects=pltpu.SideEffectType.DATAFLOW_SIDE_EFFECTING),
    )(sent, send_sems, after)[1]


def _rope_tables(positions, S):
    tm = min(S, 1024)

    half = DK // 2
    inv_freq = (ROPE_BASE ** (-jnp.arange(half, dtype=F32) / half)).reshape(1, half)

    def body(pos_ref, f_ref, cos_ref, sin_ref):
        ang = pos_ref[...].astype(F32) * f_ref[...]
        cos_ref[...] = jnp.cos(ang)
        sin_ref[...] = jnp.sin(ang)

    return pl.pallas_call(
        body, name="rope_tables", grid=(S // tm,),
        in_specs=[pl.BlockSpec((tm, 1), lambda i: (i, 0)), pl.BlockSpec((1, half), lambda i: (0, 0))],
        out_specs=[pl.BlockSpec((tm, half), lambda i: (i, 0))] * 2,
        out_shape=[jax.ShapeDtypeStruct((S, half), F32)] * 2,
    )(positions.reshape(S, 1), inv_freq)


def _mod_part(c_all, w_ada, b_part):
    def body(c_ref, w_ref, b_ref, o_ref):
        o_ref[...] = _dot(c_ref[...].astype(BF16), w_ref[...].astype(BF16)) + b_ref[...]

    return pl.pallas_call(body, name="mod_part",
                          out_shape=jax.ShapeDtypeStruct((N_DEV, w_ada.shape[1]), F32))(c_all, w_ada, b_part)


def _sum_devices(g):
    def body(g_ref, o_ref):
        acc = g_ref[0]
        for d in range(1, N_DEV):
            acc = acc + g_ref[d]
        o_ref[...] = acc

    return pl.pallas_call(body, name="sum_devices", out_shape=jax.ShapeDtypeStruct(g.shape[1:], F32))(g)


def _grad_w_ada(c_t, dmod):
    def body(c_ref, d_ref, o_ref):
        acc = c_ref[:, 0:1] * d_ref[0:1, :]
        for b in range(1, N_DEV):
            acc = acc + c_ref[:, b:b + 1] * d_ref[b:b + 1, :]
        o_ref[...] = acc

    return pl.pallas_call(body, name="grad_w_ada",
                          out_shape=jax.ShapeDtypeStruct((c_t.shape[0], dmod.shape[1]), F32))(c_t, dmod)


def _adam_math(w, g, m, v):
    m = ADAM_B1 * m + (1.0 - ADAM_B1) * g
    v = ADAM_B2 * v + (1.0 - ADAM_B2) * (g * g)
    m_hat = m / (1.0 - ADAM_B1 ** ADAM_STEP)
    v_hat = v / (1.0 - ADAM_B2 ** ADAM_STEP)
    delta = -ADAM_LR * (m_hat / (jnp.sqrt(v_hat) + ADAM_EPS) + ADAM_WD * w)
    return delta, m, v


def _adam_small(g, w, m, v, name):
    def body(g_ref, w_ref, m_ref, v_ref, d_ref, nm_ref, nv_ref):
        d_ref[...], nm_ref[...], nv_ref[...] = _adam_math(w_ref[...], g_ref[...], m_ref[...], v_ref[...])

    return pl.pallas_call(body, name=name, out_shape=[jax.ShapeDtypeStruct(w.shape, F32)] * 3)(g, w, m, v)


def _adam_shard(parts, w, m, v, name, tr):
    R, L = w.shape

    def body(p_ref, w_ref, m_ref, v_ref, g_ref, d_ref, nm_ref, nv_ref):
        g = p_ref[0].astype(F32)
        for d in range(1, N_DEV):
            g = g + p_ref[d].astype(F32)
        g_ref[...] = g
        d_ref[...], nm_ref[...], nv_ref[...] = _adam_math(w_ref[...], g, m_ref[...], v_ref[...])

    blk = pl.BlockSpec((tr, L), lambda i: (i, 0))
    return pl.pallas_call(
        body, name=name, grid=(R // tr,),
        in_specs=[pl.BlockSpec((N_DEV, tr, L), lambda i: (0, i, 0)), blk, blk, blk],
        out_specs=[blk] * 4, out_shape=[jax.ShapeDtypeStruct((R, L), F32)] * 4,
        compiler_params=_params(),
    )(parts, w, m, v)


ARRIVAL = (0, 1, 2, 4, 3, 5, 6, 7)
LEAD = 2


def _inproj_fwd(x, g_pre, scale, shift, zone, recv_sems, sent, tm):
    S = x.shape[0]
    nt = S // tm
    me = 4 * lax.axis_index("x") + 2 * lax.axis_index("y") + lax.axis_index("c")
    order = jnp.bitwise_xor(me, jnp.asarray(ARRIVAL, jnp.int32))

    def body(order_ref, x_ref, g_ref, sc_ref, sh_ref, w_ref, sent_ref, sem_ref, proj_ref, h_ref, h_all):
        j, i = pl.program_id(0), pl.program_id(1)
        for row in range(1, N_DEV):
            @pl.when((j == row) & (i == 0))
            def _():
                k = ARRIVAL[row]
                pltpu.make_async_remote_copy(
                    src_ref=sent_ref, dst_ref=sent_ref, send_sem=sem_ref.at[k - 1],
                    recv_sem=sem_ref.at[k - 1], device_id=_peer(k)[0], device_id_type=MESH).wait_recv()

        rows = pl.ds(pl.multiple_of(jnp.maximum(i - LEAD, 0) * tm, tm), tm)

        @pl.when((j == 0) & (i >= LEAD))
        def _():
            xf = x_ref[...]
            r = lax.rsqrt(jnp.mean(xf * xf, axis=-1, keepdims=True) + RMS_EPS)
            h = ((xf * r * g_ref[...]) * (1.0 + sc_ref[...]) + sh_ref[...]).astype(BF16)
            h_all[rows, :] = h
            h_ref[...] = h

        @pl.when(i >= LEAD)
        def _():
            proj_ref[...] = _dot(h_all[rows, :], w_ref[...])

    tile = lambda j, i: jnp.maximum(i - LEAD, 0)
    first_pass = lambda j, i: jnp.where(j == 0, tile(j, i), nt - 1)
    vec = pl.BlockSpec((1, D), lambda j, i, o: (0, 0))
    any_space = pl.BlockSpec(memory_space=pl.ANY)
    return pl.pallas_call(
        body, name="inproj_fwd",
        grid_spec=pltpu.PrefetchScalarGridSpec(
            num_scalar_prefetch=1, grid=(N_DEV, nt + LEAD),
            in_specs=[pl.BlockSpec((tm, D), lambda j, i, o: (first_pass(j, i), 0)), vec, vec, vec,
                      pl.BlockSpec((None, D, W_SHARD),
                                   lambda j, i, o: (o[jnp.where(i >= LEAD, j, jnp.maximum(j - 1, 0))], 0, 0)),
                      any_space, _SEM],
            out_specs=[pl.BlockSpec((tm, W_SHARD), lambda j, i, o: (tile(j, i), o[j])),
                       pl.BlockSpec((tm, D), lambda j, i, o: (first_pass(j, i), 0))],
            scratch_shapes=[pltpu.VMEM((S, D), BF16)]),
        out_shape=[jax.ShapeDtypeStruct((S, IN_W), F32), jax.ShapeDtypeStruct((S, D), BF16)],
        compiler_params=_params(has_side_effects=pltpu.SideEffectType.DATAFLOW_SIDE_EFFECTING),
    )(order, x, g_pre, scale, shift, zone, sent, recv_sems)


def _decay_tables():
    log_g = jnp.log1p(-jnp.exp2(-5.0 - jnp.arange(HEADS, dtype=F32)))
    idx = jnp.arange(CHUNK, dtype=F32)
    diff = idx[:, None] - idx[None, :]
    causal = diff >= 0
    mask = jnp.where(causal, jnp.exp(log_g[:, None, None] * jnp.where(causal, diff, 0.0)), 0.0)
    xi = jnp.exp(log_g[:, None] * (idx + 1.0))[:, :, None]
    zeta = jnp.exp(log_g[:, None] * (CHUNK - 1.0 - idx))[:, :, None]
    g_chunk = jnp.broadcast_to(jnp.exp(log_g * CHUNK)[:, None, None], (HEADS, 1, 128))
    return mask, xi, zeta, g_chunk


def _rot(t, cos, sin):
    t1, t2 = t[:, :DK // 2], t[:, DK // 2:]
    return jnp.concatenate([t1 * cos - t2 * sin, t1 * sin + t2 * cos], axis=-1)


def _rot_back(t, cos, sin):
    t1, t2 = t[:, :DK // 2], t[:, DK // 2:]
    return jnp.concatenate([t1 * cos + t2 * sin, t2 * cos - t1 * sin], axis=-1)


def _silu(z):
    return z * _sigmoid(z)


def _retention_fwd(proj, cos, sin, tables, cpb):
    S = proj.shape[0]
    tb = cpb * CHUNK
    mask, xi, zeta, g_chunk = tables

    def body(q_ref, k_ref, v_ref, z_ref, cos_ref, sin_ref, mask_ref, xi_ref, zeta_ref, gc_ref,
             pa_ref, o_ref, st_ref, state):
        @pl.when(pl.program_id(1) == 0)
        def _():
            state[...] = jnp.zeros_like(state)

        m, xi_c, zeta_c, gc = mask_ref[...], xi_ref[...], zeta_ref[...], gc_ref[0:1, 0:1]
        for ci in range(cpb):
            rows = pl.ds(ci * CHUNK, CHUNK)
            cs, sn = cos_ref[rows, :], sin_ref[rows, :]
            qr = _rot(q_ref[rows, :], cs, sn)
            kr = _rot(k_ref[rows, :], cs, sn) * (DK ** -0.5)
            vb = v_ref[rows, :].astype(BF16)
            qb = qr.astype(BF16)
            st = state[...]
            stb = st.astype(BF16)
            st_ref[ci] = stb
            scores = (_dot_nt(qb, kr.astype(BF16)) * m).astype(BF16)
            o = _dot(scores, vb) + _dot(qb, stb) * xi_c
            state[...] = st * gc + _dot_tn((kr * zeta_c).astype(BF16), vb)
            o_ref[rows, :] = o
            mu = jnp.mean(o, axis=-1, keepdims=True)
            oc = o - mu
            var = jnp.mean(oc * oc, axis=-1, keepdims=True)
            pa_ref[rows, :] = (oc * lax.rsqrt(var + GN_EPS) * _silu(z_ref[rows, :])).astype(BF16)

    tab = lambda shape: pl.BlockSpec((None,) + shape, lambda h, n: (h, 0, 0))
    return pl.pallas_call(
        body, name="retention_fwd", grid=(HEADS, S // tb),
        in_specs=[pl.BlockSpec((tb, DK), lambda h, n: (n, h)),
                  pl.BlockSpec((tb, DK), lambda h, n: (n, HEADS + h)),
                  pl.BlockSpec((tb, DV), lambda h, n: (n, COL_V * 2 + h)),
                  pl.BlockSpec((tb, DV), lambda h, n: (n, COL_Z * 2 + h)),
                  pl.BlockSpec((tb, DK // 2), lambda h, n: (n, 0)),
                  pl.BlockSpec((tb, DK // 2), lambda h, n: (n, 0)),
                  tab((CHUNK, CHUNK)), tab((CHUNK, 1)), tab((CHUNK, 1)), tab((1, 128))],
        out_specs=[pl.BlockSpec((tb, DV), lambda h, n: (n, h)),
                   pl.BlockSpec((tb, DV), lambda h, n: (n, h)),
                   pl.BlockSpec((None, cpb, DK, DV), lambda h, n: (h, n, 0, 0))],
        out_shape=[jax.ShapeDtypeStruct((S, HEADS * DV), BF16), jax.ShapeDtypeStruct((S, HEADS * DV), F32),
                   jax.ShapeDtypeStruct((HEADS, S // CHUNK, DK, DV), BF16)],
        scratch_shapes=[pltpu.VMEM((DK, DV), F32)],
        compiler_params=_params(),
    )(proj, proj, proj, proj, cos, sin, mask, xi, zeta, g_chunk)


SUBLANES = 8
FWD_ROWS = 16
BWD_ROWS = 8


def _shifted_copies(src, dst):
    rows = dst.shape[1]
    src[pl.ds(rows, SUBLANES), :] = jnp.zeros((SUBLANES, src.shape[1]), src.dtype)
    for s in range(SUBLANES):
        dst[s] = src[pl.ds(s, rows), :]


def _layernorm_stats(a):
    mu = jnp.mean(a, axis=-1, keepdims=True)
    ac = a - mu
    var = jnp.mean(ac * ac, axis=-1, keepdims=True)
    rstd = lax.rsqrt(var + LN_EPS)
    return ac * rstd, rstd


def _conv_fwd(proj, conv_w, conv_b, ln_g, ln_b, tm):
    S = proj.shape[0]
    hb = tm // HALO

    def body(uv_ref, ug_ref, uvh_ref, ugh_ref, z_ref, w_ref, b_ref, g_ref, bb_ref, pb_ref, a1_ref, ext, ex):
        halo = _round_bf16(uvh_ref[...] * _sigmoid(ugh_ref[...]))
        ext[pl.ds(0, HALO), :] = jnp.where(pl.program_id(0) == 0, 0.0, halo)
        ext[pl.ds(HALO, tm), :] = _round_bf16(uv_ref[...] * _sigmoid(ug_ref[...]))
        _shifted_copies(ext, ex)
        bias = b_ref[...]

        def rows_body(rb, carry):
            r0 = pl.multiple_of(rb * FWD_ROWS, FWD_ROWS)
            groups = range(FWD_ROWS // SUBLANES)
            acc = [jnp.zeros((SUBLANES, D), F32) for _ in groups]
            for j in range(CONV_W):
                a, s = divmod(HALO - (CONV_W - 1) + j, SUBLANES)
                wj = w_ref[j]
                for k in groups:
                    acc[k] = acc[k] + wj * ex[s, pl.ds(r0 + (a + k) * SUBLANES, SUBLANES), :]
            for k in groups:
                a1_ref[pl.ds(r0 + k * SUBLANES, SUBLANES), :] = acc[k] + bias
            return carry

        lax.fori_loop(0, tm // FWD_ROWS, rows_body, 0)
        xhat, _ = _layernorm_stats(a1_ref[...])
        a3 = _silu(xhat * g_ref[...] + bb_ref[...])
        pb_ref[...] = (a3 * _silu(z_ref[...])).astype(BF16)

    vec = pl.BlockSpec((1, D), lambda i: (0, 0))
    halo = lambda col: pl.BlockSpec((HALO, D), lambda i: (jnp.maximum(i * hb - 1, 0), col))
    return pl.pallas_call(
        body, name="conv_fwd", grid=(S // tm,),
        in_specs=[pl.BlockSpec((tm, D), lambda i: (i, COL_UVAL)), pl.BlockSpec((tm, D), lambda i: (i, COL_UGATE)),
                  halo(COL_UVAL), halo(COL_UGATE), pl.BlockSpec((tm, D), lambda i: (i, COL_ZCONV)),
                  pl.BlockSpec((CONV_W + 1, SUBLANES, D), lambda i: (0, 0, 0)), vec, vec, vec],
        out_specs=[pl.BlockSpec((tm, D), lambda i: (i, 0))] * 2,
        out_shape=[jax.ShapeDtypeStruct((S, D), BF16), jax.ShapeDtypeStruct((S, D), F32)],
        scratch_shapes=[pltpu.VMEM((HALO + tm + SUBLANES, D), F32), pltpu.VMEM((SUBLANES, HALO + tm, D), F32)],
        compiler_params=_params(),
    )(proj, proj, proj, proj, proj, conv_w, conv_b, ln_g, ln_b)


def _head(p_a, p_b, proj, x, target, gate, g_post, w_ret, w_conv, w_out, tm):
    S = x.shape[0]

    def body(pa_ref, pb_ref, ga_ref, gb_ref, x_ref, t_ref, gate_ref, gp_ref, wr_ref, wc_ref, wo_ref,
             loss_ref, dout_ref, dg_ref, dpa_ref, dpb_ref, mb_ref, dy_ref, dya_ref, dyb_ref, small_ref):
        @pl.when(pl.program_id(0) == 0)
        def _():
            loss_ref[...] = jnp.zeros_like(loss_ref)
            small_ref[...] = jnp.zeros_like(small_ref)

        gate, gp = gate_ref[...], gp_ref[...]
        ya = _dot(pa_ref[...], wr_ref[...])
        yb = _dot(pb_ref[...], wc_ref[...])
        sa, sb = _sigmoid(ga_ref[...]), _sigmoid(gb_ref[...])
        mb = (sa * ya + sb * yb).astype(BF16)
        mb_ref[...] = mb
        y = _dot(mb, wo_ref[...])
        r = lax.rsqrt(jnp.mean(y * y, axis=-1, keepdims=True) + RMS_EPS)
        yhat = y * r
        yn = yhat * gp
        err = (x_ref[...] + gate * yn) - t_ref[...]
        loss_ref[...] += 0.5 * jnp.sum(jnp.mean(err * err, axis=-1, keepdims=True))
        dout = err * (1.0 / D)
        dout_ref[...] = dout
        small_ref[0:1, :] += jnp.sum(dout * yn, axis=0, keepdims=True)
        dyn = dout * gate
        small_ref[1:2, :] += jnp.sum(dyn * yhat, axis=0, keepdims=True)
        dyhat = dyn * gp
        dy = (r * (dyhat - yhat * jnp.mean(dyhat * yhat, axis=-1, keepdims=True))).astype(BF16)
        dy_ref[...] = dy
        dmerged = _dot_nt(dy, wo_ref[...])
        dya = dmerged * sa
        dyb = dmerged * sb
        dg_ref[:, 0:D] = (dya * ya * (1.0 - sa)).astype(BF16)
        dg_ref[:, D:2 * D] = (dyb * yb * (1.0 - sb)).astype(BF16)
        dya, dyb = dya.astype(BF16), dyb.astype(BF16)
        dya_ref[...] = dya
        dyb_ref[...] = dyb
        dpa_ref[...] = _dot_nt(dya, wr_ref[...])
        dpb_ref[...] = _dot_nt(dyb, wc_ref[...])

    vec = pl.BlockSpec((1, D), lambda i: (0, 0))
    tile = lambda w, col=0: pl.BlockSpec((tm, w), lambda i: (i, col))
    whole = lambda a: pl.BlockSpec(a.shape, lambda i: (0, 0), pipeline_mode=pl.Buffered(1))
    act = lambda w: jax.ShapeDtypeStruct((S, w), BF16)
    return pl.pallas_call(
        body, name="head", grid=(S // tm,),
        in_specs=[tile(2 * D), tile(D), tile(D, COL_GA), tile(D, COL_GB), tile(D), tile(D), vec, vec,
                  whole(w_ret), whole(w_conv), whole(w_out)],
        out_specs=[pl.BlockSpec((8, 128), lambda i: (0, 0)), tile(D), tile(2 * D), tile(2 * D), tile(D),
                   tile(D), tile(D), tile(D), tile(D), pl.BlockSpec((8, D), lambda i: (0, 0))],
        out_shape=[jax.ShapeDtypeStruct((8, 128), F32), jax.ShapeDtypeStruct((S, D), F32),
                   act(2 * D), jax.ShapeDtypeStruct((S, 2 * D), F32), jax.ShapeDtypeStruct((S, D), F32),
                   act(D), act(D), act(D), act(D), jax.ShapeDtypeStruct((8, D), F32)],
        compiler_params=_params(),
    )(p_a, p_b, proj, proj, x, target, gate, g_post, w_ret, w_conv, w_out)


def _matmul_tn(a, b, name, bn, tk):
    S, M = a.shape
    N = b.shape[1]

    def body(a_ref, b_ref, o_ref):
        @pl.when(pl.program_id(1) == 0)
        def _():
            o_ref[...] = jnp.zeros_like(o_ref)

        o_ref[...] += _dot_tn(a_ref[...], b_ref[...])

    return pl.pallas_call(
        body, name=name, grid=(N // bn, S // tk),
        in_specs=[pl.BlockSpec((tk, M), lambda j, k: (k, 0)), pl.BlockSpec((tk, bn), lambda j, k: (k, j))],
        out_specs=pl.BlockSpec((M, bn), lambda j, k: (0, j)),
        out_shape=jax.ShapeDtypeStruct((M, N), F32),
        compiler_params=_params(),
    )(a, b)


SMALL_B_ROWS = 40


def _dsilu(z, s):
    return s * (1.0 + z * (1.0 - s))


def _conv_bwd(dp_b, proj, a1, conv_w, ln_g, ln_b, tm):
    S = proj.shape[0]
    nt = S // tm

    def body(dpb_ref, uv_ref, ug_ref, z_ref, a1_ref, w_ref, g_ref, bb_ref,
             dc_ref, small_ref, dext, dx, carry, a0_buf, da0_buf, dw_acc):
        @pl.when(pl.program_id(0) == 0)
        def _():
            small_ref[...] = jnp.zeros_like(small_ref)
            carry[...] = jnp.zeros_like(carry)

        xhat, rstd = _layernorm_stats(a1_ref[...])
        g = g_ref[...]
        a2 = xhat * g + bb_ref[...]
        s2 = _sigmoid(a2)
        zc = z_ref[...]
        sz = _sigmoid(zc)
        dpb = dpb_ref[...]
        dc_ref[:, 2 * D:3 * D] = (dpb * (a2 * s2) * _dsilu(zc, sz)).astype(BF16)
        da2 = dpb * (zc * sz) * _dsilu(a2, s2)
        small_ref[32:33, :] += jnp.sum(da2 * xhat, axis=0, keepdims=True)
        small_ref[33:34, :] += jnp.sum(da2, axis=0, keepdims=True)
        dxhat = da2 * g
        da1 = rstd * (dxhat - jnp.mean(dxhat, axis=-1, keepdims=True)
                      - xhat * jnp.mean(dxhat * xhat, axis=-1, keepdims=True))
        small_ref[31:32, :] += jnp.sum(da1, axis=0, keepdims=True)
        da1 = _round_bf16(da1)
        dext[pl.ds(0, tm), :] = da1
        dext[pl.ds(tm, HALO), :] = carry[...]
        carry[...] = da1[0:HALO, :]
        _shifted_copies(dext, dx)
        a0_buf[...] = _round_bf16(uv_ref[...] * _sigmoid(ug_ref[...]))
        dw_acc[...] = jnp.zeros_like(dw_acc)

        def rows_body(rb, c):
            r0 = pl.multiple_of(rb * BWD_ROWS, BWD_ROWS)
            groups = range(BWD_ROWS // SUBLANES)
            a0 = [a0_buf[pl.ds(r0 + k * SUBLANES, SUBLANES), :] for k in groups]
            acc = [jnp.zeros((SUBLANES, D), F32) for _ in groups]
            for j in range(CONV_W):
                a, s = divmod(CONV_W - 1 - j, SUBLANES)
                wj = w_ref[j]
                dw = dw_acc[j]
                for k in groups:
                    win = dx[s, pl.ds(r0 + (a + k) * SUBLANES, SUBLANES), :]
                    acc[k] = acc[k] + wj * win
                    dw = dw + a0[k] * win
                dw_acc[j] = dw
            for k in groups:
                da0_buf[pl.ds(r0 + k * SUBLANES, SUBLANES), :] = acc[k]
            return c

        lax.fori_loop(0, tm // BWD_ROWS, rows_body, 0)
        small_ref[0:CONV_W + 1, :] += jnp.sum(dw_acc[...], axis=1)
        da0 = da0_buf[...]
        sg = _sigmoid(ug_ref[...])
        dc_ref[:, 0:D] = (da0 * sg).astype(BF16)
        dc_ref[:, D:2 * D] = (da0 * uv_ref[...] * sg * (1.0 - sg)).astype(BF16)

    vec = pl.BlockSpec((1, D), lambda i: (0, 0))
    tile = lambda col: pl.BlockSpec((tm, D), lambda i: (nt - 1 - i, col))
    return pl.pallas_call(
        body, name="conv_bwd", grid=(nt,),
        in_specs=[tile(0), tile(COL_UVAL), tile(COL_UGATE), tile(COL_ZCONV), tile(0),
                  pl.BlockSpec((CONV_W + 1, SUBLANES, D), lambda i: (0, 0, 0)), vec, vec],
        out_specs=[pl.BlockSpec((tm, 3 * D), lambda i: (nt - 1 - i, 0)),
                   pl.BlockSpec((SMALL_B_ROWS, D), lambda i: (0, 0))],
        out_shape=[jax.ShapeDtypeStruct((S, 3 * D), BF16), jax.ShapeDtypeStruct((SMALL_B_ROWS, D), F32)],
        scratch_shapes=[pltpu.VMEM((tm + HALO + SUBLANES, D), F32), pltpu.VMEM((SUBLANES, tm + HALO, D), F32),
                        pltpu.VMEM((HALO, D), F32), pltpu.VMEM((tm, D), F32), pltpu.VMEM((tm, D), F32),
                        pltpu.VMEM((CONV_W + 1, SUBLANES, D), F32)],
        compiler_params=_params(),
    )(dp_b, proj, proj, proj, a1, conv_w, ln_g, ln_b)


RET_W = 2 * DK + 2 * DV


def _retention_bwd(dp_a, o, proj, states, cos, sin, tables, cpb):
    S = proj.shape[0]
    tb = cpb * CHUNK
    nb = S // tb
    mask, xi, zeta, g_chunk = tables

    def body(dpa_ref, o_ref, q_ref, k_ref, v_ref, z_ref, st_ref, cos_ref, sin_ref, mask_ref, xi_ref, zeta_ref,
             gc_ref, out_ref, dstate):
        @pl.when(pl.program_id(1) == 0)
        def _():
            dstate[...] = jnp.zeros_like(dstate)

        m, xi_c, zeta_c, gc = mask_ref[...], xi_ref[...], zeta_ref[...], gc_ref[0:1, 0:1]
        for ci in reversed(range(cpb)):
            rows = pl.ds(ci * CHUNK, CHUNK)
            o = o_ref[rows, :]
            oc = o - jnp.mean(o, axis=-1, keepdims=True)
            rstd = lax.rsqrt(jnp.mean(oc * oc, axis=-1, keepdims=True) + GN_EPS)
            rhat = oc * rstd
            z = z_ref[rows, :]
            sz = _sigmoid(z)
            dpa = dpa_ref[rows, :]
            out_ref[rows, 2 * DK + DV:RET_W] = (dpa * rhat * _dsilu(z, sz)).astype(BF16)
            dret = dpa * (z * sz)
            d_o = rstd * (dret - jnp.mean(dret, axis=-1, keepdims=True)
                          - rhat * jnp.mean(dret * rhat, axis=-1, keepdims=True))
            cs, sn = cos_ref[rows, :], sin_ref[rows, :]
            qb = _rot(q_ref[rows, :], cs, sn).astype(BF16)
            kr = _rot(k_ref[rows, :], cs, sn) * (DK ** -0.5)
            kb = kr.astype(BF16)
            kz = (kr * zeta_c).astype(BF16)
            vb = v_ref[rows, :].astype(BF16)
            scores = (_dot_nt(qb, kb) * m).astype(BF16)
            dob = d_o.astype(BF16)
            dox = (d_o * xi_c).astype(BF16)
            dscores = (_dot_nt(dob, vb) * m).astype(BF16)
            dst = dstate[...]
            dstb = dst.astype(BF16)
            dqr = _dot(dscores, kb) + _dot_nt(dox, st_ref[ci])
            dkr = _dot_tn(dscores, qb) + _dot_nt(vb, dstb) * zeta_c
            dv = _dot_tn(scores, dob) + _dot(kz, dstb)
            dstate[...] = dst * gc + _dot_tn(qb, dox)
            out_ref[rows, 0:DK] = _rot_back(dqr, cs, sn).astype(BF16)
            out_ref[rows, DK:2 * DK] = (_rot_back(dkr, cs, sn) * (DK ** -0.5)).astype(BF16)
            out_ref[rows, 2 * DK:2 * DK + DV] = dv.astype(BF16)

    tab = lambda shape: pl.BlockSpec((None,) + shape, lambda h, n: (h, 0, 0))
    blk = lambda w, col0: pl.BlockSpec((tb, w), lambda h, n: (nb - 1 - n, col0 + h))
    return pl.pallas_call(
        body, name="retention_bwd", grid=(HEADS, nb),
        in_specs=[blk(DV, 0), blk(DV, 0), blk(DK, 0), blk(DK, HEADS), blk(DV, COL_V * 2), blk(DV, COL_Z * 2),
                  pl.BlockSpec((None, cpb, DK, DV), lambda h, n: (h, nb - 1 - n, 0, 0)),
                  pl.BlockSpec((tb, DK // 2), lambda h, n: (nb - 1 - n, 0)),
                  pl.BlockSpec((tb, DK // 2), lambda h, n: (nb - 1 - n, 0)),
                  tab((CHUNK, CHUNK)), tab((CHUNK, 1)), tab((CHUNK, 1)), tab((1, 128))],
        out_specs=blk(RET_W, 0),
        out_shape=jax.ShapeDtypeStruct((S, HEADS * RET_W), BF16),
        scratch_shapes=[pltpu.VMEM((DK, DV), F32)],
        compiler_params=_params(),
    )(dp_a, o, proj, proj, proj, proj, states, cos, sin, mask, xi, zeta, g_chunk)


def _to_head_major(w):
    parts = []
    for h in range(HEADS):
        parts += [w[:, h * DK:(h + 1) * DK], w[:, D + h * DK:D + (h + 1) * DK],
                  w[:, 2 * D + h * DV:2 * D + (h + 1) * DV], w[:, 4 * D + h * DV:4 * D + (h + 1) * DV]]
    return jnp.concatenate(parts, axis=1)


def _from_head_major(w):
    starts = (0, DK, 2 * DK, 2 * DK + DV)
    widths = (DK, DK, DV, DV)
    parts = [w[:, h * RET_W + s:h * RET_W + s + n] for s, n in zip(starts, widths) for h in range(HEADS)]
    return jnp.concatenate(parts, axis=1)


def _dh(d_r, d_c, d_g, w_r, w_c, w_g, x, dout, g_pre, scale, tm):
    S = x.shape[0]

    def body(dr_ref, dc_ref, dg_ref, wr_ref, wc_ref, wg_ref, x_ref, dout_ref, g_ref, sc_ref, gx_ref, small_ref):
        @pl.when(pl.program_id(0) == 0)
        def _():
            small_ref[...] = jnp.zeros_like(small_ref)

        dh = _dot_nt(dr_ref[...], wr_ref[...]) + _dot_nt(dc_ref[...], wc_ref[...]) + _dot_nt(dg_ref[...], wg_ref[...])
        xf = x_ref[...]
        r = lax.rsqrt(jnp.mean(xf * xf, axis=-1, keepdims=True) + RMS_EPS)
        xhat = xf * r
        g, sc1 = g_ref[...], 1.0 + sc_ref[...]
        small_ref[0:1, :] += jnp.sum(dh, axis=0, keepdims=True)
        small_ref[1:2, :] += jnp.sum(dh * (xhat * g), axis=0, keepdims=True)
        small_ref[2:3, :] += jnp.sum(dh * sc1 * xhat, axis=0, keepdims=True)
        dxhat = dh * sc1 * g
        gx_ref[...] = dout_ref[...] + r * (dxhat - xhat * jnp.mean(dxhat * xhat, axis=-1, keepdims=True))

    vec = pl.BlockSpec((1, D), lambda i: (0, 0))
    tile = lambda w: pl.BlockSpec((tm, w), lambda i: (i, 0))
    whole = lambda a: pl.BlockSpec(a.shape, lambda i: (0, 0), pipeline_mode=pl.Buffered(1))
    return pl.pallas_call(
        body, name="dh", grid=(S // tm,),
        in_specs=[tile(d_r.shape[1]), tile(d_c.shape[1]), tile(d_g.shape[1]), whole(w_r), whole(w_c), whole(w_g),
                  tile(D), tile(D), vec, vec],
        out_specs=[tile(D), pl.BlockSpec((8, D), lambda i: (0, 0))],
        out_shape=[jax.ShapeDtypeStruct((S, D), F32), jax.ShapeDtypeStruct((8, D), F32)],
        compiler_params=_params(),
    )(d_r, d_c, d_g, w_r, w_c, w_g, x, dout, g_pre, scale)


ROW_GATE, ROW_POST = 0, 1
ROW_CONV_W, ROW_CONV_B, ROW_LN_G, ROW_LN_B = 8, 8 + 31, 8 + 32, 8 + 33
ROW_SHIFT, ROW_SCALE, ROW_PRE = 48, 49, 50


def kernel(x, c, positions, w_ada, b_ada, pre_norm_g, w_in, conv_w, conv_b, conv_ln_g, conv_ln_b, w_ret_out, w_conv_out, w_out, post_norm_g, loss_target, m_w_ada, m_b_ada, m_pre_norm_g, m_w_in, m_conv_w, m_conv_b, m_conv_ln_g, m_conv_ln_b, m_w_ret_out, m_w_conv_out, m_w_out, m_post_norm_g, v_w_ada, v_b_ada, v_pre_norm_g, v_w_in, v_conv_w, v_conv_b, v_conv_ln_g, v_conv_ln_b, v_w_ret_out, v_w_conv_out, v_w_out, v_post_norm_g):
    S = x.shape[1]
    axes = ("x", "y", "c")
    me = 4 * lax.axis_index("x") + 2 * lax.axis_index("y") + lax.axis_index("c")
    x2, target = x[0], loss_target[0]
    ada_w = w_ada.shape[2]
    cw_w = conv_w.shape[2]

    c_g, conv_w_g = _exchange([jnp.pad(c, ((0, 7), (0, 0))), jnp.pad(conv_w[0], ((0, 1), (0, 0)))],
                              False, "gather_c_conv_w", True)
    c_all = c_g[:, 0, :]
    conv_w_full = _round_bf16(conv_w_g.transpose(1, 0, 2).reshape(CONV_W + 1, D))
    conv_w_full = jnp.broadcast_to(conv_w_full[:, None, :], (CONV_W + 1, SUBLANES, D))
    b_part = lax.dynamic_slice(b_ada, (0, me * ada_w), (1, ada_w))
    mod_g = _exchange([_mod_part(c_all, w_ada[0], b_part)], False, "gather_mod", True)[0]

    own_in, mod_g = lax.optimization_barrier((w_in[0].astype(BF16), mod_g))
    in_send, in_recv, (in_sent,), (in_zone,), tok_in = _exchange_start([own_in], False, "gather_w_in_start")
    own_rest = [(w_ret_out[0] + tok_in[0:1, 0:1]).astype(BF16), w_conv_out[0].astype(BF16), w_out[0].astype(BF16)]
    *rest_handles, tok_rest = _exchange_start(own_rest, False, "gather_w_rest_start")
    mod = lax.dynamic_index_in_dim(mod_g, me, axis=1, keepdims=False).reshape(1, 3 * D)
    shift, scale, gate = mod[:, :D], mod[:, D:2 * D], mod[:, 2 * D:]
    positions = positions + tok_rest[0:1, 0:1].astype(jnp.int32)

    cos, sin = _rope_tables(positions, S)
    tables = _decay_tables()
    proj, h = _inproj_fwd(x2, pre_norm_g, scale, shift, in_zone, in_recv, in_sent, tm=min(S, 1024))
    g_in, proj = lax.optimization_barrier((in_zone, proj))
    gate = gate + _gather_sends_wait(in_send, in_sent, h, "gather_w_in_sends_wait")[0:1, 0:1]
    w_in_full = g_in.transpose(1, 0, 2).reshape(D, IN_W)
    p_a, o, states = _retention_fwd(proj, cos, sin, tables, cpb=4)
    p_b, a1 = _conv_fwd(proj, conv_w_full, conv_b, conv_ln_g, conv_ln_b, tm=256)
    g_ret, g_conv, g_out = _exchange_wait(*rest_handles, p_b, False, "gather_w_rest_wait")
    w_ret_full, w_conv_full, w_out_full = g_ret.reshape(2 * D, D), g_conv.reshape(D, D), g_out.reshape(D, D)

    loss_sums, dout, d_g, dp_a, dp_b, merged, dy, dy_a, dy_b, small_a = _head(
        p_a, p_b, proj, x2, target, gate, post_norm_g, w_ret_full, w_conv_full, w_out_full, tm=256)
    parts_rest = [_matmul_tn(p_a, dy_a, "dw_ret_out", bn=D, tk=512).astype(BF16).reshape(N_DEV, 2 * D // N_DEV, D),
                  _matmul_tn(p_b, dy_b, "dw_conv_out", bn=D, tk=512).astype(BF16).reshape(N_DEV, D // N_DEV, D),
                  _matmul_tn(merged, dy, "dw_out", bn=D, tk=512).astype(BF16).reshape(N_DEV, D // N_DEV, D)]
    *s_rest_handles, tok_s_rest = _exchange_start(parts_rest, True, "scatter_rest_start")
    d_c, small_b = _conv_bwd(dp_b, proj, a1, conv_w_full, conv_ln_g + tok_s_rest[0:1, 0:1], conv_ln_b, tm=256)
    d_r = _retention_bwd(dp_a, o, proj, states, cos, sin, tables, cpb=4)
    dw_r = _matmul_tn(h, d_r, "dw_in_ret", bn=RET_W, tk=512)
    dw_c = _matmul_tn(h, d_c, "dw_in_conv", bn=D, tk=512)
    dw_g = _matmul_tn(h, d_g, "dw_in_gate", bn=D, tk=512)
    dw_in = jnp.concatenate([_from_head_major(dw_r), dw_c, dw_g], axis=1)
    parts_in = dw_in.astype(BF16).reshape(D, N_DEV, W_SHARD).transpose(1, 0, 2)
    *s_in_handles, tok_s_in = _exchange_start([parts_in], True, "scatter_w_in_start")
    grad_x, small_c = _dh(d_r, d_c, d_g, _to_head_major(w_in_full), w_in_full[:, 6 * D:9 * D], w_in_full[:, 9 * D:],
                          x2, dout, pre_norm_g + tok_s_in[0:1, 0:1], scale, tm=256)

    r_ret, r_conv, r_out = _exchange_wait(*s_rest_handles, small_c, True, "scatter_rest_wait")
    (r_in,) = _exchange_wait(*s_in_handles, small_c, True, "scatter_w_in_wait")
    big = {"w_in": _adam_shard(r_in, w_in[0], m_w_in[0], v_w_in[0], "adam_w_in", 128),
           "w_ret_out": _adam_shard(r_ret, w_ret_out[0], m_w_ret_out[0], v_w_ret_out[0], "adam_w_ret_out", 128),
           "w_conv_out": _adam_shard(r_conv, w_conv_out[0], m_w_conv_out[0], v_w_conv_out[0], "adam_w_conv_out", 128),
           "w_out": _adam_shard(r_out, w_out[0], m_w_out[0], v_w_out[0], "adam_w_out", 128)}

    sm_g = _exchange([jnp.concatenate([small_a, small_b, small_c], axis=0)], False, "gather_small", True)[0]
    sm = _sum_devices(sm_g)
    row = lambda r: sm[r:r + 1]
    g8 = jnp.concatenate([row(ROW_SHIFT), row(ROW_SCALE), row(ROW_GATE), row(ROW_PRE), row(ROW_CONV_B),
                          row(ROW_LN_G), row(ROW_LN_B), row(ROW_POST)], axis=0)
    stack8 = lambda ba, pre, cb, lg, lb, post: jnp.concatenate([ba.reshape(3, D), pre, cb, lg, lb, post], axis=0)
    d8, m8, v8 = _adam_small(
        g8, stack8(b_ada, pre_norm_g, conv_b, conv_ln_g, conv_ln_b, post_norm_g),
        stack8(m_b_ada, m_pre_norm_g, m_conv_b, m_conv_ln_g, m_conv_ln_b, m_post_norm_g),
        stack8(v_b_ada, v_pre_norm_g, v_conv_b, v_conv_ln_g, v_conv_ln_b, v_post_norm_g), "adam_small")
    unstack8 = lambda a: {"b_ada": a[0:3].reshape(3 * D), "pre_norm_g": a[3], "conv_b": a[4],
                          "conv_ln_g": a[5], "conv_ln_b": a[6], "post_norm_g": a[7]}
    g_small, d_small, m_small, v_small = unstack8(g8), unstack8(d8), unstack8(m8), unstack8(v8)

    pad_row = lambda a: jnp.pad(a[0], ((0, 1), (0, 0)))
    g_cw = lax.dynamic_slice(sm[ROW_CONV_W:ROW_CONV_W + CONV_W + 1], (0, me * cw_w), (CONV_W + 1, cw_w))
    d_cw, m_cw, v_cw = _adam_small(g_cw, pad_row(conv_w), pad_row(m_conv_w), pad_row(v_conv_w), "adam_conv_w")

    dmod_all = jnp.concatenate([sm_g[:, ROW_SHIFT], sm_g[:, ROW_SCALE], sm_g[:, ROW_GATE]], axis=1)
    g_wa = _grad_w_ada(c_all.T, lax.dynamic_slice(dmod_all, (0, me * ada_w), (N_DEV, ada_w)))
    d_wa, m_wa, v_wa = _adam_small(g_wa, w_ada[0], m_w_ada[0], v_w_ada[0], "adam_w_ada")

    grads = {"w_ada": g_wa, "conv_w": g_cw[:CONV_W], **g_small, **{n: t[0] for n, t in big.items()}}
    deltas = {"w_ada": d_wa, "conv_w": d_cw[:CONV_W], **d_small, **{n: t[1] for n, t in big.items()}}
    new_m = {"w_ada": m_wa, "conv_w": m_cw[:CONV_W], **m_small, **{n: t[2] for n, t in big.items()}}
    new_v = {"w_ada": v_wa, "conv_w": v_cw[:CONV_W], **v_small, **{n: t[3] for n, t in big.items()}}
    order = ["w_ada", "b_ada", "pre_norm_g", "w_in", "conv_w", "conv_b", "conv_ln_g", "conv_ln_b", "w_ret_out",
             "w_conv_out", "w_out", "post_norm_g"]
    loss = lax.psum(loss_sums[0, 0], axes)
    out = [loss, grad_x[None]]
    for group in (grads, deltas, new_m, new_v):
        out += [group[n][None] for n in order]
    return tuple(out)
```

```python
import jax
import jax.numpy as jnp
import numpy as np
from jax import lax
from jax.experimental import pallas as pl
from jax.experimental.pallas import tpu as pltpu

F32 = jnp.float32
BF16 = jnp.bfloat16
MESH = pl.DeviceIdType.MESH

N_DEV = 8
D = 1024
HEADS = 4
DK = 256
DV = 512
CHUNK = 128
CONV_W = 31
HALO = 32
IN_W = 11264
W_SHARD = IN_W // N_DEV
ROPE_BASE = 10000.0
GN_EPS = 1e-5
LN_EPS = 1e-5
RMS_EPS = 1e-6
ADAM_LR, ADAM_B1, ADAM_B2, ADAM_EPS, ADAM_WD, ADAM_STEP = 0.001, 0.9, 0.999, 1e-08, 0.01, 10

COL_V, COL_Z, COL_UVAL, COL_UGATE, COL_ZCONV, COL_GA, COL_GB = 2, 4, 6, 7, 8, 9, 10

VMEM_LIMIT = 56 * 1024 * 1024


def _params(**kw):
    return pltpu.CompilerParams(vmem_limit_bytes=VMEM_LIMIT, **kw)


def _sigmoid(z):
    return jax.nn.sigmoid(z)


def _round_bf16(a):
    return a.astype(BF16).astype(F32)


def _dot(a, b):
    return jnp.dot(a, b, preferred_element_type=F32)


def _dot_nt(a, b):
    return lax.dot_general(a, b, (((1,), (1,)), ((), ())), preferred_element_type=F32)


def _dot_tn(a, b):
    return lax.dot_general(a, b, (((0,), (0,)), ((), ())), preferred_element_type=F32)


def _my_place():
    return lax.axis_index("x"), lax.axis_index("y"), lax.axis_index("c")


def _peer(k):
    x, y, c = _my_place()
    px = lax.rem(x + ((k >> 2) & 1), 2)
    py = lax.rem(y + ((k >> 1) & 1), 2)
    pc = lax.rem(c + (k & 1), 2)
    return (px, py, pc), 4 * px + 2 * py + pc


def _exchange(arrs, scatter, name, in_vmem):
    n = len(arrs)

    def body(*refs):
        ins, outs = refs[:n], refs[n:2 * n]
        send_sems, recv_sems, local_sems = refs[2 * n:]
        x, y, c = _my_place()
        me = 4 * x + 2 * y + c
        copies = []
        for i in range(n):
            src = ins[i].at[me] if scatter else ins[i]
            cp = pltpu.make_async_copy(src, outs[i].at[me], local_sems.at[i])
            cp.start()
            copies.append(cp)
        for k in range(1, N_DEV):
            peer, peer_idx = _peer(k)
            for i in range(n):
                src = ins[i].at[peer_idx] if scatter else ins[i]
                cp = pltpu.make_async_remote_copy(
                    src_ref=src, dst_ref=outs[i].at[me], send_sem=send_sems.at[i, k - 1],
                    recv_sem=recv_sems.at[i, k - 1], device_id=peer, device_id_type=MESH)
                cp.start()
                copies.append(cp)
        for cp in copies:
            cp.wait()

    space = pltpu.VMEM if in_vmem else pl.ANY
    out_shape = [jax.ShapeDtypeStruct(a.shape if scatter else (N_DEV,) + a.shape, a.dtype) for a in arrs]
    return pl.pallas_call(
        body, name=name, out_shape=out_shape,
        in_specs=[pl.BlockSpec(memory_space=space)] * n,
        out_specs=[pl.BlockSpec(memory_space=space)] * n,
        scratch_shapes=[pltpu.SemaphoreType.DMA((n, N_DEV - 1)), pltpu.SemaphoreType.DMA((n, N_DEV - 1)),
                        pltpu.SemaphoreType.DMA((n,))],
        compiler_params=_params(has_side_effects=True),
    )(*arrs)


def _remote_copies(ins, lands, send_sems, recv_sems, scatter):
    x, y, c = _my_place()
    me = 4 * x + 2 * y + c
    copies = []
    for k in range(1, N_DEV):
        peer, peer_idx = _peer(k)
        for i in range(len(ins)):
            copies.append(pltpu.make_async_remote_copy(
                src_ref=ins[i].at[peer_idx] if scatter else ins[i], dst_ref=lands[i].at[me],
                send_sem=send_sems.at[(N_DEV - 1) * i + k - 1], recv_sem=recv_sems.at[(N_DEV - 1) * i + k - 1],
                device_id=peer, device_id_type=MESH))
    return copies


_HBM = pl.BlockSpec(memory_space=pltpu.HBM)
_SEM = pl.BlockSpec(memory_space=pltpu.SEMAPHORE)


def _exchange_start(arrs, scatter, name):
    n = len(arrs)
    me = 4 * lax.axis_index("x") + 2 * lax.axis_index("y") + lax.axis_index("c")
    lands = []
    for a in arrs:
        own = lax.dynamic_index_in_dim(a, me, axis=0) if scatter else a[None]
        zone = lax.empty(a.shape if scatter else (N_DEV,) + a.shape, a.dtype)
        lands.append(lax.dynamic_update_slice(zone, own, (me,) + (0,) * (own.ndim - 1)))

    def body(*refs):
        ins, land_refs, send_sems, recv_sems, token = refs[:n], refs[n:2 * n], refs[2 * n], refs[2 * n + 1], refs[-1]
        for cp in _remote_copies(ins, land_refs, send_sems, recv_sems, scatter):
            cp.start()
        token[...] = jnp.zeros_like(token)

    n_sem = (N_DEV - 1) * n
    hbm = lambda a: pltpu.HBM(a.shape, a.dtype)
    outs = pl.pallas_call(
        body, name=name,
        out_shape=(pltpu.SemaphoreType.DMA((n_sem,)), pltpu.SemaphoreType.DMA((n_sem,)), *[hbm(a) for a in arrs],
                   *[hbm(a) for a in lands], jax.ShapeDtypeStruct((8, 128), F32)),
        in_specs=(_HBM,) * (2 * n), out_specs=(_SEM, _SEM) + (_HBM,) * (2 * n) + (pl.BlockSpec(memory_space=pltpu.VMEM),),
        input_output_aliases={i: 2 + i for i in range(2 * n)},
        compiler_params=pltpu.CompilerParams(has_side_effects=pltpu.SideEffectType.DATAFLOW_SIDE_EFFECTING),
    )(*[pltpu.with_memory_space_constraint(a, pltpu.HBM) for a in list(arrs) + lands])
    return outs[0], outs[1], list(outs[2:2 + n]), list(outs[2 + n:2 + 2 * n]), outs[-1]


def _exchange_wait(send_sems, recv_sems, sent, zones, after, scatter, name):
    n = len(sent)

    def body(*refs):
        ins, land_refs, s_sems, r_sems = refs[:n], refs[n:2 * n], refs[2 * n], refs[2 * n + 1]
        for cp in _remote_copies(ins, land_refs, s_sems, r_sems, scatter):
            cp.wait_send()
            cp.wait_recv()

    bufs = list(sent) + list(zones)
    outs = pl.pallas_call(
        body, name=name, out_shape=tuple(pltpu.HBM(a.shape, a.dtype) for a in bufs),
        in_specs=(_HBM,) * (2 * n) + (_SEM, _SEM, pl.BlockSpec(memory_space=pl.ANY)), out_specs=(_HBM,) * (2 * n),
        input_output_aliases={i: i for i in range(2 * n)},
        compiler_params=pltpu.CompilerParams(has_side_effects=pltpu.SideEffectType.DATAFLOW_SIDE_EFFECTING),
    )(*bufs, send_sems, recv_sems, after)
    return list(outs[n:])


def _gather_sends_wait(send_sems, sent, after, name):
    def body(sent_ref, s_sems, after_ref, out_ref, token):
        for k in range(1, N_DEV):
            peer, _ = _peer(k)
            pltpu.make_async_remote_copy(src_ref=sent_ref, dst_ref=sent_ref, send_sem=s_sems.at[k - 1],
                                         recv_sem=s_sems.at[k - 1], device_id=peer, device_id_type=MESH).wait_send()
        token[...] = jnp.zeros_like(token)

    return pl.pallas_call(
        body, name=name, out_shape=(pltpu.HBM(sent.shape, sent.dtype), jax.ShapeDtypeStruct((8, 128), F32)),
        in_specs=(_HBM, _SEM, pl.BlockSpec(memory_space=pl.ANY)),
        out_specs=(_HBM, pl.BlockSpec(memory_space=pltpu.VMEM)), input_output_aliases={0: 0},
        compiler_params=pltpu.CompilerParams(has_side_effects=pltpu.SideEffectType.DATAFLOW_SIDE_EFFECTING),
    )(sent, send_sems, after)[1]


def _rope_tables(positions, S):
    tm = min(S, 1024)

    half = DK // 2
    inv_freq = (ROPE_BASE ** (-jnp.arange(half, dtype=F32) / half)).reshape(1, half)

    def body(pos_ref, f_ref, cos_ref, sin_ref):
        ang = pos_ref[...].astype(F32) * f_ref[...]
        cos_ref[...] = jnp.cos(ang)
        sin_ref[...] = jnp.sin(ang)

    return pl.pallas_call(
        body, name="rope_tables", grid=(S // tm,),
        in_specs=[pl.BlockSpec((tm, 1), lambda i: (i, 0)), pl.BlockSpec((1, half), lambda i: (0, 0))],
        out_specs=[pl.BlockSpec((tm, half), lambda i: (i, 0))] * 2,
        out_shape=[jax.ShapeDtypeStruct((S, half), F32)] * 2,
    )(positions.reshape(S, 1), inv_freq)


def _mod_part(c_all, w_ada, b_part):
    def body(c_ref, w_ref, b_ref, o_ref):
        o_ref[...] = _dot(c_ref[...].astype(BF16), w_ref[...].astype(BF16)) + b_ref[...]

    return pl.pallas_call(body, name="mod_part",
                          out_shape=jax.ShapeDtypeStruct((N_DEV, w_ada.shape[1]), F32))(c_all, w_ada, b_part)


def _sum_devices(g):
    def body(g_ref, o_ref):
        acc = g_ref[0]
        for d in range(1, N_DEV):
            acc = acc + g_ref[d]
        o_ref[...] = acc

    return pl.pallas_call(body, name="sum_devices", out_shape=jax.ShapeDtypeStruct(g.shape[1:], F32))(g)


def _grad_w_ada(c_t, dmod):
    def body(c_ref, d_ref, o_ref):
        acc = c_ref[:, 0:1] * d_ref[0:1, :]
        for b in range(1, N_DEV):
            acc = acc + c_ref[:, b:b + 1] * d_ref[b:b + 1, :]
        o_ref[...] = acc

    return pl.pallas_call(body, name="grad_w_ada",
                          out_shape=jax.ShapeDtypeStruct((c_t.shape[0], dmod.shape[1]), F32))(c_t, dmod)


def _adam_math(w, g, m, v):
    m = ADAM_B1 * m + (1.0 - ADAM_B1) * g
    v = ADAM_B2 * v + (1.0 - ADAM_B2) * (g * g)
    m_hat = m / (1.0 - ADAM_B1 ** ADAM_STEP)
    v_hat = v / (1.0 - ADAM_B2 ** ADAM_STEP)
    delta = -ADAM_LR * (m_hat / (jnp.sqrt(v_hat) + ADAM_EPS) + ADAM_WD * w)
    return delta, m, v


def _adam_small(g, w, m, v, name):
    def body(g_ref, w_ref, m_ref, v_ref, d_ref, nm_ref, nv_ref):
        d_ref[...], nm_ref[...], nv_ref[...] = _adam_math(w_ref[...], g_ref[...], m_ref[...], v_ref[...])

    return pl.pallas_call(body, name=name, out_shape=[jax.ShapeDtypeStruct(w.shape, F32)] * 3)(g, w, m, v)


def _adam_shard(parts, w, m, v, name, tr):
    R, L = w.shape

    def body(p_ref, w_ref, m_ref, v_ref, g_ref, d_ref, nm_ref, nv_ref):
        g = p_ref[0].astype(F32)
        for d in range(1, N_DEV):
            g = g + p_ref[d].astype(F32)
        g_ref[...] = g
        d_ref[...], nm_ref[...], nv_ref[...] = _adam_math(w_ref[...], g, m_ref[...], v_ref[...])

    blk = pl.BlockSpec((tr, L), lambda i: (i, 0))
    return pl.pallas_call(
        body, name=name, grid=(R // tr,),
        in_specs=[pl.BlockSpec((N_DEV, tr, L), lambda i: (0, i, 0)), blk, blk, blk],
        out_specs=[blk] * 4, out_shape=[jax.ShapeDtypeStruct((R, L), F32)] * 4,
        compiler_params=_params(),
    )(parts, w, m, v)


ARRIVAL = (0, 1, 2, 4, 3, 5, 6, 7)
LEAD = 2


def _inproj_fwd(x, g_pre, scale, shift, zone, recv_sems, sent, tm):
    S = x.shape[0]
    nt = S // tm
    me = 4 * lax.axis_index("x") + 2 * lax.axis_index("y") + lax.axis_index("c")
    order = jnp.bitwise_xor(me, jnp.asarray(ARRIVAL, jnp.int32))

    def body(order_ref, x_ref, g_ref, sc_ref, sh_ref, w_ref, sent_ref, sem_ref, proj_ref, h_ref, h_all):
        j, i = pl.program_id(0), pl.program_id(1)
        for row in range(1, N_DEV):
            @pl.when((j == row) & (i == 0))
            def _():
                k = ARRIVAL[row]
                pltpu.make_async_remote_copy(
                    src_ref=sent_ref, dst_ref=sent_ref, send_sem=sem_ref.at[k - 1],
                    recv_sem=sem_ref.at[k - 1], device_id=_peer(k)[0], device_id_type=MESH).wait_recv()

        rows = pl.ds(pl.multiple_of(jnp.maximum(i - LEAD, 0) * tm, tm), tm)

        @pl.when((j == 0) & (i >= LEAD))
        def _():
            xf = x_ref[...]
            r = lax.rsqrt(jnp.mean(xf * xf, axis=-1, keepdims=True) + RMS_EPS)
            h = ((xf * r * g_ref[...]) * (1.0 + sc_ref[...]) + sh_ref[...]).astype(BF16)
            h_all[rows, :] = h
            h_ref[...] = h

        @pl.when(i >= LEAD)
        def _():
            proj_ref[...] = _dot(h_all[rows, :], w_ref[...])

    tile = lambda j, i: jnp.maximum(i - LEAD, 0)
    first_pass = lambda j, i: jnp.where(j == 0, tile(j, i), nt - 1)
    vec = pl.BlockSpec((1, D), lambda j, i, o: (0, 0))
    any_space = pl.BlockSpec(memory_space=pl.ANY)
    return pl.pallas_call(
        body, name="inproj_fwd",
        grid_spec=pltpu.PrefetchScalarGridSpec(
            num_scalar_prefetch=1, grid=(N_DEV, nt + LEAD),
            in_specs=[pl.BlockSpec((tm, D), lambda j, i, o: (first_pass(j, i), 0)), vec, vec, vec,
                      pl.BlockSpec((None, D, W_SHARD),
                                   lambda j, i, o: (o[jnp.where(i >= LEAD, j, jnp.maximum(j - 1, 0))], 0, 0)),
                      any_space, _SEM],
            out_specs=[pl.BlockSpec((tm, W_SHARD), lambda j, i, o: (tile(j, i), o[j])),
                       pl.BlockSpec((tm, D), lambda j, i, o: (first_pass(j, i), 0))],
            scratch_shapes=[pltpu.VMEM((S, D), BF16)]),
        out_shape=[jax.ShapeDtypeStruct((S, IN_W), F32), jax.ShapeDtypeStruct((S, D), BF16)],
        compiler_params=_params(has_side_effects=pltpu.SideEffectType.DATAFLOW_SIDE_EFFECTING),
    )(order, x, g_pre, scale, shift, zone, sent, recv_sems)


def _decay_tables():
    log_g = jnp.log1p(-jnp.exp2(-5.0 - jnp.arange(HEADS, dtype=F32)))
    idx = jnp.arange(CHUNK, dtype=F32)
    diff = idx[:, None] - idx[None, :]
    causal = diff >= 0
    mask = jnp.where(causal, jnp.exp(log_g[:, None, None] * jnp.where(causal, diff, 0.0)), 0.0)
    xi = jnp.exp(log_g[:, None] * (idx + 1.0))[:, :, None]
    zeta = jnp.exp(log_g[:, None] * (CHUNK - 1.0 - idx))[:, :, None]
    g_chunk = jnp.broadcast_to(jnp.exp(log_g * CHUNK)[:, None, None], (HEADS, 1, 128))
    return mask, xi, zeta, g_chunk


def _rot(t, cos, sin):
    t1, t2 = t[:, :DK // 2], t[:, DK // 2:]
    return jnp.concatenate([t1 * cos - t2 * sin, t1 * sin + t2 * cos], axis=-1)


def _rot_back(t, cos, sin):
    t1, t2 = t[:, :DK // 2], t[:, DK // 2:]
    return jnp.concatenate([t1 * cos + t2 * sin, t2 * cos - t1 * sin], axis=-1)


def _silu(z):
    return z * _sigmoid(z)


def _retention_fwd(proj, cos, sin, tables, cpb):
    S = proj.shape[0]
    tb = cpb * CHUNK
    mask, xi, zeta, g_chunk = tables

    def body(q_ref, k_ref, v_ref, z_ref, cos_ref, sin_ref, mask_ref, xi_ref, zeta_ref, gc_ref,
             pa_ref, o_ref, st_ref, state):
        @pl.when(pl.program_id(1) == 0)
        def _():
            state[...] = jnp.zeros_like(state)

        m, xi_c, zeta_c, gc = mask_ref[...], xi_ref[...], zeta_ref[...], gc_ref[0:1, 0:1]
        for ci in range(cpb):
            rows = pl.ds(ci * CHUNK, CHUNK)
            cs, sn = cos_ref[rows, :], sin_ref[rows, :]
            qr = _rot(q_ref[rows, :], cs, sn)
            kr = _rot(k_ref[rows, :], cs, sn) * (DK ** -0.5)
            vb = v_ref[rows, :].astype(BF16)
            qb = qr.astype(BF16)
            st = state[...]
            stb = st.astype(BF16)
            st_ref[ci] = stb
            scores = (_dot_nt(qb, kr.astype(BF16)) * m).astype(BF16)
            o = _dot(scores, vb) + _dot(qb, stb) * xi_c
            state[...] = st * gc + _dot_tn((kr * zeta_c).astype(BF16), vb)
            o_ref[rows, :] = o
            mu = jnp.mean(o, axis=-1, keepdims=True)
            oc = o - mu
            var = jnp.mean(oc * oc, axis=-1, keepdims=True)
            pa_ref[rows, :] = (oc * lax.rsqrt(var + GN_EPS) * _silu(z_ref[rows, :])).astype(BF16)

    tab = lambda shape: pl.BlockSpec((None,) + shape, lambda h, n: (h, 0, 0))
    return pl.pallas_call(
        body, name="retention_fwd", grid=(HEADS, S // tb),
        in_specs=[pl.BlockSpec((tb, DK), lambda h, n: (n, h)),
                  pl.BlockSpec((tb, DK), lambda h, n: (n, HEADS + h)),
                  pl.BlockSpec((tb, DV), lambda h, n: (n, COL_V * 2 + h)),
                  pl.BlockSpec((tb, DV), lambda h, n: (n, COL_Z * 2 + h)),
                  pl.BlockSpec((tb, DK // 2), lambda h, n: (n, 0)),
                  pl.BlockSpec((tb, DK // 2), lambda h, n: (n, 0)),
                  tab((CHUNK, CHUNK)), tab((CHUNK, 1)), tab((CHUNK, 1)), tab((1, 128))],
        out_specs=[pl.BlockSpec((tb, DV), lambda h, n: (n, h)),
                   pl.BlockSpec((tb, DV), lambda h, n: (n, h)),
                   pl.BlockSpec((None, cpb, DK, DV), lambda h, n: (h, n, 0, 0))],
        out_shape=[jax.ShapeDtypeStruct((S, HEADS * DV), BF16), jax.ShapeDtypeStruct((S, HEADS * DV), F32),
                   jax.ShapeDtypeStruct((HEADS, S // CHUNK, DK, DV), BF16)],
        scratch_shapes=[pltpu.VMEM((DK, DV), F32)],
        compiler_params=_params(),
    )(proj, proj, proj, proj, cos, sin, mask, xi, zeta, g_chunk)


SUBLANES = 8
FWD_ROWS = 16
EW_ROWS = 16


def _shifted_copies(src, dst):
    rows = dst.shape[1]
    src[pl.ds(rows, SUBLANES), :] = jnp.zeros((SUBLANES, src.shape[1]), src.dtype)
    for s in range(SUBLANES):
        dst[s] = src[pl.ds(s, rows), :]


def _layernorm_stats(a):
    mu = jnp.mean(a, axis=-1, keepdims=True)
    ac = a - mu
    var = jnp.mean(ac * ac, axis=-1, keepdims=True)
    rstd = lax.rsqrt(var + LN_EPS)
    return ac * rstd, rstd


def _conv_fwd(proj, conv_w, conv_b, ln_g, ln_b, tm):
    S = proj.shape[0]
    hb = tm // HALO

    def body(uv_ref, ug_ref, uvh_ref, ugh_ref, z_ref, w_ref, b_ref, g_ref, bb_ref, pb_ref, a1_ref, ext, ex):
        halo = _round_bf16(uvh_ref[...] * _sigmoid(ugh_ref[...]))
        ext[pl.ds(0, HALO), :] = jnp.where(pl.program_id(0) == 0, 0.0, halo)
        ext[pl.ds(HALO, tm), :] = _round_bf16(uv_ref[...] * _sigmoid(ug_ref[...]))
        _shifted_copies(ext, ex)
        bias, g, bb = b_ref[...], g_ref[...], bb_ref[...]

        def rows_body(rb, carry):
            r0 = pl.multiple_of(rb * FWD_ROWS, FWD_ROWS)
            groups = range(FWD_ROWS // SUBLANES)
            acc = [jnp.zeros((SUBLANES, D), F32) for _ in groups]
            for j in range(CONV_W):
                a, s = divmod(HALO - (CONV_W - 1) + j, SUBLANES)
                wj = w_ref[j]
                for k in groups:
                    acc[k] = acc[k] + wj * ex[s, pl.ds(r0 + (a + k) * SUBLANES, SUBLANES), :]
            for k in groups:
                a1_ref[pl.ds(r0 + k * SUBLANES, SUBLANES), :] = acc[k] + bias
            return carry

        lax.fori_loop(0, tm // FWD_ROWS, rows_body, 0)
        xhat, _ = _layernorm_stats(a1_ref[...])
        pb_ref[...] = (_silu(xhat * g + bb) * _silu(z_ref[...])).astype(BF16)

    vec = pl.BlockSpec((1, D), lambda i: (0, 0))
    halo = lambda col: pl.BlockSpec((HALO, D), lambda i: (jnp.maximum(i * hb - 1, 0), col))
    return pl.pallas_call(
        body, name="conv_fwd", grid=(S // tm,),
        in_specs=[pl.BlockSpec((tm, D), lambda i: (i, COL_UVAL)), pl.BlockSpec((tm, D), lambda i: (i, COL_UGATE)),
                  halo(COL_UVAL), halo(COL_UGATE), pl.BlockSpec((tm, D), lambda i: (i, COL_ZCONV)),
                  pl.BlockSpec((CONV_W + 1, SUBLANES, D), lambda i: (0, 0, 0)), vec, vec, vec],
        out_specs=[pl.BlockSpec((tm, D), lambda i: (i, 0))] * 2,
        out_shape=[jax.ShapeDtypeStruct((S, D), BF16), jax.ShapeDtypeStruct((S, D), F32)],
        scratch_shapes=[pltpu.VMEM((HALO + tm + SUBLANES, D), F32), pltpu.VMEM((SUBLANES, HALO + tm, D), F32)],
        compiler_params=_params(),
    )(proj, proj, proj, proj, proj, conv_w, conv_b, ln_g, ln_b)


def _head(p_a, p_b, proj, x, target, gate, g_post, w_ret, w_conv, w_out, tm):
    S = x.shape[0]

    def body(pa_ref, pb_ref, ga_ref, gb_ref, x_ref, t_ref, gate_ref, gp_ref, wr_ref, wc_ref, wo_ref,
             dout_ref, dg_ref, dpa_ref, dpb_ref, mb_ref, dy_ref, dya_ref, dyb_ref, small_ref):
        @pl.when(pl.program_id(0) == 0)
        def _():
            small_ref[...] = jnp.zeros_like(small_ref)

        gate, gp = gate_ref[...], gp_ref[...]
        ya = _dot(pa_ref[...], wr_ref[...])
        yb = _dot(pb_ref[...], wc_ref[...])
        sa, sb = _sigmoid(ga_ref[...]), _sigmoid(gb_ref[...])
        mb = (sa * ya + sb * yb).astype(BF16)
        mb_ref[...] = mb
        y = _dot(mb, wo_ref[...])
        r = lax.rsqrt(jnp.mean(y * y, axis=-1, keepdims=True) + RMS_EPS)
        yhat = y * r
        yn = yhat * gp
        err = (x_ref[...] + gate * yn) - t_ref[...]
        small_ref[2:3, :] += 0.5 * jnp.sum(jnp.mean(err * err, axis=-1, keepdims=True))
        dout = err * (1.0 / D)
        dout_ref[...] = dout
        small_ref[0:1, :] += jnp.sum(dout * yn, axis=0, keepdims=True)
        dyn = dout * gate
        small_ref[1:2, :] += jnp.sum(dyn * yhat, axis=0, keepdims=True)
        dyhat = dyn * gp
        dy = (r * (dyhat - yhat * jnp.mean(dyhat * yhat, axis=-1, keepdims=True))).astype(BF16)
        dy_ref[...] = dy
        dmerged = _dot_nt(dy, wo_ref[...])
        dya = dmerged * sa
        dyb = dmerged * sb
        dg_ref[:, 0:D] = (dya * ya * (1.0 - sa)).astype(BF16)
        dg_ref[:, D:2 * D] = (dyb * yb * (1.0 - sb)).astype(BF16)
        dya, dyb = dya.astype(BF16), dyb.astype(BF16)
        dya_ref[...] = dya
        dyb_ref[...] = dyb
        dpa_ref[...] = _dot_nt(dya, wr_ref[...])
        dpb_ref[...] = _dot_nt(dyb, wc_ref[...])

    vec = pl.BlockSpec((1, D), lambda i: (0, 0))
    tile = lambda w, col=0: pl.BlockSpec((tm, w), lambda i: (i, col))
    whole = lambda a: pl.BlockSpec(a.shape, lambda i: (0, 0), pipeline_mode=pl.Buffered(1))
    act = lambda w: jax.ShapeDtypeStruct((S, w), BF16)
    return pl.pallas_call(
        body, name="head", grid=(S // tm,),
        in_specs=[tile(2 * D), tile(D), tile(D, COL_GA), tile(D, COL_GB), tile(D), tile(D), vec, vec,
                  whole(w_ret), whole(w_conv), whole(w_out)],
        out_specs=[tile(D), tile(2 * D), tile(2 * D), tile(D),
                   tile(D), tile(D), tile(D), tile(D), pl.BlockSpec((8, D), lambda i: (0, 0))],
        out_shape=[jax.ShapeDtypeStruct((S, D), F32),
                   act(2 * D), jax.ShapeDtypeStruct((S, 2 * D), F32), jax.ShapeDtypeStruct((S, D), F32),
                   act(D), act(D), act(D), act(D), jax.ShapeDtypeStruct((8, D), F32)],
        compiler_params=_params(),
    )(p_a, p_b, proj, proj, x, target, gate, g_post, w_ret, w_conv, w_out)


def _matmul_tn(a, b, name, bn, tk):
    S, M = a.shape
    N = b.shape[1]

    def body(a_ref, b_ref, o_ref):
        @pl.when(pl.program_id(1) == 0)
        def _():
            o_ref[...] = jnp.zeros_like(o_ref)

        o_ref[...] += _dot_tn(a_ref[...], b_ref[...])

    return pl.pallas_call(
        body, name=name, grid=(N // bn, S // tk),
        in_specs=[pl.BlockSpec((tk, M), lambda j, k: (k, 0)), pl.BlockSpec((tk, bn), lambda j, k: (k, j))],
        out_specs=pl.BlockSpec((M, bn), lambda j, k: (0, j)),
        out_shape=jax.ShapeDtypeStruct((M, N), F32),
        compiler_params=_params(),
    )(a, b)


SMALL_B_ROWS = 40


def _dsilu(z, s):
    return s * (1.0 + z * (1.0 - s))


def _conv_bwd(dp_b, proj, a1, conv_w, ln_g, ln_b, tm):
    S = proj.shape[0]
    nt = S // tm

    def body(dpb_ref, uv_ref, ug_ref, z_ref, a1_ref, w_ref, g_ref, bb_ref,
             dc_ref, small_ref, dext, dx, carry, dw_acc):
        @pl.when(pl.program_id(0) == 0)
        def _():
            small_ref[...] = jnp.zeros_like(small_ref)
            carry[...] = jnp.zeros_like(carry)

        xhat, rstd = _layernorm_stats(a1_ref[...])
        g = g_ref[...]
        a2 = xhat * g + bb_ref[...]
        s2 = _sigmoid(a2)
        zc = z_ref[...]
        sz = _sigmoid(zc)
        dpb = dpb_ref[...]
        dc_ref[:, 2 * D:3 * D] = (dpb * (a2 * s2) * _dsilu(zc, sz)).astype(BF16)
        da2 = dpb * (zc * sz) * _dsilu(a2, s2)
        small_ref[32:33, :] += jnp.sum(da2 * xhat, axis=0, keepdims=True)
        small_ref[33:34, :] += jnp.sum(da2, axis=0, keepdims=True)
        dxhat = da2 * g
        da1 = rstd * (dxhat - jnp.mean(dxhat, axis=-1, keepdims=True)
                      - xhat * jnp.mean(dxhat * xhat, axis=-1, keepdims=True))
        small_ref[31:32, :] += jnp.sum(da1, axis=0, keepdims=True)
        da1 = _round_bf16(da1)
        dext[pl.ds(0, tm), :] = da1
        dext[pl.ds(tm, HALO), :] = carry[...]
        carry[...] = da1[0:HALO, :]
        _shifted_copies(dext, dx)
        dw_acc[...] = jnp.zeros_like(dw_acc)

        def conv_body(rb, c):
            r0 = pl.multiple_of(rb * EW_ROWS, EW_ROWS)
            rows = pl.ds(r0, EW_ROWS)
            uv = uv_ref[rows, :]
            sg = _sigmoid(ug_ref[rows, :])
            a0 = _round_bf16(uv * sg)
            halves = []
            for k in range(EW_ROWS // SUBLANES):
                a0_k = a0[k * SUBLANES:(k + 1) * SUBLANES, :]
                acc = jnp.zeros((SUBLANES, D), F32)
                for j in range(CONV_W):
                    a, s = divmod(CONV_W - 1 - j, SUBLANES)
                    win = dx[s, pl.ds(r0 + (a + k) * SUBLANES, SUBLANES), :]
                    acc = acc + w_ref[j] * win
                    dw_acc[j] += a0_k * win
                halves.append(acc)
            da0 = jnp.concatenate(halves, axis=0)
            dc_ref[rows, 0:D] = (da0 * sg).astype(BF16)
            dc_ref[rows, D:2 * D] = (da0 * uv * sg * (1.0 - sg)).astype(BF16)
            return c

        lax.fori_loop(0, tm // EW_ROWS, conv_body, 0)
        small_ref[0:CONV_W + 1, :] += jnp.sum(dw_acc[...], axis=1)

    vec = pl.BlockSpec((1, D), lambda i: (0, 0))
    tile = lambda col: pl.BlockSpec((tm, D), lambda i: (nt - 1 - i, col))
    return pl.pallas_call(
        body, name="conv_bwd", grid=(nt,),
        in_specs=[tile(0), tile(COL_UVAL), tile(COL_UGATE), tile(COL_ZCONV), tile(0),
                  pl.BlockSpec((CONV_W + 1, SUBLANES, D), lambda i: (0, 0, 0)), vec, vec],
        out_specs=[pl.BlockSpec((tm, 3 * D), lambda i: (nt - 1 - i, 0)),
                   pl.BlockSpec((SMALL_B_ROWS, D), lambda i: (0, 0))],
        out_shape=[jax.ShapeDtypeStruct((S, 3 * D), BF16), jax.ShapeDtypeStruct((SMALL_B_ROWS, D), F32)],
        scratch_shapes=[pltpu.VMEM((tm + HALO + SUBLANES, D), F32), pltpu.VMEM((SUBLANES, tm + HALO, D), F32),
                        pltpu.VMEM((HALO, D), F32), pltpu.VMEM((CONV_W + 1, SUBLANES, D), F32)],
        compiler_params=_params(),
    )(dp_b, proj, proj, proj, a1, conv_w, ln_g, ln_b)


RET_W = 2 * DK + 2 * DV


def _retention_bwd(dp_a, o, proj, states, cos, sin, tables, cpb):
    S = proj.shape[0]
    tb = cpb * CHUNK
    nb = S // tb
    mask, xi, zeta, g_chunk = tables

    def body(dpa_ref, o_ref, q_ref, k_ref, v_ref, z_ref, st_ref, cos_ref, sin_ref, mask_ref, xi_ref, zeta_ref,
             gc_ref, out_ref, dstate):
        @pl.when(pl.program_id(1) == 0)
        def _():
            dstate[...] = jnp.zeros_like(dstate)

        m, xi_c, zeta_c, gc = mask_ref[...], xi_ref[...], zeta_ref[...], gc_ref[0:1, 0:1]
        for ci in reversed(range(cpb)):
            rows = pl.ds(ci * CHUNK, CHUNK)
            o = o_ref[rows, :]
            oc = o - jnp.mean(o, axis=-1, keepdims=True)
            rstd = lax.rsqrt(jnp.mean(oc * oc, axis=-1, keepdims=True) + GN_EPS)
            rhat = oc * rstd
            z = z_ref[rows, :]
            sz = _sigmoid(z)
            dpa = dpa_ref[rows, :]
            out_ref[rows, 2 * DK + DV:RET_W] = (dpa * rhat * _dsilu(z, sz)).astype(BF16)
            dret = dpa * (z * sz)
            d_o = rstd * (dret - jnp.mean(dret, axis=-1, keepdims=True)
                          - rhat * jnp.mean(dret * rhat, axis=-1, keepdims=True))
            cs, sn = cos_ref[rows, :], sin_ref[rows, :]
            qb = _rot(q_ref[rows, :], cs, sn).astype(BF16)
            kr = _rot(k_ref[rows, :], cs, sn) * (DK ** -0.5)
            kb = kr.astype(BF16)
            kz = (kr * zeta_c).astype(BF16)
            vb = v_ref[rows, :].astype(BF16)
            scores = (_dot_nt(qb, kb) * m).astype(BF16)
            dob = d_o.astype(BF16)
            dox = (d_o * xi_c).astype(BF16)
            dscores = (_dot_nt(dob, vb) * m).astype(BF16)
            dst = dstate[...]
            dstb = dst.astype(BF16)
            dqr = _dot(dscores, kb) + _dot_nt(dox, st_ref[ci])
            dkr = _dot_tn(dscores, qb) + _dot_nt(vb, dstb) * zeta_c
            dv = _dot_tn(scores, dob) + _dot(kz, dstb)
            dstate[...] = dst * gc + _dot_tn(qb, dox)
            out_ref[rows, 0:DK] = _rot_back(dqr, cs, sn).astype(BF16)
            out_ref[rows, DK:2 * DK] = (_rot_back(dkr, cs, sn) * (DK ** -0.5)).astype(BF16)
            out_ref[rows, 2 * DK:2 * DK + DV] = dv.astype(BF16)

    tab = lambda shape: pl.BlockSpec((None,) + shape, lambda h, n: (h, 0, 0))
    blk = lambda w, col0: pl.BlockSpec((tb, w), lambda h, n: (nb - 1 - n, col0 + h))
    return pl.pallas_call(
        body, name="retention_bwd", grid=(HEADS, nb),
        in_specs=[blk(DV, 0), blk(DV, 0), blk(DK, 0), blk(DK, HEADS), blk(DV, COL_V * 2), blk(DV, COL_Z * 2),
                  pl.BlockSpec((None, cpb, DK, DV), lambda h, n: (h, nb - 1 - n, 0, 0)),
                  pl.BlockSpec((tb, DK // 2), lambda h, n: (nb - 1 - n, 0)),
                  pl.BlockSpec((tb, DK // 2), lambda h, n: (nb - 1 - n, 0)),
                  tab((CHUNK, CHUNK)), tab((CHUNK, 1)), tab((CHUNK, 1)), tab((1, 128))],
        out_specs=blk(RET_W, 0),
        out_shape=jax.ShapeDtypeStruct((S, HEADS * RET_W), BF16),
        scratch_shapes=[pltpu.VMEM((DK, DV), F32)],
        compiler_params=_params(),
    )(dp_a, o, proj, proj, proj, proj, states, cos, sin, mask, xi, zeta, g_chunk)


def _to_head_major(w):
    parts = []
    for h in range(HEADS):
        parts += [w[:, h * DK:(h + 1) * DK], w[:, D + h * DK:D + (h + 1) * DK],
                  w[:, 2 * D + h * DV:2 * D + (h + 1) * DV], w[:, 4 * D + h * DV:4 * D + (h + 1) * DV]]
    return jnp.concatenate(parts, axis=1)


def _from_head_major(w):
    starts = (0, DK, 2 * DK, 2 * DK + DV)
    widths = (DK, DK, DV, DV)
    parts = [w[:, h * RET_W + s:h * RET_W + s + n] for s, n in zip(starts, widths) for h in range(HEADS)]
    return jnp.concatenate(parts, axis=1)


def _dh(d_r, d_c, d_g, w_r, w_c, w_g, x, dout, g_pre, scale, tm):
    S = x.shape[0]

    def body(dr_ref, dc_ref, dg_ref, wr_ref, wc_ref, wg_ref, x_ref, dout_ref, g_ref, sc_ref, gx_ref, small_ref):
        @pl.when(pl.program_id(0) == 0)
        def _():
            small_ref[...] = jnp.zeros_like(small_ref)

        dh = _dot_nt(dr_ref[...], wr_ref[...]) + _dot_nt(dc_ref[...], wc_ref[...]) + _dot_nt(dg_ref[...], wg_ref[...])
        xf = x_ref[...]
        r = lax.rsqrt(jnp.mean(xf * xf, axis=-1, keepdims=True) + RMS_EPS)
        xhat = xf * r
        g, sc1 = g_ref[...], 1.0 + sc_ref[...]
        small_ref[0:1, :] += jnp.sum(dh, axis=0, keepdims=True)
        small_ref[1:2, :] += jnp.sum(dh * (xhat * g), axis=0, keepdims=True)
        small_ref[2:3, :] += jnp.sum(dh * sc1 * xhat, axis=0, keepdims=True)
        dxhat = dh * sc1 * g
        gx_ref[...] = dout_ref[...] + r * (dxhat - xhat * jnp.mean(dxhat * xhat, axis=-1, keepdims=True))

    vec = pl.BlockSpec((1, D), lambda i: (0, 0))
    tile = lambda w: pl.BlockSpec((tm, w), lambda i: (i, 0))
    whole = lambda a: pl.BlockSpec(a.shape, lambda i: (0, 0), pipeline_mode=pl.Buffered(1))
    return pl.pallas_call(
        body, name="dh", grid=(S // tm,),
        in_specs=[tile(d_r.shape[1]), tile(d_c.shape[1]), tile(d_g.shape[1]), whole(w_r), whole(w_c), whole(w_g),
                  tile(D), tile(D), vec, vec],
        out_specs=[tile(D), pl.BlockSpec((8, D), lambda i: (0, 0))],
        out_shape=[jax.ShapeDtypeStruct((S, D), F32), jax.ShapeDtypeStruct((8, D), F32)],
        compiler_params=_params(),
    )(d_r, d_c, d_g, w_r, w_c, w_g, x, dout, g_pre, scale)


ROW_GATE, ROW_POST, ROW_LOSS = 0, 1, 2
ROW_CONV_W, ROW_CONV_B, ROW_LN_G, ROW_LN_B = 8, 8 + 31, 8 + 32, 8 + 33
ROW_SHIFT, ROW_SCALE, ROW_PRE = 48, 49, 50


def kernel(x, c, positions, w_ada, b_ada, pre_norm_g, w_in, conv_w, conv_b, conv_ln_g, conv_ln_b, w_ret_out, w_conv_out, w_out, post_norm_g, loss_target, m_w_ada, m_b_ada, m_pre_norm_g, m_w_in, m_conv_w, m_conv_b, m_conv_ln_g, m_conv_ln_b, m_w_ret_out, m_w_conv_out, m_w_out, m_post_norm_g, v_w_ada, v_b_ada, v_pre_norm_g, v_w_in, v_conv_w, v_conv_b, v_conv_ln_g, v_conv_ln_b, v_w_ret_out, v_w_conv_out, v_w_out, v_post_norm_g):
    S = x.shape[1]
    me = 4 * lax.axis_index("x") + 2 * lax.axis_index("y") + lax.axis_index("c")
    x2, target = x[0], loss_target[0]
    ada_w = w_ada.shape[2]
    cw_w = conv_w.shape[2]

    c_g, conv_w_g = _exchange([jnp.pad(c, ((0, 7), (0, 0))), jnp.pad(conv_w[0], ((0, 1), (0, 0)))],
                              False, "gather_c_conv_w", True)
    c_all = c_g[:, 0, :]
    conv_w_full = _round_bf16(conv_w_g.transpose(1, 0, 2).reshape(CONV_W + 1, D))
    conv_w_full = jnp.broadcast_to(conv_w_full[:, None, :], (CONV_W + 1, SUBLANES, D))
    b_part = lax.dynamic_slice(b_ada, (0, me * ada_w), (1, ada_w))
    mod_g = _exchange([_mod_part(c_all, w_ada[0], b_part)], False, "gather_mod", True)[0]

    own_in, mod_g = lax.optimization_barrier((w_in[0].astype(BF16), mod_g))
    in_send, in_recv, (in_sent,), (in_zone,), tok_in = _exchange_start([own_in], False, "gather_w_in_start")
    own_rest = [(w_ret_out[0] + tok_in[0:1, 0:1]).astype(BF16), w_conv_out[0].astype(BF16), w_out[0].astype(BF16)]
    *rest_handles, tok_rest = _exchange_start(own_rest, False, "gather_w_rest_start")
    mod = lax.dynamic_index_in_dim(mod_g, me, axis=1, keepdims=False).reshape(1, 3 * D)
    shift, scale, gate = mod[:, :D], mod[:, D:2 * D], mod[:, 2 * D:]
    positions = positions + tok_rest[0:1, 0:1].astype(jnp.int32)

    cos, sin = _rope_tables(positions, S)
    tables = _decay_tables()
    proj, h = _inproj_fwd(x2, pre_norm_g, scale, shift, in_zone, in_recv, in_sent, tm=min(S, 1024))
    g_in, proj = lax.optimization_barrier((in_zone, proj))
    gate = gate + _gather_sends_wait(in_send, in_sent, h, "gather_w_in_sends_wait")[0:1, 0:1]
    w_in_full = g_in.transpose(1, 0, 2).reshape(D, IN_W)
    p_a, o, states = _retention_fwd(proj, cos, sin, tables, cpb=8)
    p_b, a1 = _conv_fwd(proj, conv_w_full, conv_b, conv_ln_g, conv_ln_b, tm=256)
    g_ret, g_conv, g_out = _exchange_wait(*rest_handles, p_b, False, "gather_w_rest_wait")
    w_ret_full, w_conv_full, w_out_full = g_ret.reshape(2 * D, D), g_conv.reshape(D, D), g_out.reshape(D, D)

    dout, d_g, dp_a, dp_b, merged, dy, dy_a, dy_b, small_a = _head(
        p_a, p_b, proj, x2, target, gate, post_norm_g, w_ret_full, w_conv_full, w_out_full, tm=256)
    parts_rest = [_matmul_tn(p_a, dy_a, "dw_ret_out", bn=D, tk=min(S, 2048)).astype(BF16).reshape(N_DEV, 2 * D // N_DEV, D),
                  _matmul_tn(p_b, dy_b, "dw_conv_out", bn=D, tk=min(S, 2048)).astype(BF16).reshape(N_DEV, D // N_DEV, D),
                  _matmul_tn(merged, dy, "dw_out", bn=D, tk=min(S, 2048)).astype(BF16).reshape(N_DEV, D // N_DEV, D)]
    *s_rest_handles, tok_s_rest = _exchange_start(parts_rest, True, "scatter_rest_start")
    d_c, small_b = _conv_bwd(dp_b, proj, a1, conv_w_full, conv_ln_g + tok_s_rest[0:1, 0:1], conv_ln_b, tm=256)
    d_r = _retention_bwd(dp_a, o, proj, states, cos, sin, tables, cpb=8)
    dw_r = _matmul_tn(h, d_r, "dw_in_ret", bn=RET_W, tk=min(S, 2048))
    dw_c = _matmul_tn(h, d_c, "dw_in_conv", bn=D, tk=min(S, 2048))
    dw_g = _matmul_tn(h, d_g, "dw_in_gate", bn=D, tk=min(S, 2048))
    dw_in = jnp.concatenate([_from_head_major(dw_r), dw_c, dw_g], axis=1)
    parts_in = dw_in.astype(BF16).reshape(D, N_DEV, W_SHARD).transpose(1, 0, 2)
    *s_in_handles, tok_s_in = _exchange_start([parts_in], True, "scatter_w_in_start")
    grad_x, small_c = _dh(d_r, d_c, d_g, _to_head_major(w_in_full), w_in_full[:, 6 * D:9 * D], w_in_full[:, 9 * D:],
                          x2, dout, pre_norm_g + tok_s_in[0:1, 0:1], scale, tm=256)

    r_ret, r_conv, r_out = _exchange_wait(*s_rest_handles, small_c, True, "scatter_rest_wait")
    (r_in,) = _exchange_wait(*s_in_handles, small_c, True, "scatter_w_in_wait")
    big = {"w_in": _adam_shard(r_in, w_in[0], m_w_in[0], v_w_in[0], "adam_w_in", 128),
           "w_ret_out": _adam_shard(r_ret, w_ret_out[0], m_w_ret_out[0], v_w_ret_out[0], "adam_w_ret_out", 128),
           "w_conv_out": _adam_shard(r_conv, w_conv_out[0], m_w_conv_out[0], v_w_conv_out[0], "adam_w_conv_out", 128),
           "w_out": _adam_shard(r_out, w_out[0], m_w_out[0], v_w_out[0], "adam_w_out", 128)}

    sm_g = _exchange([jnp.concatenate([small_a, small_b, small_c], axis=0)], False, "gather_small", True)[0]
    sm = _sum_devices(sm_g)
    row = lambda r: sm[r:r + 1]
    g8 = jnp.concatenate([row(ROW_SHIFT), row(ROW_SCALE), row(ROW_GATE), row(ROW_PRE), row(ROW_CONV_B),
                          row(ROW_LN_G), row(ROW_LN_B), row(ROW_POST)], axis=0)
    stack8 = lambda ba, pre, cb, lg, lb, post: jnp.concatenate([ba.reshape(3, D), pre, cb, lg, lb, post], axis=0)
    d8, m8, v8 = _adam_small(
        g8, stack8(b_ada, pre_norm_g, conv_b, conv_ln_g, conv_ln_b, post_norm_g),
        stack8(m_b_ada, m_pre_norm_g, m_conv_b, m_conv_ln_g, m_conv_ln_b, m_post_norm_g),
        stack8(v_b_ada, v_pre_norm_g, v_conv_b, v_conv_ln_g, v_conv_ln_b, v_post_norm_g), "adam_small")
    unstack8 = lambda a: {"b_ada": a[0:3].reshape(3 * D), "pre_norm_g": a[3], "conv_b": a[4],
                          "conv_ln_g": a[5], "conv_ln_b": a[6], "post_norm_g": a[7]}
    g_small, d_small, m_small, v_small = unstack8(g8), unstack8(d8), unstack8(m8), unstack8(v8)

    pad_row = lambda a: jnp.pad(a[0], ((0, 1), (0, 0)))
    g_cw = lax.dynamic_slice(sm[ROW_CONV_W:ROW_CONV_W + CONV_W + 1], (0, me * cw_w), (CONV_W + 1, cw_w))
    d_cw, m_cw, v_cw = _adam_small(g_cw, pad_row(conv_w), pad_row(m_conv_w), pad_row(v_conv_w), "adam_conv_w")

    dmod_all = jnp.concatenate([sm_g[:, ROW_SHIFT], sm_g[:, ROW_SCALE], sm_g[:, ROW_GATE]], axis=1)
    g_wa = _grad_w_ada(c_all.T, lax.dynamic_slice(dmod_all, (0, me * ada_w), (N_DEV, ada_w)))
    d_wa, m_wa, v_wa = _adam_small(g_wa, w_ada[0], m_w_ada[0], v_w_ada[0], "adam_w_ada")

    grads = {"w_ada": g_wa, "conv_w": g_cw[:CONV_W], **g_small, **{n: t[0] for n, t in big.items()}}
    deltas = {"w_ada": d_wa, "conv_w": d_cw[:CONV_W], **d_small, **{n: t[1] for n, t in big.items()}}
    new_m = {"w_ada": m_wa, "conv_w": m_cw[:CONV_W], **m_small, **{n: t[2] for n, t in big.items()}}
    new_v = {"w_ada": v_wa, "conv_w": v_cw[:CONV_W], **v_small, **{n: t[3] for n, t in big.items()}}
    order = ["w_ada", "b_ada", "pre_norm_g", "w_in", "conv_w", "conv_b", "conv_ln_g", "conv_ln_b", "w_ret_out",
             "w_conv_out", "w_out", "post_norm_g"]
    loss = sm[ROW_LOSS, 0]
    out = [loss, grad_x[None]]
    for group in (grads, deltas, new_m, new_v):
        out += [group[n][None] for n in order]
    return tuple(out)
```

```python
import jax
import jax.numpy as jnp
import numpy as np
from jax import lax
from jax.experimental import pallas as pl
from jax.experimental.pallas import tpu as pltpu

F32 = jnp.float32
BF16 = jnp.bfloat16
MESH = pl.DeviceIdType.MESH

N_DEV = 8
D = 1024
HEADS = 4
DK = 256
DV = 512
CHUNK = 128
CONV_W = 31
HALO = 32
IN_W = 11264
W_SHARD = IN_W // N_DEV
ROPE_BASE = 10000.0
GN_EPS = 1e-5
LN_EPS = 1e-5
RMS_EPS = 1e-6
ADAM_LR, ADAM_B1, ADAM_B2, ADAM_EPS, ADAM_WD, ADAM_STEP = 0.001, 0.9, 0.999, 1e-08, 0.01, 10

COL_V, COL_Z, COL_UVAL, COL_UGATE, COL_ZCONV, COL_GA, COL_GB = 2, 4, 6, 7, 8, 9, 10

VMEM_LIMIT = 56 * 1024 * 1024


def _params(**kw):
    return pltpu.CompilerParams(vmem_limit_bytes=VMEM_LIMIT, **kw)


def _sigmoid(z):
    return jax.nn.sigmoid(z)


def _round_bf16(a):
    return a.astype(BF16).astype(F32)


def _dot(a, b):
    return jnp.dot(a, b, preferred_element_type=F32)


def _dot_nt(a, b):
    return lax.dot_general(a, b, (((1,), (1,)), ((), ())), preferred_element_type=F32)


def _dot_tn(a, b):
    return lax.dot_general(a, b, (((0,), (0,)), ((), ())), preferred_element_type=F32)


def _my_place():
    return lax.axis_index("x"), lax.axis_index("y"), lax.axis_index("c")


def _peer(k):
    x, y, c = _my_place()
    px = lax.rem(x + ((k >> 2) & 1), 2)
    py = lax.rem(y + ((k >> 1) & 1), 2)
    pc = lax.rem(c + (k & 1), 2)
    return (px, py, pc), 4 * px + 2 * py + pc


def _exchange(arrs, scatter, name, in_vmem):
    n = len(arrs)

    def body(*refs):
        ins, outs = refs[:n], refs[n:2 * n]
        send_sems, recv_sems, local_sems = refs[2 * n:]
        x, y, c = _my_place()
        me = 4 * x + 2 * y + c
        copies = []
        for i in range(n):
            src = ins[i].at[me] if scatter else ins[i]
            cp = pltpu.make_async_copy(src, outs[i].at[me], local_sems.at[i])
            cp.start()
            copies.append(cp)
        for k in range(1, N_DEV):
            peer, peer_idx = _peer(k)
            for i in range(n):
                src = ins[i].at[peer_idx] if scatter else ins[i]
                cp = pltpu.make_async_remote_copy(
                    src_ref=src, dst_ref=outs[i].at[me], send_sem=send_sems.at[i, k - 1],
                    recv_sem=recv_sems.at[i, k - 1], device_id=peer, device_id_type=MESH)
                cp.start()
                copies.append(cp)
        for cp in copies:
            cp.wait()

    space = pltpu.VMEM if in_vmem else pl.ANY
    out_shape = [jax.ShapeDtypeStruct(a.shape if scatter else (N_DEV,) + a.shape, a.dtype) for a in arrs]
    return pl.pallas_call(
        body, name=name, out_shape=out_shape,
        in_specs=[pl.BlockSpec(memory_space=space)] * n,
        out_specs=[pl.BlockSpec(memory_space=space)] * n,
        scratch_shapes=[pltpu.SemaphoreType.DMA((n, N_DEV - 1)), pltpu.SemaphoreType.DMA((n, N_DEV - 1)),
                        pltpu.SemaphoreType.DMA((n,))],
        compiler_params=_params(has_side_effects=True),
    )(*arrs)


def _remote_copies(ins, lands, send_sems, recv_sems, scatter, first=0):
    x, y, c = _my_place()
    me = 4 * x + 2 * y + c
    copies = []
    for i in range(len(ins)):
        for k in range(1, N_DEV):
            peer, peer_idx = _peer(k)
            sem = (N_DEV - 1) * (first + i) + k - 1
            copies.append(pltpu.make_async_remote_copy(
                src_ref=ins[i].at[peer_idx] if scatter else ins[i], dst_ref=lands[i].at[me],
                send_sem=send_sems.at[sem], recv_sem=recv_sems.at[sem], device_id=peer, device_id_type=MESH))
    return copies


_HBM = pl.BlockSpec(memory_space=pltpu.HBM)
_SEM = pl.BlockSpec(memory_space=pltpu.SEMAPHORE)


def _exchange_start(arrs, scatter, name):
    n = len(arrs)
    me = 4 * lax.axis_index("x") + 2 * lax.axis_index("y") + lax.axis_index("c")
    lands = []
    for a in arrs:
        own = lax.dynamic_index_in_dim(a, me, axis=0) if scatter else a[None]
        zone = lax.empty(a.shape if scatter else (N_DEV,) + a.shape, a.dtype)
        lands.append(lax.dynamic_update_slice(zone, own, (me,) + (0,) * (own.ndim - 1)))

    def body(*refs):
        ins, land_refs, send_sems, recv_sems, token = refs[:n], refs[n:2 * n], refs[2 * n], refs[2 * n + 1], refs[-1]
        for cp in _remote_copies(ins, land_refs, send_sems, recv_sems, scatter):
            cp.start()
        token[...] = jnp.zeros_like(token)

    n_sem = (N_DEV - 1) * n
    hbm = lambda a: pltpu.HBM(a.shape, a.dtype)
    outs = pl.pallas_call(
        body, name=name,
        out_shape=(pltpu.SemaphoreType.DMA((n_sem,)), pltpu.SemaphoreType.DMA((n_sem,)), *[hbm(a) for a in arrs],
                   *[hbm(a) for a in lands], jax.ShapeDtypeStruct((8, 128), F32)),
        in_specs=(_HBM,) * (2 * n), out_specs=(_SEM, _SEM) + (_HBM,) * (2 * n) + (pl.BlockSpec(memory_space=pltpu.VMEM),),
        input_output_aliases={i: 2 + i for i in range(2 * n)},
        compiler_params=pltpu.CompilerParams(has_side_effects=pltpu.SideEffectType.DATAFLOW_SIDE_EFFECTING),
    )(*[pltpu.with_memory_space_constraint(a, pltpu.HBM) for a in list(arrs) + lands])
    return outs[0], outs[1], list(outs[2:2 + n]), list(outs[2 + n:2 + 2 * n]), outs[-1]


def _exchange_wait(send_sems, recv_sems, sent, zones, after, scatter, name, n_waited=0):
    n = len(sent)
    m = n - n_waited

    def body(*refs):
        early, ins, land_refs = refs[:n_waited], refs[n_waited:n], refs[n:n + m]
        s_sems, r_sems = refs[n + m], refs[n + m + 1]
        for i in range(n_waited):
            for k in range(1, N_DEV):
                pltpu.make_async_remote_copy(
                    src_ref=early[i], dst_ref=early[i], send_sem=s_sems.at[(N_DEV - 1) * i + k - 1],
                    recv_sem=s_sems.at[(N_DEV - 1) * i + k - 1], device_id=_peer(k)[0], device_id_type=MESH).wait_send()
        for cp in _remote_copies(ins, land_refs, s_sems, r_sems, scatter, first=n_waited):
            cp.wait_send()
            cp.wait_recv()

    thru = list(sent[n_waited:]) + list(zones[n_waited:])
    outs = pl.pallas_call(
        body, name=name, out_shape=tuple(pltpu.HBM(a.shape, a.dtype) for a in thru),
        in_specs=(_HBM,) * (n_waited + 2 * m) + (_SEM, _SEM, pl.BlockSpec(memory_space=pl.ANY)),
        out_specs=(_HBM,) * (2 * m), input_output_aliases={n_waited + i: i for i in range(2 * m)},
        compiler_params=pltpu.CompilerParams(has_side_effects=pltpu.SideEffectType.DATAFLOW_SIDE_EFFECTING),
    )(*sent[:n_waited], *thru, send_sems, recv_sems, after)
    return list(outs[m:])


def _rope_tables(positions, S):
    tm = min(S, 1024)

    half = DK // 2
    inv_freq = (ROPE_BASE ** (-jnp.arange(half, dtype=F32) / half)).reshape(1, half)

    def body(pos_ref, f_ref, cos_ref, sin_ref):
        ang = pos_ref[...].astype(F32) * f_ref[...]
        cos_ref[...] = jnp.cos(ang)
        sin_ref[...] = jnp.sin(ang)

    return pl.pallas_call(
        body, name="rope_tables", grid=(S // tm,),
        in_specs=[pl.BlockSpec((tm, 1), lambda i: (i, 0)), pl.BlockSpec((1, half), lambda i: (0, 0))],
        out_specs=[pl.BlockSpec((tm, half), lambda i: (i, 0))] * 2,
        out_shape=[jax.ShapeDtypeStruct((S, half), F32)] * 2,
    )(positions.reshape(S, 1), inv_freq)


def _mod_part(c_all, w_ada, b_part):
    def body(c_ref, w_ref, b_ref, o_ref):
        o_ref[...] = _dot(c_ref[...].astype(BF16), w_ref[...].astype(BF16)) + b_ref[...]

    return pl.pallas_call(body, name="mod_part",
                          out_shape=jax.ShapeDtypeStruct((N_DEV, w_ada.shape[1]), F32))(c_all, w_ada, b_part)


def _sum_devices(g):
    def body(g_ref, o_ref):
        acc = g_ref[0]
        for d in range(1, N_DEV):
            acc = acc + g_ref[d]
        o_ref[...] = acc

    return pl.pallas_call(body, name="sum_devices", out_shape=jax.ShapeDtypeStruct(g.shape[1:], F32))(g)


def _grad_w_ada(c_t, dmod):
    def body(c_ref, d_ref, o_ref):
        acc = c_ref[:, 0:1] * d_ref[0:1, :]
        for b in range(1, N_DEV):
            acc = acc + c_ref[:, b:b + 1] * d_ref[b:b + 1, :]
        o_ref[...] = acc

    return pl.pallas_call(body, name="grad_w_ada",
                          out_shape=jax.ShapeDtypeStruct((c_t.shape[0], dmod.shape[1]), F32))(c_t, dmod)


def _adam_math(w, g, m, v):
    m = ADAM_B1 * m + (1.0 - ADAM_B1) * g
    v = ADAM_B2 * v + (1.0 - ADAM_B2) * (g * g)
    m_hat = m / (1.0 - ADAM_B1 ** ADAM_STEP)
    v_hat = v / (1.0 - ADAM_B2 ** ADAM_STEP)
    delta = -ADAM_LR * (m_hat / (jnp.sqrt(v_hat) + ADAM_EPS) + ADAM_WD * w)
    return delta, m, v


def _adam_small(g, w, m, v, name):
    def body(g_ref, w_ref, m_ref, v_ref, d_ref, nm_ref, nv_ref):
        d_ref[...], nm_ref[...], nv_ref[...] = _adam_math(w_ref[...], g_ref[...], m_ref[...], v_ref[...])

    return pl.pallas_call(body, name=name, out_shape=[jax.ShapeDtypeStruct(w.shape, F32)] * 3)(g, w, m, v)


def _adam_shard(parts, w, m, v, name, tr):
    R, L = w.shape

    def body(p_ref, w_ref, m_ref, v_ref, g_ref, d_ref, nm_ref, nv_ref):
        g = p_ref[0].astype(F32)
        for d in range(1, N_DEV):
            g = g + p_ref[d].astype(F32)
        g_ref[...] = g
        d_ref[...], nm_ref[...], nv_ref[...] = _adam_math(w_ref[...], g, m_ref[...], v_ref[...])

    blk = pl.BlockSpec((tr, L), lambda i: (i, 0))
    return pl.pallas_call(
        body, name=name, grid=(R // tr,),
        in_specs=[pl.BlockSpec((N_DEV, tr, L), lambda i: (0, i, 0)), blk, blk, blk],
        out_specs=[blk] * 4, out_shape=[jax.ShapeDtypeStruct((R, L), F32)] * 4,
        compiler_params=_params(),
    )(parts, w, m, v)


ARRIVAL = (0, 1, 2, 4, 3, 5, 6, 7)
LEAD = 2


def _inproj_fwd(x, g_pre, scale, shift, zone, recv_sems, sent, tm):
    S = x.shape[0]
    nt = S // tm
    me = 4 * lax.axis_index("x") + 2 * lax.axis_index("y") + lax.axis_index("c")
    order = jnp.bitwise_xor(me, jnp.asarray(ARRIVAL, jnp.int32))

    def body(order_ref, x_ref, g_ref, sc_ref, sh_ref, w_ref, sent_ref, sem_ref, proj_ref, h_ref, h_all):
        j, i = pl.program_id(0), pl.program_id(1)
        for row in range(1, N_DEV):
            @pl.when((j == row) & (i == 0))
            def _():
                k = ARRIVAL[row]
                pltpu.make_async_remote_copy(
                    src_ref=sent_ref, dst_ref=sent_ref, send_sem=sem_ref.at[k - 1],
                    recv_sem=sem_ref.at[k - 1], device_id=_peer(k)[0], device_id_type=MESH).wait_recv()

        rows = pl.ds(pl.multiple_of(jnp.maximum(i - LEAD, 0) * tm, tm), tm)

        @pl.when((j == 0) & (i >= LEAD))
        def _():
            xf = x_ref[...]
            r = lax.rsqrt(jnp.mean(xf * xf, axis=-1, keepdims=True) + RMS_EPS)
            h = ((xf * r * g_ref[...]) * (1.0 + sc_ref[...]) + sh_ref[...]).astype(BF16)
            h_all[rows, :] = h
            h_ref[...] = h

        @pl.when(i >= LEAD)
        def _():
            proj_ref[...] = _dot(h_all[rows, :], w_ref[...])

    tile = lambda j, i: jnp.maximum(i - LEAD, 0)
    first_pass = lambda j, i: jnp.where(j == 0, tile(j, i), nt - 1)
    vec = pl.BlockSpec((1, D), lambda j, i, o: (0, 0))
    any_space = pl.BlockSpec(memory_space=pl.ANY)
    return pl.pallas_call(
        body, name="inproj_fwd",
        grid_spec=pltpu.PrefetchScalarGridSpec(
            num_scalar_prefetch=1, grid=(N_DEV, nt + LEAD),
            in_specs=[pl.BlockSpec((tm, D), lambda j, i, o: (first_pass(j, i), 0)), vec, vec, vec,
                      pl.BlockSpec((None, D, W_SHARD),
                                   lambda j, i, o: (o[jnp.where(i >= LEAD, j, jnp.maximum(j - 1, 0))], 0, 0)),
                      any_space, _SEM],
            out_specs=[pl.BlockSpec((tm, W_SHARD), lambda j, i, o: (tile(j, i), o[j])),
                       pl.BlockSpec((tm, D), lambda j, i, o: (first_pass(j, i), 0))],
            scratch_shapes=[pltpu.VMEM((S, D), BF16)]),
        out_shape=[jax.ShapeDtypeStruct((S, IN_W), F32), jax.ShapeDtypeStruct((S, D), BF16)],
        compiler_params=_params(has_side_effects=pltpu.SideEffectType.DATAFLOW_SIDE_EFFECTING),
    )(order, x, g_pre, scale, shift, zone, sent, recv_sems)


def _decay_tables():
    log_g = jnp.log1p(-jnp.exp2(-5.0 - jnp.arange(HEADS, dtype=F32)))
    idx = jnp.arange(CHUNK, dtype=F32)
    diff = idx[:, None] - idx[None, :]
    causal = diff >= 0
    mask = jnp.where(causal, jnp.exp(log_g[:, None, None] * jnp.where(causal, diff, 0.0)), 0.0)
    xi = jnp.exp(log_g[:, None] * (idx + 1.0))[:, :, None]
    zeta = jnp.exp(log_g[:, None] * (CHUNK - 1.0 - idx))[:, :, None]
    g_chunk = jnp.broadcast_to(jnp.exp(log_g * CHUNK)[:, None, None], (HEADS, 1, 128))
    return mask, xi, zeta, g_chunk


def _rot(t, cos, sin):
    t1, t2 = t[:, :DK // 2], t[:, DK // 2:]
    return jnp.concatenate([t1 * cos - t2 * sin, t1 * sin + t2 * cos], axis=-1)


def _rot_back(t, cos, sin):
    t1, t2 = t[:, :DK // 2], t[:, DK // 2:]
    return jnp.concatenate([t1 * cos + t2 * sin, t2 * cos - t1 * sin], axis=-1)


def _silu(z):
    return z * _sigmoid(z)


def _retention_fwd(proj, cos, sin, tables, cpb):
    S = proj.shape[0]
    tb = cpb * CHUNK
    mask, xi, zeta, g_chunk = tables

    def body(q_ref, k_ref, v_ref, z_ref, cos_ref, sin_ref, mask_ref, xi_ref, zeta_ref, gc_ref,
             pa_ref, o_ref, st_ref, state):
        @pl.when(pl.program_id(1) == 0)
        def _():
            state[...] = jnp.zeros_like(state)

        m, xi_c, zeta_c, gc = mask_ref[...], xi_ref[...], zeta_ref[...], gc_ref[0:1, 0:1]
        for ci in range(cpb):
            rows = pl.ds(ci * CHUNK, CHUNK)
            cs, sn = cos_ref[rows, :], sin_ref[rows, :]
            qr = _rot(q_ref[rows, :], cs, sn)
            kr = _rot(k_ref[rows, :], cs, sn) * (DK ** -0.5)
            vb = v_ref[rows, :].astype(BF16)
            qb = qr.astype(BF16)
            st = state[...]
            stb = st.astype(BF16)
            st_ref[ci] = stb
            scores = (_dot_nt(qb, kr.astype(BF16)) * m).astype(BF16)
            o = _dot(scores, vb) + _dot(qb, stb) * xi_c
            state[...] = st * gc + _dot_tn((kr * zeta_c).astype(BF16), vb)
            o_ref[rows, :] = o
            mu = jnp.mean(o, axis=-1, keepdims=True)
            oc = o - mu
            var = jnp.mean(oc * oc, axis=-1, keepdims=True)
            pa_ref[rows, :] = (oc * lax.rsqrt(var + GN_EPS) * _silu(z_ref[rows, :])).astype(BF16)

    tab = lambda shape: pl.BlockSpec((None,) + shape, lambda h, n: (h, 0, 0))
    return pl.pallas_call(
        body, name="retention_fwd", grid=(HEADS, S // tb),
        in_specs=[pl.BlockSpec((tb, DK), lambda h, n: (n, h)),
                  pl.BlockSpec((tb, DK), lambda h, n: (n, HEADS + h)),
                  pl.BlockSpec((tb, DV), lambda h, n: (n, COL_V * 2 + h)),
                  pl.BlockSpec((tb, DV), lambda h, n: (n, COL_Z * 2 + h)),
                  pl.BlockSpec((tb, DK // 2), lambda h, n: (n, 0)),
                  pl.BlockSpec((tb, DK // 2), lambda h, n: (n, 0)),
                  tab((CHUNK, CHUNK)), tab((CHUNK, 1)), tab((CHUNK, 1)), tab((1, 128))],
        out_specs=[pl.BlockSpec((tb, DV), lambda h, n: (n, h)),
                   pl.BlockSpec((tb, DV), lambda h, n: (n, h)),
                   pl.BlockSpec((None, cpb, DK, DV), lambda h, n: (h, n, 0, 0))],
        out_shape=[jax.ShapeDtypeStruct((S, HEADS * DV), BF16), jax.ShapeDtypeStruct((S, HEADS * DV), F32),
                   jax.ShapeDtypeStruct((HEADS, S // CHUNK, DK, DV), BF16)],
        scratch_shapes=[pltpu.VMEM((DK, DV), F32)],
        compiler_params=_params(),
    )(proj, proj, proj, proj, cos, sin, mask, xi, zeta, g_chunk)


SUBLANES = 8
FWD_ROWS = 16
EW_ROWS = 16


def _shifted_copies(src, dst):
    rows = dst.shape[1]
    src[pl.ds(rows, SUBLANES), :] = jnp.zeros((SUBLANES, src.shape[1]), src.dtype)
    for s in range(SUBLANES):
        dst[s] = src[pl.ds(s, rows), :]


def _layernorm_stats(a):
    mu = jnp.mean(a, axis=-1, keepdims=True)
    ac = a - mu
    var = jnp.mean(ac * ac, axis=-1, keepdims=True)
    rstd = lax.rsqrt(var + LN_EPS)
    return ac * rstd, rstd


def _conv_fwd(proj, conv_w, conv_b, ln_g, ln_b, tm):
    S = proj.shape[0]
    hb = tm // HALO

    def body(uv_ref, ug_ref, uvh_ref, ugh_ref, z_ref, w_ref, b_ref, g_ref, bb_ref, pb_ref, a1_ref, ext, ex):
        halo = _round_bf16(uvh_ref[...] * _sigmoid(ugh_ref[...]))
        ext[pl.ds(0, HALO), :] = jnp.where(pl.program_id(0) == 0, 0.0, halo)
        ext[pl.ds(HALO, tm), :] = _round_bf16(uv_ref[...] * _sigmoid(ug_ref[...]))
        _shifted_copies(ext, ex)
        bias, g, bb = b_ref[...], g_ref[...], bb_ref[...]

        def rows_body(rb, carry):
            r0 = pl.multiple_of(rb * FWD_ROWS, FWD_ROWS)
            groups = range(FWD_ROWS // SUBLANES)
            acc = [jnp.zeros((SUBLANES, D), F32) for _ in groups]
            for j in range(CONV_W):
                a, s = divmod(HALO - (CONV_W - 1) + j, SUBLANES)
                wj = w_ref[j]
                for k in groups:
                    acc[k] = acc[k] + wj * ex[s, pl.ds(r0 + (a + k) * SUBLANES, SUBLANES), :]
            for k in groups:
                a1_ref[pl.ds(r0 + k * SUBLANES, SUBLANES), :] = acc[k] + bias
            return carry

        lax.fori_loop(0, tm // FWD_ROWS, rows_body, 0)
        xhat, _ = _layernorm_stats(a1_ref[...])
        pb_ref[...] = (_silu(xhat * g + bb) * _silu(z_ref[...])).astype(BF16)

    vec = pl.BlockSpec((1, D), lambda i: (0, 0))
    halo = lambda col: pl.BlockSpec((HALO, D), lambda i: (jnp.maximum(i * hb - 1, 0), col))
    return pl.pallas_call(
        body, name="conv_fwd", grid=(S // tm,),
        in_specs=[pl.BlockSpec((tm, D), lambda i: (i, COL_UVAL)), pl.BlockSpec((tm, D), lambda i: (i, COL_UGATE)),
                  halo(COL_UVAL), halo(COL_UGATE), pl.BlockSpec((tm, D), lambda i: (i, COL_ZCONV)),
                  pl.BlockSpec((CONV_W + 1, SUBLANES, D), lambda i: (0, 0, 0)), vec, vec, vec],
        out_specs=[pl.BlockSpec((tm, D), lambda i: (i, 0))] * 2,
        out_shape=[jax.ShapeDtypeStruct((S, D), BF16), jax.ShapeDtypeStruct((S, D), F32)],
        scratch_shapes=[pltpu.VMEM((HALO + tm + SUBLANES, D), F32), pltpu.VMEM((SUBLANES, HALO + tm, D), F32)],
        compiler_params=_params(),
    )(proj, proj, proj, proj, proj, conv_w, conv_b, ln_g, ln_b)


def _head(p_a, p_b, proj, x, target, gate, g_post, w_ret, w_conv, w_out, tm):
    S = x.shape[0]

    def body(pa_ref, pb_ref, ga_ref, gb_ref, x_ref, t_ref, gate_ref, gp_ref, wr_ref, wc_ref, wo_ref,
             dout_ref, dg_ref, dpa_ref, dpb_ref, mb_ref, dy_ref, dya_ref, dyb_ref, small_ref):
        @pl.when(pl.program_id(0) == 0)
        def _():
            small_ref[...] = jnp.zeros_like(small_ref)

        gate, gp = gate_ref[...], gp_ref[...]
        ya = _dot(pa_ref[...], wr_ref[...])
        yb = _dot(pb_ref[...], wc_ref[...])
        sa, sb = _sigmoid(ga_ref[...]), _sigmoid(gb_ref[...])
        mb = (sa * ya + sb * yb).astype(BF16)
        mb_ref[...] = mb
        y = _dot(mb, wo_ref[...])
        r = lax.rsqrt(jnp.mean(y * y, axis=-1, keepdims=True) + RMS_EPS)
        yhat = y * r
        yn = yhat * gp
        err = (x_ref[...] + gate * yn) - t_ref[...]
        small_ref[2:3, :] += 0.5 * jnp.sum(jnp.mean(err * err, axis=-1, keepdims=True))
        dout = err * (1.0 / D)
        dout_ref[...] = dout
        small_ref[0:1, :] += jnp.sum(dout * yn, axis=0, keepdims=True)
        dyn = dout * gate
        small_ref[1:2, :] += jnp.sum(dyn * yhat, axis=0, keepdims=True)
        dyhat = dyn * gp
        dy = (r * (dyhat - yhat * jnp.mean(dyhat * yhat, axis=-1, keepdims=True))).astype(BF16)
        dy_ref[...] = dy
        dmerged = _dot_nt(dy, wo_ref[...])
        dya = dmerged * sa
        dyb = dmerged * sb
        dg_ref[:, 0:D] = (dya * ya * (1.0 - sa)).astype(BF16)
        dg_ref[:, D:2 * D] = (dyb * yb * (1.0 - sb)).astype(BF16)
        dya, dyb = dya.astype(BF16), dyb.astype(BF16)
        dya_ref[...] = dya
        dyb_ref[...] = dyb
        dpa_ref[...] = _dot_nt(dya, wr_ref[...])
        dpb_ref[...] = _dot_nt(dyb, wc_ref[...])

    vec = pl.BlockSpec((1, D), lambda i: (0, 0))
    tile = lambda w, col=0: pl.BlockSpec((tm, w), lambda i: (i, col))
    whole = lambda a: pl.BlockSpec(a.shape, lambda i: (0, 0), pipeline_mode=pl.Buffered(1))
    act = lambda w: jax.ShapeDtypeStruct((S, w), BF16)
    return pl.pallas_call(
        body, name="head", grid=(S // tm,),
        in_specs=[tile(2 * D), tile(D), tile(D, COL_GA), tile(D, COL_GB), tile(D), tile(D), vec, vec,
                  whole(w_ret), whole(w_conv), whole(w_out)],
        out_specs=[tile(D), tile(2 * D), tile(2 * D), tile(D),
                   tile(D), tile(D), tile(D), tile(D), pl.BlockSpec((8, D), lambda i: (0, 0))],
        out_shape=[jax.ShapeDtypeStruct((S, D), F32),
                   act(2 * D), jax.ShapeDtypeStruct((S, 2 * D), F32), jax.ShapeDtypeStruct((S, D), F32),
                   act(D), act(D), act(D), act(D), jax.ShapeDtypeStruct((8, D), F32)],
        compiler_params=_params(),
    )(p_a, p_b, proj, proj, x, target, gate, g_post, w_ret, w_conv, w_out)


def _matmul_tn(a, b, name, bn, tk):
    S, M = a.shape
    N = b.shape[1]

    def body(a_ref, b_ref, o_ref):
        @pl.when(pl.program_id(1) == 0)
        def _():
            o_ref[...] = jnp.zeros_like(o_ref)

        o_ref[...] += _dot_tn(a_ref[...], b_ref[...])

    return pl.pallas_call(
        body, name=name, grid=(N // bn, S // tk),
        in_specs=[pl.BlockSpec((tk, M), lambda j, k: (k, 0)), pl.BlockSpec((tk, bn), lambda j, k: (k, j))],
        out_specs=pl.BlockSpec((M, bn), lambda j, k: (0, j)),
        out_shape=jax.ShapeDtypeStruct((M, N), F32),
        compiler_params=_params(),
    )(a, b)


SMALL_B_ROWS = 40


def _dsilu(z, s):
    return s * (1.0 + z * (1.0 - s))


def _conv_bwd(dp_b, proj, a1, conv_w, ln_g, ln_b, tm):
    S = proj.shape[0]
    nt = S // tm

    def body(dpb_ref, uv_ref, ug_ref, z_ref, a1_ref, w_ref, g_ref, bb_ref,
             dc_ref, small_ref, dext, dx, carry, dw_acc):
        @pl.when(pl.program_id(0) == 0)
        def _():
            small_ref[...] = jnp.zeros_like(small_ref)
            carry[...] = jnp.zeros_like(carry)

        xhat, rstd = _layernorm_stats(a1_ref[...])
        g = g_ref[...]
        a2 = xhat * g + bb_ref[...]
        s2 = _sigmoid(a2)
        zc = z_ref[...]
        sz = _sigmoid(zc)
        dpb = dpb_ref[...]
        dc_ref[:, 2 * D:3 * D] = (dpb * (a2 * s2) * _dsilu(zc, sz)).astype(BF16)
        da2 = dpb * (zc * sz) * _dsilu(a2, s2)
        small_ref[32:33, :] += jnp.sum(da2 * xhat, axis=0, keepdims=True)
        small_ref[33:34, :] += jnp.sum(da2, axis=0, keepdims=True)
        dxhat = da2 * g
        da1 = rstd * (dxhat - jnp.mean(dxhat, axis=-1, keepdims=True)
                      - xhat * jnp.mean(dxhat * xhat, axis=-1, keepdims=True))
        small_ref[31:32, :] += jnp.sum(da1, axis=0, keepdims=True)
        da1 = _round_bf16(da1)
        dext[pl.ds(0, tm), :] = da1
        dext[pl.ds(tm, HALO), :] = carry[...]
        carry[...] = da1[0:HALO, :]
        _shifted_copies(dext, dx)
        dw_acc[...] = jnp.zeros_like(dw_acc)

        def conv_body(rb, c):
            r0 = pl.multiple_of(rb * EW_ROWS, EW_ROWS)
            rows = pl.ds(r0, EW_ROWS)
            uv = uv_ref[rows, :]
            sg = _sigmoid(ug_ref[rows, :])
            a0 = _round_bf16(uv * sg)
            halves = []
            for k in range(EW_ROWS // SUBLANES):
                a0_k = a0[k * SUBLANES:(k + 1) * SUBLANES, :]
                acc = jnp.zeros((SUBLANES, D), F32)
                for j in range(CONV_W):
                    a, s = divmod(CONV_W - 1 - j, SUBLANES)
                    win = dx[s, pl.ds(r0 + (a + k) * SUBLANES, SUBLANES), :]
                    acc = acc + w_ref[j] * win
                    dw_acc[j] += a0_k * win
                halves.append(acc)
            da0 = jnp.concatenate(halves, axis=0)
            dc_ref[rows, 0:D] = (da0 * sg).astype(BF16)
            dc_ref[rows, D:2 * D] = (da0 * uv * sg * (1.0 - sg)).astype(BF16)
            return c

        lax.fori_loop(0, tm // EW_ROWS, conv_body, 0)
        small_ref[0:CONV_W + 1, :] += jnp.sum(dw_acc[...], axis=1)

    vec = pl.BlockSpec((1, D), lambda i: (0, 0))
    tile = lambda col: pl.BlockSpec((tm, D), lambda i: (nt - 1 - i, col))
    return pl.pallas_call(
        body, name="conv_bwd", grid=(nt,),
        in_specs=[tile(0), tile(COL_UVAL), tile(COL_UGATE), tile(COL_ZCONV), tile(0),
                  pl.BlockSpec((CONV_W + 1, SUBLANES, D), lambda i: (0, 0, 0)), vec, vec],
        out_specs=[pl.BlockSpec((tm, 3 * D), lambda i: (nt - 1 - i, 0)),
                   pl.BlockSpec((SMALL_B_ROWS, D), lambda i: (0, 0))],
        out_shape=[jax.ShapeDtypeStruct((S, 3 * D), BF16), jax.ShapeDtypeStruct((SMALL_B_ROWS, D), F32)],
        scratch_shapes=[pltpu.VMEM((tm + HALO + SUBLANES, D), F32), pltpu.VMEM((SUBLANES, tm + HALO, D), F32),
                        pltpu.VMEM((HALO, D), F32), pltpu.VMEM((CONV_W + 1, SUBLANES, D), F32)],
        compiler_params=_params(),
    )(dp_b, proj, proj, proj, a1, conv_w, ln_g, ln_b)


RET_W = 2 * DK + 2 * DV


def _retention_bwd(dp_a, o, proj, states, cos, sin, tables, cpb):
    S = proj.shape[0]
    tb = cpb * CHUNK
    nb = S // tb
    mask, xi, zeta, g_chunk = tables

    def body(dpa_ref, o_ref, q_ref, k_ref, v_ref, z_ref, st_ref, cos_ref, sin_ref, mask_ref, xi_ref, zeta_ref,
             gc_ref, out_ref, dstate):
        @pl.when(pl.program_id(1) == 0)
        def _():
            dstate[...] = jnp.zeros_like(dstate)

        m, xi_c, zeta_c, gc = mask_ref[...], xi_ref[...], zeta_ref[...], gc_ref[0:1, 0:1]
        for ci in reversed(range(cpb)):
            rows = pl.ds(ci * CHUNK, CHUNK)
            o = o_ref[rows, :]
            oc = o - jnp.mean(o, axis=-1, keepdims=True)
            rstd = lax.rsqrt(jnp.mean(oc * oc, axis=-1, keepdims=True) + GN_EPS)
            rhat = oc * rstd
            z = z_ref[rows, :]
            sz = _sigmoid(z)
            dpa = dpa_ref[rows, :]
            out_ref[rows, 2 * DK + DV:RET_W] = (dpa * rhat * _dsilu(z, sz)).astype(BF16)
            dret = dpa * (z * sz)
            d_o = rstd * (dret - jnp.mean(dret, axis=-1, keepdims=True)
                          - rhat * jnp.mean(dret * rhat, axis=-1, keepdims=True))
            cs, sn = cos_ref[rows, :], sin_ref[rows, :]
            qb = _rot(q_ref[rows, :], cs, sn).astype(BF16)
            kr = _rot(k_ref[rows, :], cs, sn) * (DK ** -0.5)
            kb = kr.astype(BF16)
            kz = (kr * zeta_c).astype(BF16)
            vb = v_ref[rows, :].astype(BF16)
            scores = (_dot_nt(qb, kb) * m).astype(BF16)
            dob = d_o.astype(BF16)
            dox = (d_o * xi_c).astype(BF16)
            dscores = (_dot_nt(dob, vb) * m).astype(BF16)
            dst = dstate[...]
            dstb = dst.astype(BF16)
            dqr = _dot(dscores, kb) + _dot_nt(dox, st_ref[ci])
            dkr = _dot_tn(dscores, qb) + _dot_nt(vb, dstb) * zeta_c
            dv = _dot_tn(scores, dob) + _dot(kz, dstb)
            dstate[...] = dst * gc + _dot_tn(qb, dox)
            out_ref[rows, 0:DK] = _rot_back(dqr, cs, sn).astype(BF16)
            out_ref[rows, DK:2 * DK] = (_rot_back(dkr, cs, sn) * (DK ** -0.5)).astype(BF16)
            out_ref[rows, 2 * DK:2 * DK + DV] = dv.astype(BF16)

    tab = lambda shape: pl.BlockSpec((None,) + shape, lambda h, n: (h, 0, 0))
    blk = lambda w, col0: pl.BlockSpec((tb, w), lambda h, n: (nb - 1 - n, col0 + h))
    return pl.pallas_call(
        body, name="retention_bwd", grid=(HEADS, nb),
        in_specs=[blk(DV, 0), blk(DV, 0), blk(DK, 0), blk(DK, HEADS), blk(DV, COL_V * 2), blk(DV, COL_Z * 2),
                  pl.BlockSpec((None, cpb, DK, DV), lambda h, n: (h, nb - 1 - n, 0, 0)),
                  pl.BlockSpec((tb, DK // 2), lambda h, n: (nb - 1 - n, 0)),
                  pl.BlockSpec((tb, DK // 2), lambda h, n: (nb - 1 - n, 0)),
                  tab((CHUNK, CHUNK)), tab((CHUNK, 1)), tab((CHUNK, 1)), tab((1, 128))],
        out_specs=blk(RET_W, 0),
        out_shape=jax.ShapeDtypeStruct((S, HEADS * RET_W), BF16),
        scratch_shapes=[pltpu.VMEM((DK, DV), F32)],
        compiler_params=_params(),
    )(dp_a, o, proj, proj, proj, proj, states, cos, sin, mask, xi, zeta, g_chunk)


def _to_head_major(w):
    parts = []
    for h in range(HEADS):
        parts += [w[:, h * DK:(h + 1) * DK], w[:, D + h * DK:D + (h + 1) * DK],
                  w[:, 2 * D + h * DV:2 * D + (h + 1) * DV], w[:, 4 * D + h * DV:4 * D + (h + 1) * DV]]
    return jnp.concatenate(parts, axis=1)


def _from_head_major(w):
    starts = (0, DK, 2 * DK, 2 * DK + DV)
    widths = (DK, DK, DV, DV)
    parts = [w[:, h * RET_W + s:h * RET_W + s + n] for s, n in zip(starts, widths) for h in range(HEADS)]
    return jnp.concatenate(parts, axis=1)


def _dh(d_r, d_c, d_g, w_r, w_c, w_g, x, dout, g_pre, scale, tm):
    S = x.shape[0]

    def body(dr_ref, dc_ref, dg_ref, wr_ref, wc_ref, wg_ref, x_ref, dout_ref, g_ref, sc_ref, gx_ref, small_ref):
        @pl.when(pl.program_id(0) == 0)
        def _():
            small_ref[...] = jnp.zeros_like(small_ref)

        dh = _dot_nt(dr_ref[...], wr_ref[...]) + _dot_nt(dc_ref[...], wc_ref[...]) + _dot_nt(dg_ref[...], wg_ref[...])
        xf = x_ref[...]
        r = lax.rsqrt(jnp.mean(xf * xf, axis=-1, keepdims=True) + RMS_EPS)
        xhat = xf * r
        g, sc1 = g_ref[...], 1.0 + sc_ref[...]
        small_ref[0:1, :] += jnp.sum(dh, axis=0, keepdims=True)
        small_ref[1:2, :] += jnp.sum(dh * (xhat * g), axis=0, keepdims=True)
        small_ref[2:3, :] += jnp.sum(dh * sc1 * xhat, axis=0, keepdims=True)
        dxhat = dh * sc1 * g
        gx_ref[...] = dout_ref[...] + r * (dxhat - xhat * jnp.mean(dxhat * xhat, axis=-1, keepdims=True))

    vec = pl.BlockSpec((1, D), lambda i: (0, 0))
    tile = lambda w: pl.BlockSpec((tm, w), lambda i: (i, 0))
    whole = lambda a: pl.BlockSpec(a.shape, lambda i: (0, 0), pipeline_mode=pl.Buffered(1))
    return pl.pallas_call(
        body, name="dh", grid=(S // tm,),
        in_specs=[tile(d_r.shape[1]), tile(d_c.shape[1]), tile(d_g.shape[1]), whole(w_r), whole(w_c), whole(w_g),
                  tile(D), tile(D), vec, vec],
        out_specs=[tile(D), pl.BlockSpec((8, D), lambda i: (0, 0))],
        out_shape=[jax.ShapeDtypeStruct((S, D), F32), jax.ShapeDtypeStruct((8, D), F32)],
        compiler_params=_params(),
    )(d_r, d_c, d_g, w_r, w_c, w_g, x, dout, g_pre, scale)


ROW_GATE, ROW_POST, ROW_LOSS = 0, 1, 2
ROW_CONV_W, ROW_CONV_B, ROW_LN_G, ROW_LN_B = 8, 8 + 31, 8 + 32, 8 + 33
ROW_SHIFT, ROW_SCALE, ROW_PRE = 48, 49, 50


def kernel(x, c, positions, w_ada, b_ada, pre_norm_g, w_in, conv_w, conv_b, conv_ln_g, conv_ln_b, w_ret_out, w_conv_out, w_out, post_norm_g, loss_target, m_w_ada, m_b_ada, m_pre_norm_g, m_w_in, m_conv_w, m_conv_b, m_conv_ln_g, m_conv_ln_b, m_w_ret_out, m_w_conv_out, m_w_out, m_post_norm_g, v_w_ada, v_b_ada, v_pre_norm_g, v_w_in, v_conv_w, v_conv_b, v_conv_ln_g, v_conv_ln_b, v_w_ret_out, v_w_conv_out, v_w_out, v_post_norm_g):
    S = x.shape[1]
    me = 4 * lax.axis_index("x") + 2 * lax.axis_index("y") + lax.axis_index("c")
    x2, target = x[0], loss_target[0]
    ada_w = w_ada.shape[2]
    cw_w = conv_w.shape[2]

    c_g, conv_w_g = _exchange([jnp.pad(c, ((0, 7), (0, 0))), jnp.pad(conv_w[0], ((0, 1), (0, 0)))],
                              False, "gather_c_conv_w", True)
    c_all = c_g[:, 0, :]
    conv_w_full = _round_bf16(conv_w_g.transpose(1, 0, 2).reshape(CONV_W + 1, D))
    conv_w_full = jnp.broadcast_to(conv_w_full[:, None, :], (CONV_W + 1, SUBLANES, D))
    b_part = lax.dynamic_slice(b_ada, (0, me * ada_w), (1, ada_w))
    mod_g = _exchange([_mod_part(c_all, w_ada[0], b_part)], False, "gather_mod", True)[0]

    own_in, mod_g = lax.optimization_barrier((w_in[0].astype(BF16), mod_g))
    w_send, w_recv, w_sent, w_zones, tok_w = _exchange_start(
        [own_in, w_ret_out[0].astype(BF16), w_conv_out[0].astype(BF16), w_out[0].astype(BF16)], False, "gather_weights_start")
    mod = lax.dynamic_index_in_dim(mod_g, me, axis=1, keepdims=False).reshape(1, 3 * D)
    shift, scale, gate = mod[:, :D], mod[:, D:2 * D], mod[:, 2 * D:]
    positions = positions + tok_w[0:1, 0:1].astype(jnp.int32)

    cos, sin = _rope_tables(positions, S)
    tables = _decay_tables()
    proj, h = _inproj_fwd(x2, pre_norm_g, scale, shift, w_zones[0], w_recv, w_sent[0], tm=min(S, 1024))
    g_in, proj = lax.optimization_barrier((w_zones[0], proj))
    w_in_full = g_in.transpose(1, 0, 2).reshape(D, IN_W)
    p_a, o, states = _retention_fwd(proj, cos, sin, tables, cpb=8)
    p_b, a1 = _conv_fwd(proj, conv_w_full, conv_b, conv_ln_g, conv_ln_b, tm=256)
    g_ret, g_conv, g_out = _exchange_wait(w_send, w_recv, w_sent, w_zones, p_b, False, "gather_weights_wait", n_waited=1)
    w_ret_full, w_conv_full, w_out_full = g_ret.reshape(2 * D, D), g_conv.reshape(D, D), g_out.reshape(D, D)

    dout, d_g, dp_a, dp_b, merged, dy, dy_a, dy_b, small_a = _head(
        p_a, p_b, proj, x2, target, gate, post_norm_g, w_ret_full, w_conv_full, w_out_full, tm=256)
    parts_rest = [_matmul_tn(p_a, dy_a, "dw_ret_out", bn=D, tk=min(S, 2048)).astype(BF16).reshape(N_DEV, 2 * D // N_DEV, D),
                  _matmul_tn(p_b, dy_b, "dw_conv_out", bn=D, tk=min(S, 2048)).astype(BF16).reshape(N_DEV, D // N_DEV, D),
                  _matmul_tn(merged, dy, "dw_out", bn=D, tk=min(S, 2048)).astype(BF16).reshape(N_DEV, D // N_DEV, D)]
    *s_rest_handles, tok_s_rest = _exchange_start(parts_rest, True, "scatter_rest_start")
    d_c, small_b = _conv_bwd(dp_b, proj, a1, conv_w_full, conv_ln_g + tok_s_rest[0:1, 0:1], conv_ln_b, tm=256)
    d_r = _retention_bwd(dp_a, o, proj, states, cos, sin, tables, cpb=8)
    dw_r = _matmul_tn(h, d_r, "dw_in_ret", bn=RET_W, tk=min(S, 2048))
    dw_c = _matmul_tn(h, d_c, "dw_in_conv", bn=D, tk=min(S, 2048))
    dw_g = _matmul_tn(h, d_g, "dw_in_gate", bn=D, tk=min(S, 2048))
    dw_in = jnp.concatenate([_from_head_major(dw_r), dw_c, dw_g], axis=1)
    parts_in = dw_in.astype(BF16).reshape(D, N_DEV, W_SHARD).transpose(1, 0, 2)
    *s_in_handles, tok_s_in = _exchange_start([parts_in], True, "scatter_w_in_start")
    grad_x, small_c = _dh(d_r, d_c, d_g, _to_head_major(w_in_full), w_in_full[:, 6 * D:9 * D], w_in_full[:, 9 * D:],
                          x2, dout, pre_norm_g + tok_s_in[0:1, 0:1], scale, tm=256)

    r_ret, r_conv, r_out = _exchange_wait(*s_rest_handles, small_c, True, "scatter_rest_wait")
    (r_in,) = _exchange_wait(*s_in_handles, small_c, True, "scatter_w_in_wait")
    big = {"w_in": _adam_shard(r_in, w_in[0], m_w_in[0], v_w_in[0], "adam_w_in", 128),
           "w_ret_out": _adam_shard(r_ret, w_ret_out[0], m_w_ret_out[0], v_w_ret_out[0], "adam_w_ret_out", 128),
           "w_conv_out": _adam_shard(r_conv, w_conv_out[0], m_w_conv_out[0], v_w_conv_out[0], "adam_w_conv_out", 128),
           "w_out": _adam_shard(r_out, w_out[0], m_w_out[0], v_w_out[0], "adam_w_out", 128)}

    sm_g = _exchange([jnp.concatenate([small_a, small_b, small_c], axis=0)], False, "gather_small", True)[0]
    sm = _sum_devices(sm_g)
    row = lambda r: sm[r:r + 1]
    g8 = jnp.concatenate([row(ROW_SHIFT), row(ROW_SCALE), row(ROW_GATE), row(ROW_PRE), row(ROW_CONV_B),
                          row(ROW_LN_G), row(ROW_LN_B), row(ROW_POST)], axis=0)
    stack8 = lambda ba, pre, cb, lg, lb, post: jnp.concatenate([ba.reshape(3, D), pre, cb, lg, lb, post], axis=0)
    d8, m8, v8 = _adam_small(
        g8, stack8(b_ada, pre_norm_g, conv_b, conv_ln_g, conv_ln_b, post_norm_g),
        stack8(m_b_ada, m_pre_norm_g, m_conv_b, m_conv_ln_g, m_conv_ln_b, m_post_norm_g),
        stack8(v_b_ada, v_pre_norm_g, v_conv_b, v_conv_ln_g, v_conv_ln_b, v_post_norm_g), "adam_small")
    unstack8 = lambda a: {"b_ada": a[0:3].reshape(3 * D), "pre_norm_g": a[3], "conv_b": a[4],
                          "conv_ln_g": a[5], "conv_ln_b": a[6], "post_norm_g": a[7]}
    g_small, d_small, m_small, v_small = unstack8(g8), unstack8(d8), unstack8(m8), unstack8(v8)

    pad_row = lambda a: jnp.pad(a[0], ((0, 1), (0, 0)))
    g_cw = lax.dynamic_slice(sm[ROW_CONV_W:ROW_CONV_W + CONV_W + 1], (0, me * cw_w), (CONV_W + 1, cw_w))
    d_cw, m_cw, v_cw = _adam_small(g_cw, pad_row(conv_w), pad_row(m_conv_w), pad_row(v_conv_w), "adam_conv_w")

    dmod_all = jnp.concatenate([sm_g[:, ROW_SHIFT], sm_g[:, ROW_SCALE], sm_g[:, ROW_GATE]], axis=1)
    g_wa = _grad_w_ada(c_all.T, lax.dynamic_slice(dmod_all, (0, me * ada_w), (N_DEV, ada_w)))
    d_wa, m_wa, v_wa = _adam_small(g_wa, w_ada[0], m_w_ada[0], v_w_ada[0], "adam_w_ada")

    grads = {"w_ada": g_wa, "conv_w": g_cw[:CONV_W], **g_small, **{n: t[0] for n, t in big.items()}}
    deltas = {"w_ada": d_wa, "conv_w": d_cw[:CONV_W], **d_small, **{n: t[1] for n, t in big.items()}}
    new_m = {"w_ada": m_wa, "conv_w": m_cw[:CONV_W], **m_small, **{n: t[2] for n, t in big.items()}}
    new_v = {"w_ada": v_wa, "conv_w": v_cw[:CONV_W], **v_small, **{n: t[3] for n, t in big.items()}}
    order = ["w_ada", "b_ada", "pre_norm_g", "w_in", "conv_w", "conv_b", "conv_ln_g", "conv_ln_b", "w_ret_out",
             "w_conv_out", "w_out", "post_norm_g"]
    loss = sm[ROW_LOSS, 0]
    out = [loss, grad_x[None]]
    for group in (grads, deltas, new_m, new_v):
        out += [group[n][None] for n in order]
    return tuple(out)
```

```python
import jax
import jax.numpy as jnp
import numpy as np
from jax import lax
from jax.experimental import pallas as pl
from jax.experimental.pallas import tpu as pltpu

F32 = jnp.float32
BF16 = jnp.bfloat16
MESH = pl.DeviceIdType.MESH

N_DEV = 8
D = 1024
HEADS = 4
DK = 256
DV = 512
CHUNK = 128
CONV_W = 31
HALO = 32
IN_W = 11264
W_SHARD = IN_W // N_DEV
ROPE_BASE = 10000.0
GN_EPS = 1e-5
LN_EPS = 1e-5
RMS_EPS = 1e-6
ADAM_LR, ADAM_B1, ADAM_B2, ADAM_EPS, ADAM_WD, ADAM_STEP = 0.001, 0.9, 0.999, 1e-08, 0.01, 10

COL_V, COL_Z, COL_UVAL, COL_UGATE, COL_ZCONV, COL_GA, COL_GB = 2, 4, 6, 7, 8, 9, 10

VMEM_LIMIT = 56 * 1024 * 1024


def _params(**kw):
    return pltpu.CompilerParams(vmem_limit_bytes=VMEM_LIMIT, **kw)


def _sigmoid(z):
    return jax.nn.sigmoid(z)


def _round_bf16(a):
    return a.astype(BF16).astype(F32)


def _dot(a, b):
    return jnp.dot(a, b, preferred_element_type=F32)


def _dot_nt(a, b):
    return lax.dot_general(a, b, (((1,), (1,)), ((), ())), preferred_element_type=F32)


def _dot_tn(a, b):
    return lax.dot_general(a, b, (((0,), (0,)), ((), ())), preferred_element_type=F32)


def _my_place():
    return lax.axis_index("x"), lax.axis_index("y"), lax.axis_index("c")


def _peer(k):
    x, y, c = _my_place()
    px = lax.rem(x + ((k >> 2) & 1), 2)
    py = lax.rem(y + ((k >> 1) & 1), 2)
    pc = lax.rem(c + (k & 1), 2)
    return (px, py, pc), 4 * px + 2 * py + pc


def _exchange(arrs, scatter, name, in_vmem):
    n = len(arrs)

    def body(*refs):
        ins, outs = refs[:n], refs[n:2 * n]
        send_sems, recv_sems, local_sems = refs[2 * n:]
        x, y, c = _my_place()
        me = 4 * x + 2 * y + c
        copies = []
        for i in range(n):
            src = ins[i].at[me] if scatter else ins[i]
            cp = pltpu.make_async_copy(src, outs[i].at[me], local_sems.at[i])
            cp.start()
            copies.append(cp)
        for k in range(1, N_DEV):
            peer, peer_idx = _peer(k)
            for i in range(n):
                src = ins[i].at[peer_idx] if scatter else ins[i]
                cp = pltpu.make_async_remote_copy(
                    src_ref=src, dst_ref=outs[i].at[me], send_sem=send_sems.at[i, k - 1],
                    recv_sem=recv_sems.at[i, k - 1], device_id=peer, device_id_type=MESH)
                cp.start()
                copies.append(cp)
        for cp in copies:
            cp.wait()

    space = pltpu.VMEM if in_vmem else pl.ANY
    out_shape = [jax.ShapeDtypeStruct(a.shape if scatter else (N_DEV,) + a.shape, a.dtype) for a in arrs]
    return pl.pallas_call(
        body, name=name, out_shape=out_shape,
        in_specs=[pl.BlockSpec(memory_space=space)] * n,
        out_specs=[pl.BlockSpec(memory_space=space)] * n,
        scratch_shapes=[pltpu.SemaphoreType.DMA((n, N_DEV - 1)), pltpu.SemaphoreType.DMA((n, N_DEV - 1)),
                        pltpu.SemaphoreType.DMA((n,))],
        compiler_params=_params(has_side_effects=True),
    )(*arrs)


def _remote_copies(ins, lands, send_sems, recv_sems, scatter, first=0):
    x, y, c = _my_place()
    me = 4 * x + 2 * y + c
    copies = []
    for i in range(len(ins)):
        for k in range(1, N_DEV):
            peer, peer_idx = _peer(k)
            sem = (N_DEV - 1) * (first + i) + k - 1
            copies.append(pltpu.make_async_remote_copy(
                src_ref=ins[i].at[peer_idx] if scatter else ins[i], dst_ref=lands[i].at[me],
                send_sem=send_sems.at[sem], recv_sem=recv_sems.at[sem], device_id=peer, device_id_type=MESH))
    return copies


_HBM = pl.BlockSpec(memory_space=pltpu.HBM)
_SEM = pl.BlockSpec(memory_space=pltpu.SEMAPHORE)


def _exchange_start(arrs, scatter, name):
    n = len(arrs)
    me = 4 * lax.axis_index("x") + 2 * lax.axis_index("y") + lax.axis_index("c")
    lands = []
    for a in arrs:
        own = lax.dynamic_index_in_dim(a, me, axis=0) if scatter else a[None]
        zone = lax.empty(a.shape if scatter else (N_DEV,) + a.shape, a.dtype)
        lands.append(lax.dynamic_update_slice(zone, own, (me,) + (0,) * (own.ndim - 1)))

    def body(*refs):
        ins, land_refs, send_sems, recv_sems, token = refs[:n], refs[n:2 * n], refs[2 * n], refs[2 * n + 1], refs[-1]
        for cp in _remote_copies(ins, land_refs, send_sems, recv_sems, scatter):
            cp.start()
        token[...] = jnp.zeros_like(token)

    n_sem = (N_DEV - 1) * n
    hbm = lambda a: pltpu.HBM(a.shape, a.dtype)
    outs = pl.pallas_call(
        body, name=name,
        out_shape=(pltpu.SemaphoreType.DMA((n_sem,)), pltpu.SemaphoreType.DMA((n_sem,)), *[hbm(a) for a in arrs],
                   *[hbm(a) for a in lands], jax.ShapeDtypeStruct((8, 128), F32)),
        in_specs=(_HBM,) * (2 * n), out_specs=(_SEM, _SEM) + (_HBM,) * (2 * n) + (pl.BlockSpec(memory_space=pltpu.VMEM),),
        input_output_aliases={i: 2 + i for i in range(2 * n)},
        compiler_params=pltpu.CompilerParams(has_side_effects=pltpu.SideEffectType.DATAFLOW_SIDE_EFFECTING),
    )(*[pltpu.with_memory_space_constraint(a, pltpu.HBM) for a in list(arrs) + lands])
    return outs[0], outs[1], list(outs[2:2 + n]), list(outs[2 + n:2 + 2 * n]), outs[-1]


def _exchange_wait(send_sems, recv_sems, sent, zones, after, scatter, name, n_waited=0):
    n = len(sent)
    m = n - n_waited

    def body(*refs):
        early, ins, land_refs = refs[:n_waited], refs[n_waited:n], refs[n:n + m]
        s_sems, r_sems, token = refs[n + m], refs[n + m + 1], refs[-1]
        token[...] = jnp.zeros_like(token)
        for i in range(n_waited):
            for k in range(1, N_DEV):
                pltpu.make_async_remote_copy(
                    src_ref=early[i], dst_ref=early[i], send_sem=s_sems.at[(N_DEV - 1) * i + k - 1],
                    recv_sem=s_sems.at[(N_DEV - 1) * i + k - 1], device_id=_peer(k)[0], device_id_type=MESH).wait_send()
        for cp in _remote_copies(ins, land_refs, s_sems, r_sems, scatter, first=n_waited):
            cp.wait_send()
            cp.wait_recv()

    thru = list(sent[n_waited:]) + list(zones[n_waited:])
    outs = pl.pallas_call(
        body, name=name,
        out_shape=tuple(pltpu.HBM(a.shape, a.dtype) for a in thru) + (jax.ShapeDtypeStruct((8, 128), F32),),
        in_specs=(_HBM,) * (n_waited + 2 * m) + (_SEM, _SEM, pl.BlockSpec(memory_space=pl.ANY)),
        out_specs=(_HBM,) * (2 * m) + (pl.BlockSpec(memory_space=pltpu.VMEM),),
        input_output_aliases={n_waited + i: i for i in range(2 * m)},
        compiler_params=pltpu.CompilerParams(has_side_effects=pltpu.SideEffectType.DATAFLOW_SIDE_EFFECTING),
    )(*sent[:n_waited], *thru, send_sems, recv_sems, after)
    return list(outs[m:2 * m]), outs[-1]


def _rope_tables(positions, S):
    tm = min(S, 1024)

    half = DK // 2
    inv_freq = (ROPE_BASE ** (-jnp.arange(half, dtype=F32) / half)).reshape(1, half)

    def body(pos_ref, f_ref, cos_ref, sin_ref):
        ang = pos_ref[...].astype(F32) * f_ref[...]
        cos_ref[...] = jnp.cos(ang)
        sin_ref[...] = jnp.sin(ang)

    return pl.pallas_call(
        body, name="rope_tables", grid=(S // tm,),
        in_specs=[pl.BlockSpec((tm, 1), lambda i: (i, 0)), pl.BlockSpec((1, half), lambda i: (0, 0))],
        out_specs=[pl.BlockSpec((tm, half), lambda i: (i, 0))] * 2,
        out_shape=[jax.ShapeDtypeStruct((S, half), F32)] * 2,
    )(positions.reshape(S, 1), inv_freq)


def _mod_part(c_all, w_ada, b_part):
    def body(c_ref, w_ref, b_ref, o_ref):
        o_ref[...] = _dot(c_ref[...].astype(BF16), w_ref[...].astype(BF16)) + b_ref[...]

    return pl.pallas_call(body, name="mod_part",
                          out_shape=jax.ShapeDtypeStruct((N_DEV, w_ada.shape[1]), F32))(c_all, w_ada, b_part)


def _sum_devices(g):
    def body(g_ref, o_ref):
        acc = g_ref[0]
        for d in range(1, N_DEV):
            acc = acc + g_ref[d]
        o_ref[...] = acc

    return pl.pallas_call(body, name="sum_devices", out_shape=jax.ShapeDtypeStruct(g.shape[1:], F32))(g)


def _grad_w_ada(c_t, dmod):
    def body(c_ref, d_ref, o_ref):
        acc = c_ref[:, 0:1] * d_ref[0:1, :]
        for b in range(1, N_DEV):
            acc = acc + c_ref[:, b:b + 1] * d_ref[b:b + 1, :]
        o_ref[...] = acc

    return pl.pallas_call(body, name="grad_w_ada",
                          out_shape=jax.ShapeDtypeStruct((c_t.shape[0], dmod.shape[1]), F32))(c_t, dmod)


def _adam_math(w, g, m, v):
    m = ADAM_B1 * m + (1.0 - ADAM_B1) * g
    v = ADAM_B2 * v + (1.0 - ADAM_B2) * (g * g)
    m_hat = m / (1.0 - ADAM_B1 ** ADAM_STEP)
    v_hat = v / (1.0 - ADAM_B2 ** ADAM_STEP)
    delta = -ADAM_LR * (m_hat / (jnp.sqrt(v_hat) + ADAM_EPS) + ADAM_WD * w)
    return delta, m, v


def _adam_small(g, w, m, v, name):
    def body(g_ref, w_ref, m_ref, v_ref, d_ref, nm_ref, nv_ref):
        d_ref[...], nm_ref[...], nv_ref[...] = _adam_math(w_ref[...], g_ref[...], m_ref[...], v_ref[...])

    return pl.pallas_call(body, name=name, out_shape=[jax.ShapeDtypeStruct(w.shape, F32)] * 3)(g, w, m, v)


def _adam_shard(parts, w, m, v, name, tr):
    R, L = w.shape

    def body(p_ref, w_ref, m_ref, v_ref, g_ref, d_ref, nm_ref, nv_ref):
        g = p_ref[0].astype(F32)
        for d in range(1, N_DEV):
            g = g + p_ref[d].astype(F32)
        g_ref[...] = g
        d_ref[...], nm_ref[...], nv_ref[...] = _adam_math(w_ref[...], g, m_ref[...], v_ref[...])

    blk = pl.BlockSpec((tr, L), lambda i: (i, 0))
    return pl.pallas_call(
        body, name=name, grid=(R // tr,),
        in_specs=[pl.BlockSpec((N_DEV, tr, L), lambda i: (0, i, 0)), blk, blk, blk],
        out_specs=[blk] * 4, out_shape=[jax.ShapeDtypeStruct((R, L), F32)] * 4,
        compiler_params=_params(),
    )(parts, w, m, v)


ARRIVAL = (0, 1, 2, 4, 3, 5, 6, 7)
LEAD = 2


def _inproj_fwd(x, g_pre, scale, shift, zone, recv_sems, sent, tm):
    S = x.shape[0]
    nt = S // tm
    me = 4 * lax.axis_index("x") + 2 * lax.axis_index("y") + lax.axis_index("c")
    order = jnp.bitwise_xor(me, jnp.asarray(ARRIVAL, jnp.int32))

    def body(order_ref, x_ref, g_ref, sc_ref, sh_ref, w_ref, sent_ref, sem_ref, proj_ref, h_ref, h_all):
        j, i = pl.program_id(0), pl.program_id(1)
        for row in range(1, N_DEV):
            @pl.when((j == row) & (i == 0))
            def _():
                k = ARRIVAL[row]
                pltpu.make_async_remote_copy(
                    src_ref=sent_ref, dst_ref=sent_ref, send_sem=sem_ref.at[k - 1],
                    recv_sem=sem_ref.at[k - 1], device_id=_peer(k)[0], device_id_type=MESH).wait_recv()

        rows = pl.ds(pl.multiple_of(jnp.maximum(i - LEAD, 0) * tm, tm), tm)

        @pl.when((j == 0) & (i >= LEAD))
        def _():
            xf = x_ref[...]
            r = lax.rsqrt(jnp.mean(xf * xf, axis=-1, keepdims=True) + RMS_EPS)
            h = ((xf * r * g_ref[...]) * (1.0 + sc_ref[...]) + sh_ref[...]).astype(BF16)
            h_all[rows, :] = h
            h_ref[...] = h

        @pl.when(i >= LEAD)
        def _():
            proj_ref[...] = _dot(h_all[rows, :], w_ref[...])

    tile = lambda j, i: jnp.maximum(i - LEAD, 0)
    first_pass = lambda j, i: jnp.where(j == 0, tile(j, i), nt - 1)
    vec = pl.BlockSpec((1, D), lambda j, i, o: (0, 0))
    any_space = pl.BlockSpec(memory_space=pl.ANY)
    return pl.pallas_call(
        body, name="inproj_fwd",
        grid_spec=pltpu.PrefetchScalarGridSpec(
            num_scalar_prefetch=1, grid=(N_DEV, nt + LEAD),
            in_specs=[pl.BlockSpec((tm, D), lambda j, i, o: (first_pass(j, i), 0)), vec, vec, vec,
                      pl.BlockSpec((None, D, W_SHARD),
                                   lambda j, i, o: (o[jnp.where(i >= LEAD, j, jnp.maximum(j - 1, 0))], 0, 0)),
                      any_space, _SEM],
            out_specs=[pl.BlockSpec((tm, W_SHARD), lambda j, i, o: (tile(j, i), o[j])),
                       pl.BlockSpec((tm, D), lambda j, i, o: (first_pass(j, i), 0))],
            scratch_shapes=[pltpu.VMEM((S, D), BF16)]),
        out_shape=[jax.ShapeDtypeStruct((S, IN_W), F32), jax.ShapeDtypeStruct((S, D), BF16)],
        compiler_params=_params(has_side_effects=pltpu.SideEffectType.DATAFLOW_SIDE_EFFECTING),
    )(order, x, g_pre, scale, shift, zone, sent, recv_sems)


def _decay_tables():
    log_g = jnp.log1p(-jnp.exp2(-5.0 - jnp.arange(HEADS, dtype=F32)))
    idx = jnp.arange(CHUNK, dtype=F32)
    diff = idx[:, None] - idx[None, :]
    causal = diff >= 0
    mask = jnp.where(causal, jnp.exp(log_g[:, None, None] * jnp.where(causal, diff, 0.0)), 0.0)
    xi = jnp.exp(log_g[:, None] * (idx + 1.0))[:, :, None]
    zeta = jnp.exp(log_g[:, None] * (CHUNK - 1.0 - idx))[:, :, None]
    g_chunk = jnp.broadcast_to(jnp.exp(log_g * CHUNK)[:, None, None], (HEADS, 1, 128))
    return mask, xi, zeta, g_chunk


def _rot(t, cos, sin):
    t1, t2 = t[:, :DK // 2], t[:, DK // 2:]
    return jnp.concatenate([t1 * cos - t2 * sin, t1 * sin + t2 * cos], axis=-1)


def _rot_back(t, cos, sin):
    t1, t2 = t[:, :DK // 2], t[:, DK // 2:]
    return jnp.concatenate([t1 * cos + t2 * sin, t2 * cos - t1 * sin], axis=-1)


def _silu(z):
    return z * _sigmoid(z)


def _retention_fwd(proj, cos, sin, tables, cpb):
    S = proj.shape[0]
    tb = cpb * CHUNK
    mask, xi, zeta, g_chunk = tables

    def body(q_ref, k_ref, v_ref, z_ref, cos_ref, sin_ref, mask_ref, xi_ref, zeta_ref, gc_ref,
             pa_ref, o_ref, st_ref, state):
        @pl.when(pl.program_id(1) == 0)
        def _():
            state[...] = jnp.zeros_like(state)

        m, xi_c, zeta_c, gc = mask_ref[...], xi_ref[...], zeta_ref[...], gc_ref[0:1, 0:1]
        for ci in range(cpb):
            rows = pl.ds(ci * CHUNK, CHUNK)
            cs, sn = cos_ref[rows, :], sin_ref[rows, :]
            qr = _rot(q_ref[rows, :], cs, sn)
            kr = _rot(k_ref[rows, :], cs, sn) * (DK ** -0.5)
            vb = v_ref[rows, :].astype(BF16)
            qb = qr.astype(BF16)
            st = state[...]
            stb = st.astype(BF16)
            st_ref[ci] = stb
            scores = (_dot_nt(qb, kr.astype(BF16)) * m).astype(BF16)
            o = _dot(scores, vb) + _dot(qb, stb) * xi_c
            state[...] = st * gc + _dot_tn((kr * zeta_c).astype(BF16), vb)
            o_ref[rows, :] = o
            mu = jnp.mean(o, axis=-1, keepdims=True)
            oc = o - mu
            var = jnp.mean(oc * oc, axis=-1, keepdims=True)
            pa_ref[rows, :] = (oc * lax.rsqrt(var + GN_EPS) * _silu(z_ref[rows, :])).astype(BF16)

    tab = lambda shape: pl.BlockSpec((None,) + shape, lambda h, n: (h, 0, 0))
    return pl.pallas_call(
        body, name="retention_fwd", grid=(HEADS, S // tb),
        in_specs=[pl.BlockSpec((tb, DK), lambda h, n: (n, h)),
                  pl.BlockSpec((tb, DK), lambda h, n: (n, HEADS + h)),
                  pl.BlockSpec((tb, DV), lambda h, n: (n, COL_V * 2 + h)),
                  pl.BlockSpec((tb, DV), lambda h, n: (n, COL_Z * 2 + h)),
                  pl.BlockSpec((tb, DK // 2), lambda h, n: (n, 0)),
                  pl.BlockSpec((tb, DK // 2), lambda h, n: (n, 0)),
                  tab((CHUNK, CHUNK)), tab((CHUNK, 1)), tab((CHUNK, 1)), tab((1, 128))],
        out_specs=[pl.BlockSpec((tb, DV), lambda h, n: (n, h)),
                   pl.BlockSpec((tb, DV), lambda h, n: (n, h)),
                   pl.BlockSpec((None, cpb, DK, DV), lambda h, n: (h, n, 0, 0))],
        out_shape=[jax.ShapeDtypeStruct((S, HEADS * DV), BF16), jax.ShapeDtypeStruct((S, HEADS * DV), F32),
                   jax.ShapeDtypeStruct((HEADS, S // CHUNK, DK, DV), BF16)],
        scratch_shapes=[pltpu.VMEM((DK, DV), F32)],
        compiler_params=_params(),
    )(proj, proj, proj, proj, cos, sin, mask, xi, zeta, g_chunk)


SUBLANES = 8
FWD_ROWS = 16
EW_ROWS = 16


def _shifted_copies(src, dst):
    rows = dst.shape[1]
    src[pl.ds(rows, SUBLANES), :] = jnp.zeros((SUBLANES, src.shape[1]), src.dtype)
    for s in range(SUBLANES):
        dst[s] = src[pl.ds(s, rows), :]


def _layernorm_stats(a):
    mu = jnp.mean(a, axis=-1, keepdims=True)
    ac = a - mu
    var = jnp.mean(ac * ac, axis=-1, keepdims=True)
    rstd = lax.rsqrt(var + LN_EPS)
    return ac * rstd, rstd


def _conv_fwd(proj, conv_w, conv_b, ln_g, ln_b, tm):
    S = proj.shape[0]
    hb = tm // HALO

    def body(uv_ref, ug_ref, uvh_ref, ugh_ref, z_ref, w_ref, b_ref, g_ref, bb_ref, pb_ref, a1_ref, ext, ex):
        halo = _round_bf16(uvh_ref[...] * _sigmoid(ugh_ref[...]))
        ext[pl.ds(0, HALO), :] = jnp.where(pl.program_id(0) == 0, 0.0, halo)
        ext[pl.ds(HALO, tm), :] = _round_bf16(uv_ref[...] * _sigmoid(ug_ref[...]))
        _shifted_copies(ext, ex)
        bias, g, bb = b_ref[...], g_ref[...], bb_ref[...]

        def rows_body(rb, carry):
            r0 = pl.multiple_of(rb * FWD_ROWS, FWD_ROWS)
            groups = range(FWD_ROWS // SUBLANES)
            acc = [jnp.zeros((SUBLANES, D), F32) for _ in groups]
            for j in range(CONV_W):
                a, s = divmod(HALO - (CONV_W - 1) + j, SUBLANES)
                wj = w_ref[j]
                for k in groups:
                    acc[k] = acc[k] + wj * ex[s, pl.ds(r0 + (a + k) * SUBLANES, SUBLANES), :]
            for k in groups:
                a1_ref[pl.ds(r0 + k * SUBLANES, SUBLANES), :] = acc[k] + bias
            return carry

        lax.fori_loop(0, tm // FWD_ROWS, rows_body, 0)
        xhat, _ = _layernorm_stats(a1_ref[...])
        pb_ref[...] = (_silu(xhat * g + bb) * _silu(z_ref[...])).astype(BF16)

    vec = pl.BlockSpec((1, D), lambda i: (0, 0))
    halo = lambda col: pl.BlockSpec((HALO, D), lambda i: (jnp.maximum(i * hb - 1, 0), col))
    return pl.pallas_call(
        body, name="conv_fwd", grid=(S // tm,),
        in_specs=[pl.BlockSpec((tm, D), lambda i: (i, COL_UVAL)), pl.BlockSpec((tm, D), lambda i: (i, COL_UGATE)),
                  halo(COL_UVAL), halo(COL_UGATE), pl.BlockSpec((tm, D), lambda i: (i, COL_ZCONV)),
                  pl.BlockSpec((CONV_W + 1, SUBLANES, D), lambda i: (0, 0, 0)), vec, vec, vec],
        out_specs=[pl.BlockSpec((tm, D), lambda i: (i, 0))] * 2,
        out_shape=[jax.ShapeDtypeStruct((S, D), BF16), jax.ShapeDtypeStruct((S, D), F32)],
        scratch_shapes=[pltpu.VMEM((HALO + tm + SUBLANES, D), F32), pltpu.VMEM((SUBLANES, HALO + tm, D), F32)],
        compiler_params=_params(),
    )(proj, proj, proj, proj, proj, conv_w, conv_b, ln_g, ln_b)


def _head(p_a, p_b, proj, x, target, gate, g_post, w_ret, w_conv, w_out, tm):
    S = x.shape[0]

    def body(pa_ref, pb_ref, ga_ref, gb_ref, x_ref, t_ref, gate_ref, gp_ref, wr_ref, wc_ref, wo_ref,
             dout_ref, dg_ref, dpa_ref, dpb_ref, mb_ref, dy_ref, dya_ref, dyb_ref, small_ref):
        @pl.when(pl.program_id(0) == 0)
        def _():
            small_ref[...] = jnp.zeros_like(small_ref)

        gate, gp = gate_ref[...], gp_ref[...]
        ya = _dot(pa_ref[...], wr_ref[...])
        yb = _dot(pb_ref[...], wc_ref[...])
        sa, sb = _sigmoid(ga_ref[...]), _sigmoid(gb_ref[...])
        mb = (sa * ya + sb * yb).astype(BF16)
        mb_ref[...] = mb
        y = _dot(mb, wo_ref[...])
        r = lax.rsqrt(jnp.mean(y * y, axis=-1, keepdims=True) + RMS_EPS)
        yhat = y * r
        yn = yhat * gp
        err = (x_ref[...] + gate * yn) - t_ref[...]
        small_ref[2:3, :] += 0.5 * jnp.sum(jnp.mean(err * err, axis=-1, keepdims=True))
        dout = err * (1.0 / D)
        dout_ref[...] = dout
        small_ref[0:1, :] += jnp.sum(dout * yn, axis=0, keepdims=True)
        dyn = dout * gate
        small_ref[1:2, :] += jnp.sum(dyn * yhat, axis=0, keepdims=True)
        dyhat = dyn * gp
        dy = (r * (dyhat - yhat * jnp.mean(dyhat * yhat, axis=-1, keepdims=True))).astype(BF16)
        dy_ref[...] = dy
        dmerged = _dot_nt(dy, wo_ref[...])
        dya = dmerged * sa
        dyb = dmerged * sb
        dg_ref[:, 0:D] = (dya * ya * (1.0 - sa)).astype(BF16)
        dg_ref[:, D:2 * D] = (dyb * yb * (1.0 - sb)).astype(BF16)
        dya, dyb = dya.astype(BF16), dyb.astype(BF16)
        dya_ref[...] = dya
        dyb_ref[...] = dyb
        dpa_ref[...] = _dot_nt(dya, wr_ref[...])
        dpb_ref[...] = _dot_nt(dyb, wc_ref[...])

    vec = pl.BlockSpec((1, D), lambda i: (0, 0))
    tile = lambda w, col=0: pl.BlockSpec((tm, w), lambda i: (i, col))
    whole = lambda a: pl.BlockSpec(a.shape, lambda i: (0, 0), pipeline_mode=pl.Buffered(1))
    act = lambda w: jax.ShapeDtypeStruct((S, w), BF16)
    return pl.pallas_call(
        body, name="head", grid=(S // tm,),
        in_specs=[tile(2 * D), tile(D), tile(D, COL_GA), tile(D, COL_GB), tile(D), tile(D), vec, vec,
                  whole(w_ret), whole(w_conv), whole(w_out)],
        out_specs=[tile(D), tile(2 * D), tile(2 * D), tile(D),
                   tile(D), tile(D), tile(D), tile(D), pl.BlockSpec((8, D), lambda i: (0, 0))],
        out_shape=[jax.ShapeDtypeStruct((S, D), F32),
                   act(2 * D), jax.ShapeDtypeStruct((S, 2 * D), F32), jax.ShapeDtypeStruct((S, D), F32),
                   act(D), act(D), act(D), act(D), jax.ShapeDtypeStruct((8, D), F32)],
        compiler_params=_params(),
    )(p_a, p_b, proj, proj, x, target, gate, g_post, w_ret, w_conv, w_out)


def _matmul_tn(a, b, name, bn, tk):
    S, M = a.shape
    N = b.shape[1]

    def body(a_ref, b_ref, o_ref):
        @pl.when(pl.program_id(1) == 0)
        def _():
            o_ref[...] = jnp.zeros_like(o_ref)

        o_ref[...] += _dot_tn(a_ref[...], b_ref[...])

    return pl.pallas_call(
        body, name=name, grid=(N // bn, S // tk),
        in_specs=[pl.BlockSpec((tk, M), lambda j, k: (k, 0)), pl.BlockSpec((tk, bn), lambda j, k: (k, j))],
        out_specs=pl.BlockSpec((M, bn), lambda j, k: (0, j)),
        out_shape=jax.ShapeDtypeStruct((M, N), F32),
        compiler_params=_params(),
    )(a, b)


SMALL_B_ROWS = 40


def _dsilu(z, s):
    return s * (1.0 + z * (1.0 - s))


def _conv_bwd(dp_b, proj, a1, conv_w, ln_g, ln_b, tm):
    S = proj.shape[0]
    nt = S // tm

    def body(dpb_ref, uv_ref, ug_ref, z_ref, a1_ref, w_ref, g_ref, bb_ref,
             dc_ref, small_ref, dext, dx, carry, dw_acc):
        @pl.when(pl.program_id(0) == 0)
        def _():
            small_ref[...] = jnp.zeros_like(small_ref)
            carry[...] = jnp.zeros_like(carry)

        xhat, rstd = _layernorm_stats(a1_ref[...])
        g = g_ref[...]
        a2 = xhat * g + bb_ref[...]
        s2 = _sigmoid(a2)
        zc = z_ref[...]
        sz = _sigmoid(zc)
        dpb = dpb_ref[...]
        dc_ref[:, 2 * D:3 * D] = (dpb * (a2 * s2) * _dsilu(zc, sz)).astype(BF16)
        da2 = dpb * (zc * sz) * _dsilu(a2, s2)
        small_ref[32:33, :] += jnp.sum(da2 * xhat, axis=0, keepdims=True)
        small_ref[33:34, :] += jnp.sum(da2, axis=0, keepdims=True)
        dxhat = da2 * g
        da1 = rstd * (dxhat - jnp.mean(dxhat, axis=-1, keepdims=True)
                      - xhat * jnp.mean(dxhat * xhat, axis=-1, keepdims=True))
        small_ref[31:32, :] += jnp.sum(da1, axis=0, keepdims=True)
        da1 = _round_bf16(da1)
        dext[pl.ds(0, tm), :] = da1
        dext[pl.ds(tm, HALO), :] = carry[...]
        carry[...] = da1[0:HALO, :]
        _shifted_copies(dext, dx)
        dw_acc[...] = jnp.zeros_like(dw_acc)

        def conv_body(rb, c):
            r0 = pl.multiple_of(rb * EW_ROWS, EW_ROWS)
            rows = pl.ds(r0, EW_ROWS)
            uv = uv_ref[rows, :]
            sg = _sigmoid(ug_ref[rows, :])
            a0 = _round_bf16(uv * sg)
            halves = []
            for k in range(EW_ROWS // SUBLANES):
                a0_k = a0[k * SUBLANES:(k + 1) * SUBLANES, :]
                acc = jnp.zeros((SUBLANES, D), F32)
                for j in range(CONV_W):
                    a, s = divmod(CONV_W - 1 - j, SUBLANES)
                    win = dx[s, pl.ds(r0 + (a + k) * SUBLANES, SUBLANES), :]
                    acc = acc + w_ref[j] * win
                    dw_acc[j] += a0_k * win
                halves.append(acc)
            da0 = jnp.concatenate(halves, axis=0)
            dc_ref[rows, 0:D] = (da0 * sg).astype(BF16)
            dc_ref[rows, D:2 * D] = (da0 * uv * sg * (1.0 - sg)).astype(BF16)
            return c

        lax.fori_loop(0, tm // EW_ROWS, conv_body, 0)
        small_ref[0:CONV_W + 1, :] += jnp.sum(dw_acc[...], axis=1)

    vec = pl.BlockSpec((1, D), lambda i: (0, 0))
    tile = lambda col: pl.BlockSpec((tm, D), lambda i: (nt - 1 - i, col))
    return pl.pallas_call(
        body, name="conv_bwd", grid=(nt,),
        in_specs=[tile(0), tile(COL_UVAL), tile(COL_UGATE), tile(COL_ZCONV), tile(0),
                  pl.BlockSpec((CONV_W + 1, SUBLANES, D), lambda i: (0, 0, 0)), vec, vec],
        out_specs=[pl.BlockSpec((tm, 3 * D), lambda i: (nt - 1 - i, 0)),
                   pl.BlockSpec((SMALL_B_ROWS, D), lambda i: (0, 0))],
        out_shape=[jax.ShapeDtypeStruct((S, 3 * D), BF16), jax.ShapeDtypeStruct((SMALL_B_ROWS, D), F32)],
        scratch_shapes=[pltpu.VMEM((tm + HALO + SUBLANES, D), F32), pltpu.VMEM((SUBLANES, tm + HALO, D), F32),
                        pltpu.VMEM((HALO, D), F32), pltpu.VMEM((CONV_W + 1, SUBLANES, D), F32)],
        compiler_params=_params(),
    )(dp_b, proj, proj, proj, a1, conv_w, ln_g, ln_b)


RET_W = 2 * DK + 2 * DV


def _retention_bwd(dp_a, o, proj, states, cos, sin, tables, cpb):
    S = proj.shape[0]
    tb = cpb * CHUNK
    nb = S // tb
    mask, xi, zeta, g_chunk = tables

    def body(dpa_ref, o_ref, q_ref, k_ref, v_ref, z_ref, st_ref, cos_ref, sin_ref, mask_ref, xi_ref, zeta_ref,
             gc_ref, out_ref, dstate):
        @pl.when(pl.program_id(1) == 0)
        def _():
            dstate[...] = jnp.zeros_like(dstate)

        m, xi_c, zeta_c, gc = mask_ref[...], xi_ref[...], zeta_ref[...], gc_ref[0:1, 0:1]
        for ci in reversed(range(cpb)):
            rows = pl.ds(ci * CHUNK, CHUNK)
            o = o_ref[rows, :]
            oc = o - jnp.mean(o, axis=-1, keepdims=True)
            rstd = lax.rsqrt(jnp.mean(oc * oc, axis=-1, keepdims=True) + GN_EPS)
            rhat = oc * rstd
            z = z_ref[rows, :]
            sz = _sigmoid(z)
            dpa = dpa_ref[rows, :]
            out_ref[rows, 2 * DK + DV:RET_W] = (dpa * rhat * _dsilu(z, sz)).astype(BF16)
            dret = dpa * (z * sz)
            d_o = rstd * (dret - jnp.mean(dret, axis=-1, keepdims=True)
                          - rhat * jnp.mean(dret * rhat, axis=-1, keepdims=True))
            cs, sn = cos_ref[rows, :], sin_ref[rows, :]
            qb = _rot(q_ref[rows, :], cs, sn).astype(BF16)
            kr = _rot(k_ref[rows, :], cs, sn) * (DK ** -0.5)
            kb = kr.astype(BF16)
            kz = (kr * zeta_c).astype(BF16)
            vb = v_ref[rows, :].astype(BF16)
            scores = (_dot_nt(qb, kb) * m).astype(BF16)
            dob = d_o.astype(BF16)
            dox = (d_o * xi_c).astype(BF16)
            dscores = (_dot_nt(dob, vb) * m).astype(BF16)
            dst = dstate[...]
            dstb = dst.astype(BF16)
            dqr = _dot(dscores, kb) + _dot_nt(dox, st_ref[ci])
            dkr = _dot_tn(dscores, qb) + _dot_nt(vb, dstb) * zeta_c
            dv = _dot_tn(scores, dob) + _dot(kz, dstb)
            dstate[...] = dst * gc + _dot_tn(qb, dox)
            out_ref[rows, 0:DK] = _rot_back(dqr, cs, sn).astype(BF16)
            out_ref[rows, DK:2 * DK] = (_rot_back(dkr, cs, sn) * (DK ** -0.5)).astype(BF16)
            out_ref[rows, 2 * DK:2 * DK + DV] = dv.astype(BF16)

    tab = lambda shape: pl.BlockSpec((None,) + shape, lambda h, n: (h, 0, 0))
    blk = lambda w, col0: pl.BlockSpec((tb, w), lambda h, n: (nb - 1 - n, col0 + h))
    return pl.pallas_call(
        body, name="retention_bwd", grid=(HEADS, nb),
        in_specs=[blk(DV, 0), blk(DV, 0), blk(DK, 0), blk(DK, HEADS), blk(DV, COL_V * 2), blk(DV, COL_Z * 2),
                  pl.BlockSpec((None, cpb, DK, DV), lambda h, n: (h, nb - 1 - n, 0, 0)),
                  pl.BlockSpec((tb, DK // 2), lambda h, n: (nb - 1 - n, 0)),
                  pl.BlockSpec((tb, DK // 2), lambda h, n: (nb - 1 - n, 0)),
                  tab((CHUNK, CHUNK)), tab((CHUNK, 1)), tab((CHUNK, 1)), tab((1, 128))],
        out_specs=blk(RET_W, 0),
        out_shape=jax.ShapeDtypeStruct((S, HEADS * RET_W), BF16),
        scratch_shapes=[pltpu.VMEM((DK, DV), F32)],
        compiler_params=_params(),
    )(dp_a, o, proj, proj, proj, proj, states, cos, sin, mask, xi, zeta, g_chunk)


def _to_head_major(w):
    parts = []
    for h in range(HEADS):
        parts += [w[:, h * DK:(h + 1) * DK], w[:, D + h * DK:D + (h + 1) * DK],
                  w[:, 2 * D + h * DV:2 * D + (h + 1) * DV], w[:, 4 * D + h * DV:4 * D + (h + 1) * DV]]
    return jnp.concatenate(parts, axis=1)


def _from_head_major(w):
    starts = (0, DK, 2 * DK, 2 * DK + DV)
    widths = (DK, DK, DV, DV)
    parts = [w[:, h * RET_W + s:h * RET_W + s + n] for s, n in zip(starts, widths) for h in range(HEADS)]
    return jnp.concatenate(parts, axis=1)


def _dh(d_r, d_c, d_g, w_r, w_c, w_g, x, dout, g_pre, scale, tm):
    S = x.shape[0]

    def body(dr_ref, dc_ref, dg_ref, wr_ref, wc_ref, wg_ref, x_ref, dout_ref, g_ref, sc_ref, gx_ref, small_ref):
        @pl.when(pl.program_id(0) == 0)
        def _():
            small_ref[...] = jnp.zeros_like(small_ref)

        dh = _dot_nt(dr_ref[...], wr_ref[...]) + _dot_nt(dc_ref[...], wc_ref[...]) + _dot_nt(dg_ref[...], wg_ref[...])
        xf = x_ref[...]
        r = lax.rsqrt(jnp.mean(xf * xf, axis=-1, keepdims=True) + RMS_EPS)
        xhat = xf * r
        g, sc1 = g_ref[...], 1.0 + sc_ref[...]
        small_ref[0:1, :] += jnp.sum(dh, axis=0, keepdims=True)
        small_ref[1:2, :] += jnp.sum(dh * (xhat * g), axis=0, keepdims=True)
        small_ref[2:3, :] += jnp.sum(dh * sc1 * xhat, axis=0, keepdims=True)
        dxhat = dh * sc1 * g
        gx_ref[...] = dout_ref[...] + r * (dxhat - xhat * jnp.mean(dxhat * xhat, axis=-1, keepdims=True))

    vec = pl.BlockSpec((1, D), lambda i: (0, 0))
    tile = lambda w: pl.BlockSpec((tm, w), lambda i: (i, 0))
    whole = lambda a: pl.BlockSpec(a.shape, lambda i: (0, 0), pipeline_mode=pl.Buffered(1))
    return pl.pallas_call(
        body, name="dh", grid=(S // tm,),
        in_specs=[tile(d_r.shape[1]), tile(d_c.shape[1]), tile(d_g.shape[1]), whole(w_r), whole(w_c), whole(w_g),
                  tile(D), tile(D), vec, vec],
        out_specs=[tile(D), pl.BlockSpec((8, D), lambda i: (0, 0))],
        out_shape=[jax.ShapeDtypeStruct((S, D), F32), jax.ShapeDtypeStruct((8, D), F32)],
        compiler_params=_params(),
    )(d_r, d_c, d_g, w_r, w_c, w_g, x, dout, g_pre, scale)


ROW_GATE, ROW_POST, ROW_LOSS = 0, 1, 2
ROW_CONV_W, ROW_CONV_B, ROW_LN_G, ROW_LN_B = 8, 8 + 31, 8 + 32, 8 + 33
ROW_SHIFT, ROW_SCALE, ROW_PRE = 48, 49, 50


def kernel(x, c, positions, w_ada, b_ada, pre_norm_g, w_in, conv_w, conv_b, conv_ln_g, conv_ln_b, w_ret_out, w_conv_out, w_out, post_norm_g, loss_target, m_w_ada, m_b_ada, m_pre_norm_g, m_w_in, m_conv_w, m_conv_b, m_conv_ln_g, m_conv_ln_b, m_w_ret_out, m_w_conv_out, m_w_out, m_post_norm_g, v_w_ada, v_b_ada, v_pre_norm_g, v_w_in, v_conv_w, v_conv_b, v_conv_ln_g, v_conv_ln_b, v_w_ret_out, v_w_conv_out, v_w_out, v_post_norm_g):
    S = x.shape[1]
    me = 4 * lax.axis_index("x") + 2 * lax.axis_index("y") + lax.axis_index("c")
    x2, target = x[0], loss_target[0]
    ada_w = w_ada.shape[2]
    cw_w = conv_w.shape[2]

    c_g, conv_w_g = _exchange([jnp.pad(c, ((0, 7), (0, 0))), jnp.pad(conv_w[0], ((0, 1), (0, 0)))],
                              False, "gather_c_conv_w", True)
    c_all = c_g[:, 0, :]
    conv_w_full = _round_bf16(conv_w_g.transpose(1, 0, 2).reshape(CONV_W + 1, D))
    conv_w_full = jnp.broadcast_to(conv_w_full[:, None, :], (CONV_W + 1, SUBLANES, D))
    b_part = lax.dynamic_slice(b_ada, (0, me * ada_w), (1, ada_w))
    mod_g = _exchange([_mod_part(c_all, w_ada[0], b_part)], False, "gather_mod", True)[0]

    own_in, mod_g = lax.optimization_barrier((w_in[0].astype(BF16), mod_g))
    in_send, in_recv, in_sent, in_zones, tok_in = _exchange_start([own_in], False, "gather_w_in_start")
    mod = lax.dynamic_index_in_dim(mod_g, me, axis=1, keepdims=False).reshape(1, 3 * D)
    shift, scale, gate = mod[:, :D], mod[:, D:2 * D], mod[:, 2 * D:]
    positions = positions + tok_in[0:1, 0:1].astype(jnp.int32)

    cos, sin = _rope_tables(positions, S)
    tables = _decay_tables()
    proj, h = _inproj_fwd(x2, pre_norm_g, scale, shift, in_zones[0], in_recv, in_sent[0], tm=min(S, 1024))
    g_in, proj = lax.optimization_barrier((in_zones[0], proj))
    _, tok_sent = _exchange_wait(in_send, in_recv, in_sent, in_zones, h, False, "gather_w_in_sends_wait", n_waited=1)
    *rest_handles, tok_rest = _exchange_start(
        [(w_ret_out[0] + tok_sent[0:1, 0:1]).astype(BF16), w_conv_out[0].astype(BF16), w_out[0].astype(BF16)],
        False, "gather_w_rest_start")
    w_in_full = g_in.transpose(1, 0, 2).reshape(D, IN_W)
    p_a, o, states = _retention_fwd(proj, cos, sin, tables, cpb=8)
    p_b, a1 = _conv_fwd(proj, conv_w_full, conv_b + tok_rest[0:1, 0:1], conv_ln_g, conv_ln_b, tm=256)
    (g_ret, g_conv, g_out), _ = _exchange_wait(*rest_handles, p_b, False, "gather_w_rest_wait")
    w_ret_full, w_conv_full, w_out_full = g_ret.reshape(2 * D, D), g_conv.reshape(D, D), g_out.reshape(D, D)

    dout, d_g, dp_a, dp_b, merged, dy, dy_a, dy_b, small_a = _head(
        p_a, p_b, proj, x2, target, gate, post_norm_g, w_ret_full, w_conv_full, w_out_full, tm=256)
    parts_rest = [_matmul_tn(p_a, dy_a, "dw_ret_out", bn=D, tk=min(S, 2048)).astype(BF16).reshape(N_DEV, 2 * D // N_DEV, D),
                  _matmul_tn(p_b, dy_b, "dw_conv_out", bn=D, tk=min(S, 2048)).astype(BF16).reshape(N_DEV, D // N_DEV, D),
                  _matmul_tn(merged, dy, "dw_out", bn=D, tk=min(S, 2048)).astype(BF16).reshape(N_DEV, D // N_DEV, D)]
    *s_rest_handles, tok_s_rest = _exchange_start(parts_rest, True, "scatter_rest_start")
    d_c, small_b = _conv_bwd(dp_b, proj, a1, conv_w_full, conv_ln_g + tok_s_rest[0:1, 0:1], conv_ln_b, tm=256)
    d_r = _retention_bwd(dp_a, o, proj, states, cos, sin, tables, cpb=8)
    dw_r = _matmul_tn(h, d_r, "dw_in_ret", bn=RET_W, tk=min(S, 2048))
    dw_c = _matmul_tn(h, d_c, "dw_in_conv", bn=D, tk=min(S, 2048))
    dw_g = _matmul_tn(h, d_g, "dw_in_gate", bn=D, tk=min(S, 2048))
    dw_in = jnp.concatenate([_from_head_major(dw_r), dw_c, dw_g], axis=1)
    parts_in = dw_in.astype(BF16).reshape(D, N_DEV, W_SHARD).transpose(1, 0, 2)
    *s_in_handles, tok_s_in = _exchange_start([parts_in], True, "scatter_w_in_start")
    grad_x, small_c = _dh(d_r, d_c, d_g, _to_head_major(w_in_full), w_in_full[:, 6 * D:9 * D], w_in_full[:, 9 * D:],
                          x2, dout, pre_norm_g + tok_s_in[0:1, 0:1], scale, tm=256)

    (r_ret, r_conv, r_out), _ = _exchange_wait(*s_rest_handles, small_c, True, "scatter_rest_wait")
    (r_in,), _ = _exchange_wait(*s_in_handles, small_c, True, "scatter_w_in_wait")
    big = {"w_in": _adam_shard(r_in, w_in[0], m_w_in[0], v_w_in[0], "adam_w_in", 128),
           "w_ret_out": _adam_shard(r_ret, w_ret_out[0], m_w_ret_out[0], v_w_ret_out[0], "adam_w_ret_out", 128),
           "w_conv_out": _adam_shard(r_conv, w_conv_out[0], m_w_conv_out[0], v_w_conv_out[0], "adam_w_conv_out", 128),
           "w_out": _adam_shard(r_out, w_out[0], m_w_out[0], v_w_out[0], "adam_w_out", 128)}

    sm_g = _exchange([jnp.concatenate([small_a, small_b, small_c], axis=0)], False, "gather_small", True)[0]
    sm = _sum_devices(sm_g)
    row = lambda r: sm[r:r + 1]
    g8 = jnp.concatenate([row(ROW_SHIFT), row(ROW_SCALE), row(ROW_GATE), row(ROW_PRE), row(ROW_CONV_B),
                          row(ROW_LN_G), row(ROW_LN_B), row(ROW_POST)], axis=0)
    stack8 = lambda ba, pre, cb, lg, lb, post: jnp.concatenate([ba.reshape(3, D), pre, cb, lg, lb, post], axis=0)
    d8, m8, v8 = _adam_small(
        g8, stack8(b_ada, pre_norm_g, conv_b, conv_ln_g, conv_ln_b, post_norm_g),
        stack8(m_b_ada, m_pre_norm_g, m_conv_b, m_conv_ln_g, m_conv_ln_b, m_post_norm_g),
        stack8(v_b_ada, v_pre_norm_g, v_conv_b, v_conv_ln_g, v_conv_ln_b, v_post_norm_g), "adam_small")
    unstack8 = lambda a: {"b_ada": a[0:3].reshape(3 * D), "pre_norm_g": a[3], "conv_b": a[4],
                          "conv_ln_g": a[5], "conv_ln_b": a[6], "post_norm_g": a[7]}
    g_small, d_small, m_small, v_small = unstack8(g8), unstack8(d8), unstack8(m8), unstack8(v8)

    pad_row = lambda a: jnp.pad(a[0], ((0, 1), (0, 0)))
    g_cw = lax.dynamic_slice(sm[ROW_CONV_W:ROW_CONV_W + CONV_W + 1], (0, me * cw_w), (CONV_W + 1, cw_w))
    d_cw, m_cw, v_cw = _adam_small(g_cw, pad_row(conv_w), pad_row(m_conv_w), pad_row(v_conv_w), "adam_conv_w")

    dmod_all = jnp.concatenate([sm_g[:, ROW_SHIFT], sm_g[:, ROW_SCALE], sm_g[:, ROW_GATE]], axis=1)
    g_wa = _grad_w_ada(c_all.T, lax.dynamic_slice(dmod_all, (0, me * ada_w), (N_DEV, ada_w)))
    d_wa, m_wa, v_wa = _adam_small(g_wa, w_ada[0], m_w_ada[0], v_w_ada[0], "adam_w_ada")

    grads = {"w_ada": g_wa, "conv_w": g_cw[:CONV_W], **g_small, **{n: t[0] for n, t in big.items()}}
    deltas = {"w_ada": d_wa, "conv_w": d_cw[:CONV_W], **d_small, **{n: t[1] for n, t in big.items()}}
    new_m = {"w_ada": m_wa, "conv_w": m_cw[:CONV_W], **m_small, **{n: t[2] for n, t in big.items()}}
    new_v = {"w_ada": v_wa, "conv_w": v_cw[:CONV_W], **v_small, **{n: t[3] for n, t in big.items()}}
    order = ["w_ada", "b_ada", "pre_norm_g", "w_in", "conv_w", "conv_b", "conv_ln_g", "conv_ln_b", "w_ret_out",
             "w_conv_out", "w_out", "post_norm_g"]
    loss = sm[ROW_LOSS, 0]
    out = [loss, grad_x[None]]
    for group in (grads, deltas, new_m, new_v):
        out += [group[n][None] for n in order]
    return tuple(out)
```

```python
import jax
import jax.numpy as jnp
import numpy as np
from jax import lax
from jax.experimental import pallas as pl
from jax.experimental.pallas import tpu as pltpu

F32 = jnp.float32
BF16 = jnp.bfloat16
MESH = pl.DeviceIdType.MESH

N_DEV = 8
D = 1024
HEADS = 4
DK = 256
DV = 512
CHUNK = 128
CONV_W = 31
HALO = 32
IN_W = 11264
W_SHARD = IN_W // N_DEV
ROPE_BASE = 10000.0
GN_EPS = 1e-5
LN_EPS = 1e-5
RMS_EPS = 1e-6
ADAM_LR, ADAM_B1, ADAM_B2, ADAM_EPS, ADAM_WD, ADAM_STEP = 0.001, 0.9, 0.999, 1e-08, 0.01, 10

COL_V, COL_Z, COL_UVAL, COL_UGATE, COL_ZCONV, COL_GA, COL_GB = 2, 4, 6, 7, 8, 9, 10

VMEM_LIMIT = 56 * 1024 * 1024


def _params(**kw):
    return pltpu.CompilerParams(vmem_limit_bytes=VMEM_LIMIT, **kw)


def _sigmoid(z):
    return jax.nn.sigmoid(z)


def _round_bf16(a):
    return a.astype(BF16).astype(F32)


def _dot(a, b):
    return jnp.dot(a, b, preferred_element_type=F32)


def _dot_nt(a, b):
    return lax.dot_general(a, b, (((1,), (1,)), ((), ())), preferred_element_type=F32)


def _dot_tn(a, b):
    return lax.dot_general(a, b, (((0,), (0,)), ((), ())), preferred_element_type=F32)


def _my_place():
    return lax.axis_index("x"), lax.axis_index("y"), lax.axis_index("c")


def _peer(k):
    x, y, c = _my_place()
    px = lax.rem(x + ((k >> 2) & 1), 2)
    py = lax.rem(y + ((k >> 1) & 1), 2)
    pc = lax.rem(c + (k & 1), 2)
    return (px, py, pc), 4 * px + 2 * py + pc


def _exchange(arrs, scatter, name, in_vmem):
    n = len(arrs)

    def body(*refs):
        ins, outs = refs[:n], refs[n:2 * n]
        send_sems, recv_sems, local_sems = refs[2 * n:]
        x, y, c = _my_place()
        me = 4 * x + 2 * y + c
        copies = []
        for i in range(n):
            src = ins[i].at[me] if scatter else ins[i]
            cp = pltpu.make_async_copy(src, outs[i].at[me], local_sems.at[i])
            cp.start()
            copies.append(cp)
        for k in range(1, N_DEV):
            peer, peer_idx = _peer(k)
            for i in range(n):
                src = ins[i].at[peer_idx] if scatter else ins[i]
                cp = pltpu.make_async_remote_copy(
                    src_ref=src, dst_ref=outs[i].at[me], send_sem=send_sems.at[i, k - 1],
                    recv_sem=recv_sems.at[i, k - 1], device_id=peer, device_id_type=MESH)
                cp.start()
                copies.append(cp)
        for cp in copies:
            cp.wait()

    space = pltpu.VMEM if in_vmem else pl.ANY
    out_shape = [jax.ShapeDtypeStruct(a.shape if scatter else (N_DEV,) + a.shape, a.dtype) for a in arrs]
    return pl.pallas_call(
        body, name=name, out_shape=out_shape,
        in_specs=[pl.BlockSpec(memory_space=space)] * n,
        out_specs=[pl.BlockSpec(memory_space=space)] * n,
        scratch_shapes=[pltpu.SemaphoreType.DMA((n, N_DEV - 1)), pltpu.SemaphoreType.DMA((n, N_DEV - 1)),
                        pltpu.SemaphoreType.DMA((n,))],
        compiler_params=_params(has_side_effects=True),
    )(*arrs)


def _remote_copies(ins, lands, send_sems, recv_sems, scatter, first=0):
    x, y, c = _my_place()
    me = 4 * x + 2 * y + c
    copies = []
    for i in range(len(ins)):
        for k in range(1, N_DEV):
            peer, peer_idx = _peer(k)
            sem = (N_DEV - 1) * (first + i) + k - 1
            copies.append(pltpu.make_async_remote_copy(
                src_ref=ins[i].at[peer_idx] if scatter else ins[i], dst_ref=lands[i].at[me],
                send_sem=send_sems.at[sem], recv_sem=recv_sems.at[sem], device_id=peer, device_id_type=MESH))
    return copies


_HBM = pl.BlockSpec(memory_space=pltpu.HBM)
_SEM = pl.BlockSpec(memory_space=pltpu.SEMAPHORE)


def _exchange_start(arrs, scatter, name):
    n = len(arrs)
    me = 4 * lax.axis_index("x") + 2 * lax.axis_index("y") + lax.axis_index("c")
    lands = []
    for a in arrs:
        own = lax.dynamic_index_in_dim(a, me, axis=0) if scatter else a[None]
        zone = lax.empty(a.shape if scatter else (N_DEV,) + a.shape, a.dtype)
        lands.append(lax.dynamic_update_slice(zone, own, (me,) + (0,) * (own.ndim - 1)))

    def body(*refs):
        ins, land_refs, send_sems, recv_sems, token = refs[:n], refs[n:2 * n], refs[2 * n], refs[2 * n + 1], refs[-1]
        for cp in _remote_copies(ins, land_refs, send_sems, recv_sems, scatter):
            cp.start()
        token[...] = jnp.zeros_like(token)

    n_sem = (N_DEV - 1) * n
    hbm = lambda a: pltpu.HBM(a.shape, a.dtype)
    outs = pl.pallas_call(
        body, name=name,
        out_shape=(pltpu.SemaphoreType.DMA((n_sem,)), pltpu.SemaphoreType.DMA((n_sem,)), *[hbm(a) for a in arrs],
                   *[hbm(a) for a in lands], jax.ShapeDtypeStruct((8, 128), F32)),
        in_specs=(_HBM,) * (2 * n), out_specs=(_SEM, _SEM) + (_HBM,) * (2 * n) + (pl.BlockSpec(memory_space=pltpu.VMEM),),
        input_output_aliases={i: 2 + i for i in range(2 * n)},
        compiler_params=pltpu.CompilerParams(has_side_effects=pltpu.SideEffectType.DATAFLOW_SIDE_EFFECTING),
    )(*[pltpu.with_memory_space_constraint(a, pltpu.HBM) for a in list(arrs) + lands])
    return outs[0], outs[1], list(outs[2:2 + n]), list(outs[2 + n:2 + 2 * n]), outs[-1]


def _exchange_wait(send_sems, recv_sems, sent, zones, after, scatter, name, n_waited=0):
    n = len(sent)
    m = n - n_waited

    def body(*refs):
        early, ins, land_refs = refs[:n_waited], refs[n_waited:n], refs[n:n + m]
        s_sems, r_sems, token = refs[n + m], refs[n + m + 1], refs[-1]
        token[...] = jnp.zeros_like(token)
        for i in range(n_waited):
            for k in range(1, N_DEV):
                pltpu.make_async_remote_copy(
                    src_ref=early[i], dst_ref=early[i], send_sem=s_sems.at[(N_DEV - 1) * i + k - 1],
                    recv_sem=s_sems.at[(N_DEV - 1) * i + k - 1], device_id=_peer(k)[0], device_id_type=MESH).wait_send()
        for cp in _remote_copies(ins, land_refs, s_sems, r_sems, scatter, first=n_waited):
            cp.wait_send()
            cp.wait_recv()

    thru = list(sent[n_waited:]) + list(zones[n_waited:])
    outs = pl.pallas_call(
        body, name=name,
        out_shape=tuple(pltpu.HBM(a.shape, a.dtype) for a in thru) + (jax.ShapeDtypeStruct((8, 128), F32),),
        in_specs=(_HBM,) * (n_waited + 2 * m) + (_SEM, _SEM, pl.BlockSpec(memory_space=pl.ANY)),
        out_specs=(_HBM,) * (2 * m) + (pl.BlockSpec(memory_space=pltpu.VMEM),),
        input_output_aliases={n_waited + i: i for i in range(2 * m)},
        compiler_params=pltpu.CompilerParams(has_side_effects=pltpu.SideEffectType.DATAFLOW_SIDE_EFFECTING),
    )(*sent[:n_waited], *thru, send_sems, recv_sems, after)
    return list(outs[m:2 * m]), outs[-1]


def _rope_tables(positions, S):
    tm = min(S, 1024)

    half = DK // 2
    inv_freq = (ROPE_BASE ** (-jnp.arange(half, dtype=F32) / half)).reshape(1, half)

    def body(pos_ref, f_ref, cos_ref, sin_ref):
        ang = pos_ref[...].astype(F32) * f_ref[...]
        cos_ref[...] = jnp.cos(ang)
        sin_ref[...] = jnp.sin(ang)

    return pl.pallas_call(
        body, name="rope_tables", grid=(S // tm,),
        in_specs=[pl.BlockSpec((tm, 1), lambda i: (i, 0)), pl.BlockSpec((1, half), lambda i: (0, 0))],
        out_specs=[pl.BlockSpec((tm, half), lambda i: (i, 0))] * 2,
        out_shape=[jax.ShapeDtypeStruct((S, half), F32)] * 2,
    )(positions.reshape(S, 1), inv_freq)


def _mod_part(c_all, w_ada, b_part):
    def body(c_ref, w_ref, b_ref, o_ref):
        o_ref[...] = _dot(c_ref[...].astype(BF16), w_ref[...].astype(BF16)) + b_ref[...]

    return pl.pallas_call(body, name="mod_part",
                          out_shape=jax.ShapeDtypeStruct((N_DEV, w_ada.shape[1]), F32))(c_all, w_ada, b_part)


def _sum_devices(g):
    def body(g_ref, o_ref):
        acc = g_ref[0]
        for d in range(1, N_DEV):
            acc = acc + g_ref[d]
        o_ref[...] = acc

    return pl.pallas_call(body, name="sum_devices", out_shape=jax.ShapeDtypeStruct(g.shape[1:], F32))(g)


def _grad_w_ada(c_t, dmod):
    def body(c_ref, d_ref, o_ref):
        acc = c_ref[:, 0:1] * d_ref[0:1, :]
        for b in range(1, N_DEV):
            acc = acc + c_ref[:, b:b + 1] * d_ref[b:b + 1, :]
        o_ref[...] = acc

    return pl.pallas_call(body, name="grad_w_ada",
                          out_shape=jax.ShapeDtypeStruct((c_t.shape[0], dmod.shape[1]), F32))(c_t, dmod)


def _adam_math(w, g, m, v):
    m = ADAM_B1 * m + (1.0 - ADAM_B1) * g
    v = ADAM_B2 * v + (1.0 - ADAM_B2) * (g * g)
    m_hat = m / (1.0 - ADAM_B1 ** ADAM_STEP)
    v_hat = v / (1.0 - ADAM_B2 ** ADAM_STEP)
    delta = -ADAM_LR * (m_hat / (jnp.sqrt(v_hat) + ADAM_EPS) + ADAM_WD * w)
    return delta, m, v


def _adam_small(g, w, m, v, name):
    def body(g_ref, w_ref, m_ref, v_ref, d_ref, nm_ref, nv_ref):
        d_ref[...], nm_ref[...], nv_ref[...] = _adam_math(w_ref[...], g_ref[...], m_ref[...], v_ref[...])

    return pl.pallas_call(body, name=name, out_shape=[jax.ShapeDtypeStruct(w.shape, F32)] * 3)(g, w, m, v)


def _adam_shard(parts, w, m, v, name, tr):
    R, L = w.shape

    def body(p_ref, w_ref, m_ref, v_ref, g_ref, d_ref, nm_ref, nv_ref):
        g = p_ref[0].astype(F32)
        for d in range(1, N_DEV):
            g = g + p_ref[d].astype(F32)
        g_ref[...] = g
        d_ref[...], nm_ref[...], nv_ref[...] = _adam_math(w_ref[...], g, m_ref[...], v_ref[...])

    blk = pl.BlockSpec((tr, L), lambda i: (i, 0))
    return pl.pallas_call(
        body, name=name, grid=(R // tr,),
        in_specs=[pl.BlockSpec((N_DEV, tr, L), lambda i: (0, i, 0)), blk, blk, blk],
        out_specs=[blk] * 4, out_shape=[jax.ShapeDtypeStruct((R, L), F32)] * 4,
        compiler_params=_params(),
    )(parts, w, m, v)


ARRIVAL = (0, 1, 2, 4, 3, 5, 6, 7)
LEAD = 2


def _inproj_fwd(x, g_pre, scale, shift, zone, recv_sems, sent, tm):
    S = x.shape[0]
    nt = S // tm
    me = 4 * lax.axis_index("x") + 2 * lax.axis_index("y") + lax.axis_index("c")
    order = jnp.bitwise_xor(me, jnp.asarray(ARRIVAL, jnp.int32))

    def body(order_ref, x_ref, g_ref, sc_ref, sh_ref, w_ref, sent_ref, sem_ref, proj_ref, h_ref, h_all):
        j, i = pl.program_id(0), pl.program_id(1)
        for row in range(1, N_DEV):
            @pl.when((j == row) & (i == 0))
            def _():
                k = ARRIVAL[row]
                pltpu.make_async_remote_copy(
                    src_ref=sent_ref, dst_ref=sent_ref, send_sem=sem_ref.at[k - 1],
                    recv_sem=sem_ref.at[k - 1], device_id=_peer(k)[0], device_id_type=MESH).wait_recv()

        rows = pl.ds(pl.multiple_of(jnp.maximum(i - LEAD, 0) * tm, tm), tm)

        @pl.when((j == 0) & (i >= LEAD))
        def _():
            xf = x_ref[...]
            r = lax.rsqrt(jnp.mean(xf * xf, axis=-1, keepdims=True) + RMS_EPS)
            h = ((xf * r * g_ref[...]) * (1.0 + sc_ref[...]) + sh_ref[...]).astype(BF16)
            h_all[rows, :] = h
            h_ref[...] = h

        @pl.when(i >= LEAD)
        def _():
            proj_ref[...] = _dot(h_all[rows, :], w_ref[...])

    tile = lambda j, i: jnp.maximum(i - LEAD, 0)
    first_pass = lambda j, i: jnp.where(j == 0, tile(j, i), nt - 1)
    vec = pl.BlockSpec((1, D), lambda j, i, o: (0, 0))
    any_space = pl.BlockSpec(memory_space=pl.ANY)
    return pl.pallas_call(
        body, name="inproj_fwd",
        grid_spec=pltpu.PrefetchScalarGridSpec(
            num_scalar_prefetch=1, grid=(N_DEV, nt + LEAD),
            in_specs=[pl.BlockSpec((tm, D), lambda j, i, o: (first_pass(j, i), 0)), vec, vec, vec,
                      pl.BlockSpec((None, D, W_SHARD),
                                   lambda j, i, o: (o[jnp.where(i >= LEAD, j, jnp.maximum(j - 1, 0))], 0, 0)),
                      any_space, _SEM],
            out_specs=[pl.BlockSpec((tm, W_SHARD), lambda j, i, o: (tile(j, i), o[j])),
                       pl.BlockSpec((tm, D), lambda j, i, o: (first_pass(j, i), 0))],
            scratch_shapes=[pltpu.VMEM((S, D), BF16)]),
        out_shape=[jax.ShapeDtypeStruct((S, IN_W), F32), jax.ShapeDtypeStruct((S, D), BF16)],
        compiler_params=_params(has_side_effects=pltpu.SideEffectType.DATAFLOW_SIDE_EFFECTING),
    )(order, x, g_pre, scale, shift, zone, sent, recv_sems)


def _decay_tables():
    log_g = jnp.log1p(-jnp.exp2(-5.0 - jnp.arange(HEADS, dtype=F32)))
    idx = jnp.arange(CHUNK, dtype=F32)
    diff = idx[:, None] - idx[None, :]
    causal = diff >= 0
    mask = jnp.where(causal, jnp.exp(log_g[:, None, None] * jnp.where(causal, diff, 0.0)), 0.0)
    xi = jnp.exp(log_g[:, None] * (idx + 1.0))[:, :, None]
    zeta = jnp.exp(log_g[:, None] * (CHUNK - 1.0 - idx))[:, :, None]
    g_chunk = jnp.broadcast_to(jnp.exp(log_g * CHUNK)[:, None, None], (HEADS, 1, 128))
    return mask, xi, zeta, g_chunk


def _rot(t, cos, sin):
    t1, t2 = t[:, :DK // 2], t[:, DK // 2:]
    return jnp.concatenate([t1 * cos - t2 * sin, t1 * sin + t2 * cos], axis=-1)


def _rot_back(t, cos, sin):
    t1, t2 = t[:, :DK // 2], t[:, DK // 2:]
    return jnp.concatenate([t1 * cos + t2 * sin, t2 * cos - t1 * sin], axis=-1)


def _silu(z):
    return z * _sigmoid(z)


def _retention_fwd(proj, cos, sin, tables, cpb):
    S = proj.shape[0]
    tb = cpb * CHUNK
    mask, xi, zeta, g_chunk = tables

    def body(q_ref, k_ref, v_ref, z_ref, cos_ref, sin_ref, mask_ref, xi_ref, zeta_ref, gc_ref,
             pa_ref, o_ref, st_ref, state):
        @pl.when(pl.program_id(1) == 0)
        def _():
            state[...] = jnp.zeros_like(state)

        m, xi_c, zeta_c, gc = mask_ref[...], xi_ref[...], zeta_ref[...], gc_ref[0:1, 0:1]
        for ci in range(cpb):
            rows = pl.ds(ci * CHUNK, CHUNK)
            cs, sn = cos_ref[rows, :], sin_ref[rows, :]
            qr = _rot(q_ref[rows, :], cs, sn)
            kr = _rot(k_ref[rows, :], cs, sn) * (DK ** -0.5)
            vb = v_ref[rows, :].astype(BF16)
            qb = qr.astype(BF16)
            st = state[...]
            stb = st.astype(BF16)
            st_ref[ci] = stb
            scores = (_dot_nt(qb, kr.astype(BF16)) * m).astype(BF16)
            o = _dot(scores, vb) + _dot(qb, stb) * xi_c
            state[...] = st * gc + _dot_tn((kr * zeta_c).astype(BF16), vb)
            o_ref[rows, :] = o
            mu = jnp.mean(o, axis=-1, keepdims=True)
            oc = o - mu
            var = jnp.mean(oc * oc, axis=-1, keepdims=True)
            pa_ref[rows, :] = (oc * lax.rsqrt(var + GN_EPS) * _silu(z_ref[rows, :])).astype(BF16)

    tab = lambda shape: pl.BlockSpec((None,) + shape, lambda h, n: (h, 0, 0))
    return pl.pallas_call(
        body, name="retention_fwd", grid=(HEADS, S // tb),
        in_specs=[pl.BlockSpec((tb, DK), lambda h, n: (n, h)),
                  pl.BlockSpec((tb, DK), lambda h, n: (n, HEADS + h)),
                  pl.BlockSpec((tb, DV), lambda h, n: (n, COL_V * 2 + h)),
                  pl.BlockSpec((tb, DV), lambda h, n: (n, COL_Z * 2 + h)),
                  pl.BlockSpec((tb, DK // 2), lambda h, n: (n, 0)),
                  pl.BlockSpec((tb, DK // 2), lambda h, n: (n, 0)),
                  tab((CHUNK, CHUNK)), tab((CHUNK, 1)), tab((CHUNK, 1)), tab((1, 128))],
        out_specs=[pl.BlockSpec((tb, DV), lambda h, n: (n, h)),
                   pl.BlockSpec((tb, DV), lambda h, n: (n, h)),
                   pl.BlockSpec((None, cpb, DK, DV), lambda h, n: (h, n, 0, 0))],
        out_shape=[jax.ShapeDtypeStruct((S, HEADS * DV), BF16), jax.ShapeDtypeStruct((S, HEADS * DV), F32),
                   jax.ShapeDtypeStruct((HEADS, S // CHUNK, DK, DV), BF16)],
        scratch_shapes=[pltpu.VMEM((DK, DV), F32)],
        compiler_params=_params(),
    )(proj, proj, proj, proj, cos, sin, mask, xi, zeta, g_chunk)


SUBLANES = 8
FWD_ROWS = 16
EW_ROWS = 16


def _shifted_copies(src, dst):
    rows = dst.shape[1]
    src[pl.ds(rows, SUBLANES), :] = jnp.zeros((SUBLANES, src.shape[1]), src.dtype)
    for s in range(SUBLANES):
        dst[s] = src[pl.ds(s, rows), :]


def _layernorm_stats(a):
    mu = jnp.mean(a, axis=-1, keepdims=True)
    ac = a - mu
    var = jnp.mean(ac * ac, axis=-1, keepdims=True)
    rstd = lax.rsqrt(var + LN_EPS)
    return ac * rstd, rstd


def _conv_fwd(proj, conv_w, conv_b, ln_g, ln_b, tm):
    S = proj.shape[0]
    hb = tm // HALO

    def body(uv_ref, ug_ref, uvh_ref, ugh_ref, z_ref, w_ref, b_ref, g_ref, bb_ref, pb_ref, a1_ref, ext, ex):
        halo = _round_bf16(uvh_ref[...] * _sigmoid(ugh_ref[...]))
        ext[pl.ds(0, HALO), :] = jnp.where(pl.program_id(0) == 0, 0.0, halo)
        ext[pl.ds(HALO, tm), :] = _round_bf16(uv_ref[...] * _sigmoid(ug_ref[...]))
        _shifted_copies(ext, ex)
        bias, g, bb = b_ref[...], g_ref[...], bb_ref[...]

        def rows_body(rb, carry):
            r0 = pl.multiple_of(rb * FWD_ROWS, FWD_ROWS)
            groups = range(FWD_ROWS // SUBLANES)
            acc = [jnp.zeros((SUBLANES, D), F32) for _ in groups]
            for j in range(CONV_W):
                a, s = divmod(HALO - (CONV_W - 1) + j, SUBLANES)
                wj = w_ref[j]
                for k in groups:
                    acc[k] = acc[k] + wj * ex[s, pl.ds(r0 + (a + k) * SUBLANES, SUBLANES), :]
            for k in groups:
                a1_ref[pl.ds(r0 + k * SUBLANES, SUBLANES), :] = acc[k] + bias
            return carry

        lax.fori_loop(0, tm // FWD_ROWS, rows_body, 0)
        xhat, _ = _layernorm_stats(a1_ref[...])
        pb_ref[...] = (_silu(xhat * g + bb) * _silu(z_ref[...])).astype(BF16)

    vec = pl.BlockSpec((1, D), lambda i: (0, 0))
    halo = lambda col: pl.BlockSpec((HALO, D), lambda i: (jnp.maximum(i * hb - 1, 0), col))
    return pl.pallas_call(
        body, name="conv_fwd", grid=(S // tm,),
        in_specs=[pl.BlockSpec((tm, D), lambda i: (i, COL_UVAL)), pl.BlockSpec((tm, D), lambda i: (i, COL_UGATE)),
                  halo(COL_UVAL), halo(COL_UGATE), pl.BlockSpec((tm, D), lambda i: (i, COL_ZCONV)),
                  pl.BlockSpec((CONV_W + 1, SUBLANES, D), lambda i: (0, 0, 0)), vec, vec, vec],
        out_specs=[pl.BlockSpec((tm, D), lambda i: (i, 0))] * 2,
        out_shape=[jax.ShapeDtypeStruct((S, D), BF16), jax.ShapeDtypeStruct((S, D), F32)],
        scratch_shapes=[pltpu.VMEM((HALO + tm + SUBLANES, D), F32), pltpu.VMEM((SUBLANES, HALO + tm, D), F32)],
        compiler_params=_params(),
    )(proj, proj, proj, proj, proj, conv_w, conv_b, ln_g, ln_b)


def _head(p_a, p_b, proj, x, target, gate, g_post, w_ret, w_conv, w_out, tm):
    S = x.shape[0]

    def body(pa_ref, pb_ref, ga_ref, gb_ref, x_ref, t_ref, gate_ref, gp_ref, wr_ref, wc_ref, wo_ref,
             dout_ref, dg_ref, dpa_ref, dpb_ref, mb_ref, dy_ref, dya_ref, dyb_ref, small_ref):
        @pl.when(pl.program_id(0) == 0)
        def _():
            small_ref[...] = jnp.zeros_like(small_ref)

        gate, gp = gate_ref[...], gp_ref[...]
        ya = _dot(pa_ref[...], wr_ref[...])
        yb = _dot(pb_ref[...], wc_ref[...])
        sa, sb = _sigmoid(ga_ref[...]), _sigmoid(gb_ref[...])
        mb = (sa * ya + sb * yb).astype(BF16)
        mb_ref[...] = mb
        y = _dot(mb, wo_ref[...])
        r = lax.rsqrt(jnp.mean(y * y, axis=-1, keepdims=True) + RMS_EPS)
        yhat = y * r
        yn = yhat * gp
        err = (x_ref[...] + gate * yn) - t_ref[...]
        small_ref[2:3, :] += 0.5 * jnp.sum(jnp.mean(err * err, axis=-1, keepdims=True))
        dout = err * (1.0 / D)
        dout_ref[...] = dout
        small_ref[0:1, :] += jnp.sum(dout * yn, axis=0, keepdims=True)
        dyn = dout * gate
        small_ref[1:2, :] += jnp.sum(dyn * yhat, axis=0, keepdims=True)
        dyhat = dyn * gp
        dy = (r * (dyhat - yhat * jnp.mean(dyhat * yhat, axis=-1, keepdims=True))).astype(BF16)
        dy_ref[...] = dy
        dmerged = _dot_nt(dy, wo_ref[...])
        dya = dmerged * sa
        dyb = dmerged * sb
        dg_ref[:, 0:D] = (dya * ya * (1.0 - sa)).astype(BF16)
        dg_ref[:, D:2 * D] = (dyb * yb * (1.0 - sb)).astype(BF16)
        dya, dyb = dya.astype(BF16), dyb.astype(BF16)
        dya_ref[...] = dya
        dyb_ref[...] = dyb
        dpa_ref[...] = _dot_nt(dya, wr_ref[...])
        dpb_ref[...] = _dot_nt(dyb, wc_ref[...])

    vec = pl.BlockSpec((1, D), lambda i: (0, 0))
    tile = lambda w, col=0: pl.BlockSpec((tm, w), lambda i: (i, col))
    whole = lambda a: pl.BlockSpec(a.shape, lambda i: (0, 0), pipeline_mode=pl.Buffered(1))
    act = lambda w: jax.ShapeDtypeStruct((S, w), BF16)
    return pl.pallas_call(
        body, name="head", grid=(S // tm,),
        in_specs=[tile(2 * D), tile(D), tile(D, COL_GA), tile(D, COL_GB), tile(D), tile(D), vec, vec,
                  whole(w_ret), whole(w_conv), whole(w_out)],
        out_specs=[tile(D), tile(2 * D), tile(2 * D), tile(D),
                   tile(D), tile(D), tile(D), tile(D), pl.BlockSpec((8, D), lambda i: (0, 0))],
        out_shape=[jax.ShapeDtypeStruct((S, D), F32),
                   act(2 * D), jax.ShapeDtypeStruct((S, 2 * D), F32), jax.ShapeDtypeStruct((S, D), F32),
                   act(D), act(D), act(D), act(D), jax.ShapeDtypeStruct((8, D), F32)],
        compiler_params=_params(),
    )(p_a, p_b, proj, proj, x, target, gate, g_post, w_ret, w_conv, w_out)


def _matmul_tn(a, b, name, bn, tk):
    S, M = a.shape
    N = b.shape[1]

    def body(a_ref, b_ref, o_ref):
        @pl.when(pl.program_id(1) == 0)
        def _():
            o_ref[...] = jnp.zeros_like(o_ref)

        o_ref[...] += _dot_tn(a_ref[...], b_ref[...])

    return pl.pallas_call(
        body, name=name, grid=(N // bn, S // tk),
        in_specs=[pl.BlockSpec((tk, M), lambda j, k: (k, 0)), pl.BlockSpec((tk, bn), lambda j, k: (k, j))],
        out_specs=pl.BlockSpec((M, bn), lambda j, k: (0, j)),
        out_shape=jax.ShapeDtypeStruct((M, N), F32),
        compiler_params=_params(),
    )(a, b)


SMALL_B_ROWS = 40


def _dsilu(z, s):
    return s * (1.0 + z * (1.0 - s))


def _conv_bwd(dp_b, proj, a1, conv_w, ln_g, ln_b, tm):
    S = proj.shape[0]
    nt = S // tm

    def body(dpb_ref, uv_ref, ug_ref, z_ref, a1_ref, w_ref, g_ref, bb_ref,
             dc_ref, small_ref, dext, dx, carry, dw_acc):
        @pl.when(pl.program_id(0) == 0)
        def _():
            small_ref[...] = jnp.zeros_like(small_ref)
            carry[...] = jnp.zeros_like(carry)

        xhat, rstd = _layernorm_stats(a1_ref[...])
        g = g_ref[...]
        a2 = xhat * g + bb_ref[...]
        s2 = _sigmoid(a2)
        zc = z_ref[...]
        sz = _sigmoid(zc)
        dpb = dpb_ref[...]
        dc_ref[:, 2 * D:3 * D] = (dpb * (a2 * s2) * _dsilu(zc, sz)).astype(BF16)
        da2 = dpb * (zc * sz) * _dsilu(a2, s2)
        small_ref[32:33, :] += jnp.sum(da2 * xhat, axis=0, keepdims=True)
        small_ref[33:34, :] += jnp.sum(da2, axis=0, keepdims=True)
        dxhat = da2 * g
        da1 = rstd * (dxhat - jnp.mean(dxhat, axis=-1, keepdims=True)
                      - xhat * jnp.mean(dxhat * xhat, axis=-1, keepdims=True))
        small_ref[31:32, :] += jnp.sum(da1, axis=0, keepdims=True)
        da1 = _round_bf16(da1)
        dext[pl.ds(0, tm), :] = da1
        dext[pl.ds(tm, HALO), :] = carry[...]
        carry[...] = da1[0:HALO, :]
        _shifted_copies(dext, dx)
        dw_acc[...] = jnp.zeros_like(dw_acc)

        def conv_body(rb, c):
            r0 = pl.multiple_of(rb * EW_ROWS, EW_ROWS)
            rows = pl.ds(r0, EW_ROWS)
            uv = uv_ref[rows, :]
            sg = _sigmoid(ug_ref[rows, :])
            a0 = _round_bf16(uv * sg)
            halves = []
            for k in range(EW_ROWS // SUBLANES):
                a0_k = a0[k * SUBLANES:(k + 1) * SUBLANES, :]
                acc = jnp.zeros((SUBLANES, D), F32)
                for j in range(CONV_W):
                    a, s = divmod(CONV_W - 1 - j, SUBLANES)
                    win = dx[s, pl.ds(r0 + (a + k) * SUBLANES, SUBLANES), :]
                    acc = acc + w_ref[j] * win
                    dw_acc[j] += a0_k * win
                halves.append(acc)
            da0 = jnp.concatenate(halves, axis=0)
            dc_ref[rows, 0:D] = (da0 * sg).astype(BF16)
            dc_ref[rows, D:2 * D] = (da0 * uv * sg * (1.0 - sg)).astype(BF16)
            return c

        lax.fori_loop(0, tm // EW_ROWS, conv_body, 0)
        small_ref[0:CONV_W + 1, :] += jnp.sum(dw_acc[...], axis=1)

    vec = pl.BlockSpec((1, D), lambda i: (0, 0))
    tile = lambda col: pl.BlockSpec((tm, D), lambda i: (nt - 1 - i, col))
    return pl.pallas_call(
        body, name="conv_bwd", grid=(nt,),
        in_specs=[tile(0), tile(COL_UVAL), tile(COL_UGATE), tile(COL_ZCONV), tile(0),
                  pl.BlockSpec((CONV_W + 1, SUBLANES, D), lambda i: (0, 0, 0)), vec, vec],
        out_specs=[pl.BlockSpec((tm, 3 * D), lambda i: (nt - 1 - i, 0)),
                   pl.BlockSpec((SMALL_B_ROWS, D), lambda i: (0, 0))],
        out_shape=[jax.ShapeDtypeStruct((S, 3 * D), BF16), jax.ShapeDtypeStruct((SMALL_B_ROWS, D), F32)],
        scratch_shapes=[pltpu.VMEM((tm + HALO + SUBLANES, D), F32), pltpu.VMEM((SUBLANES, tm + HALO, D), F32),
                        pltpu.VMEM((HALO, D), F32), pltpu.VMEM((CONV_W + 1, SUBLANES, D), F32)],
        compiler_params=_params(),
    )(dp_b, proj, proj, proj, a1, conv_w, ln_g, ln_b)


RET_W = 2 * DK + 2 * DV


def _retention_bwd(dp_a, o, proj, states, cos, sin, tables, cpb):
    S = proj.shape[0]
    tb = cpb * CHUNK
    nb = S // tb
    mask, xi, zeta, g_chunk = tables

    def body(dpa_ref, o_ref, q_ref, k_ref, v_ref, z_ref, st_ref, cos_ref, sin_ref, mask_ref, xi_ref, zeta_ref,
             gc_ref, out_ref, dstate):
        @pl.when(pl.program_id(1) == 0)
        def _():
            dstate[...] = jnp.zeros_like(dstate)

        m, xi_c, zeta_c, gc = mask_ref[...], xi_ref[...], zeta_ref[...], gc_ref[0:1, 0:1]
        for ci in reversed(range(cpb)):
            rows = pl.ds(ci * CHUNK, CHUNK)
            o = o_ref[rows, :]
            oc = o - jnp.mean(o, axis=-1, keepdims=True)
            rstd = lax.rsqrt(jnp.mean(oc * oc, axis=-1, keepdims=True) + GN_EPS)
            rhat = oc * rstd
            z = z_ref[rows, :]
            sz = _sigmoid(z)
            dpa = dpa_ref[rows, :]
            out_ref[rows, 2 * DK + DV:RET_W] = (dpa * rhat * _dsilu(z, sz)).astype(BF16)
            dret = dpa * (z * sz)
            d_o = rstd * (dret - jnp.mean(dret, axis=-1, keepdims=True)
                          - rhat * jnp.mean(dret * rhat, axis=-1, keepdims=True))
            cs, sn = cos_ref[rows, :], sin_ref[rows, :]
            qb = _rot(q_ref[rows, :], cs, sn).astype(BF16)
            kr = _rot(k_ref[rows, :], cs, sn) * (DK ** -0.5)
            kb = kr.astype(BF16)
            kz = (kr * zeta_c).astype(BF16)
            vb = v_ref[rows, :].astype(BF16)
            scores = (_dot_nt(qb, kb) * m).astype(BF16)
            dob = d_o.astype(BF16)
            dox = (d_o * xi_c).astype(BF16)
            dscores = (_dot_nt(dob, vb) * m).astype(BF16)
            dst = dstate[...]
            dstb = dst.astype(BF16)
            dqr = _dot(dscores, kb) + _dot_nt(dox, st_ref[ci])
            dkr = _dot_tn(dscores, qb) + _dot_nt(vb, dstb) * zeta_c
            dv = _dot_tn(scores, dob) + _dot(kz, dstb)
            dstate[...] = dst * gc + _dot_tn(qb, dox)
            out_ref[rows, 0:DK] = _rot_back(dqr, cs, sn).astype(BF16)
            out_ref[rows, DK:2 * DK] = (_rot_back(dkr, cs, sn) * (DK ** -0.5)).astype(BF16)
            out_ref[rows, 2 * DK:2 * DK + DV] = dv.astype(BF16)

    tab = lambda shape: pl.BlockSpec((None,) + shape, lambda h, n: (h, 0, 0))
    blk = lambda w, col0: pl.BlockSpec((tb, w), lambda h, n: (nb - 1 - n, col0 + h))
    return pl.pallas_call(
        body, name="retention_bwd", grid=(HEADS, nb),
        in_specs=[blk(DV, 0), blk(DV, 0), blk(DK, 0), blk(DK, HEADS), blk(DV, COL_V * 2), blk(DV, COL_Z * 2),
                  pl.BlockSpec((None, cpb, DK, DV), lambda h, n: (h, nb - 1 - n, 0, 0)),
                  pl.BlockSpec((tb, DK // 2), lambda h, n: (nb - 1 - n, 0)),
                  pl.BlockSpec((tb, DK // 2), lambda h, n: (nb - 1 - n, 0)),
                  tab((CHUNK, CHUNK)), tab((CHUNK, 1)), tab((CHUNK, 1)), tab((1, 128))],
        out_specs=blk(RET_W, 0),
        out_shape=jax.ShapeDtypeStruct((S, HEADS * RET_W), BF16),
        scratch_shapes=[pltpu.VMEM((DK, DV), F32)],
        compiler_params=_params(),
    )(dp_a, o, proj, proj, proj, proj, states, cos, sin, mask, xi, zeta, g_chunk)


def _dw_in_retention(h, d_r, tk):
    S = h.shape[0]

    def body(a_ref, b_ref, q_ref, k_ref, v_ref, z_ref):
        @pl.when(pl.program_id(1) == 0)
        def _():
            for ref in (q_ref, k_ref, v_ref, z_ref):
                ref[...] = jnp.zeros_like(ref)

        p = _dot_tn(a_ref[...], b_ref[...])
        q_ref[...] += p[:, 0:DK]
        k_ref[...] += p[:, DK:2 * DK]
        v_ref[...] += p[:, 2 * DK:2 * DK + DV]
        z_ref[...] += p[:, 2 * DK + DV:RET_W]

    col = lambda w: pl.BlockSpec((D, w), lambda j, k: (0, j))
    out = lambda w: jax.ShapeDtypeStruct((D, HEADS * w), F32)
    return pl.pallas_call(
        body, name="dw_in_ret", grid=(HEADS, S // tk),
        in_specs=[pl.BlockSpec((tk, D), lambda j, k: (k, 0)), pl.BlockSpec((tk, RET_W), lambda j, k: (k, j))],
        out_specs=[col(DK), col(DK), col(DV), col(DV)], out_shape=[out(DK), out(DK), out(DV), out(DV)],
        compiler_params=_params(),
    )(h, d_r)


def _dh(d_r, d_c, d_g, w_in, x, dout, g_pre, scale, tm):
    S = x.shape[0]

    def body(dr_ref, dc_ref, dg_ref, w_ref, x_ref, dout_ref, g_ref, sc_ref, gx_ref, small_ref):
        @pl.when(pl.program_id(0) == 0)
        def _():
            small_ref[...] = jnp.zeros_like(small_ref)

        dh = _dot_nt(dc_ref[...], w_ref[:, 6 * D:9 * D]) + _dot_nt(dg_ref[...], w_ref[:, 9 * D:11 * D])
        for h in range(HEADS):
            pieces = ((0, DK, h * DK), (DK, DK, D + h * DK), (2 * DK, DV, 2 * D + h * DV), (2 * DK + DV, DV, 4 * D + h * DV))
            for start, width, col in pieces:
                dh = dh + _dot_nt(dr_ref[:, h * RET_W + start:h * RET_W + start + width], w_ref[:, col:col + width])
        xf = x_ref[...]
        r = lax.rsqrt(jnp.mean(xf * xf, axis=-1, keepdims=True) + RMS_EPS)
        xhat = xf * r
        g, sc1 = g_ref[...], 1.0 + sc_ref[...]
        small_ref[0:1, :] += jnp.sum(dh, axis=0, keepdims=True)
        small_ref[1:2, :] += jnp.sum(dh * (xhat * g), axis=0, keepdims=True)
        small_ref[2:3, :] += jnp.sum(dh * sc1 * xhat, axis=0, keepdims=True)
        dxhat = dh * sc1 * g
        gx_ref[...] = dout_ref[...] + r * (dxhat - xhat * jnp.mean(dxhat * xhat, axis=-1, keepdims=True))

    vec = pl.BlockSpec((1, D), lambda i: (0, 0))
    tile = lambda w: pl.BlockSpec((tm, w), lambda i: (i, 0))
    whole = lambda a: pl.BlockSpec(a.shape, lambda i: (0, 0), pipeline_mode=pl.Buffered(1))
    return pl.pallas_call(
        body, name="dh", grid=(S // tm,),
        in_specs=[tile(d_r.shape[1]), tile(d_c.shape[1]), tile(d_g.shape[1]), whole(w_in), tile(D), tile(D), vec, vec],
        out_specs=[tile(D), pl.BlockSpec((8, D), lambda i: (0, 0))],
        out_shape=[jax.ShapeDtypeStruct((S, D), F32), jax.ShapeDtypeStruct((8, D), F32)],
        compiler_params=_params(),
    )(d_r, d_c, d_g, w_in, x, dout, g_pre, scale)


ROW_GATE, ROW_POST, ROW_LOSS = 0, 1, 2
ROW_CONV_W, ROW_CONV_B, ROW_LN_G, ROW_LN_B = 8, 8 + 31, 8 + 32, 8 + 33
ROW_SHIFT, ROW_SCALE, ROW_PRE = 48, 49, 50


def kernel(x, c, positions, w_ada, b_ada, pre_norm_g, w_in, conv_w, conv_b, conv_ln_g, conv_ln_b, w_ret_out, w_conv_out, w_out, post_norm_g, loss_target, m_w_ada, m_b_ada, m_pre_norm_g, m_w_in, m_conv_w, m_conv_b, m_conv_ln_g, m_conv_ln_b, m_w_ret_out, m_w_conv_out, m_w_out, m_post_norm_g, v_w_ada, v_b_ada, v_pre_norm_g, v_w_in, v_conv_w, v_conv_b, v_conv_ln_g, v_conv_ln_b, v_w_ret_out, v_w_conv_out, v_w_out, v_post_norm_g):
    S = x.shape[1]
    me = 4 * lax.axis_index("x") + 2 * lax.axis_index("y") + lax.axis_index("c")
    x2, target = x[0], loss_target[0]
    ada_w = w_ada.shape[2]
    cw_w = conv_w.shape[2]

    c_g, conv_w_g = _exchange([jnp.pad(c, ((0, 7), (0, 0))), jnp.pad(conv_w[0], ((0, 1), (0, 0)))],
                              False, "gather_c_conv_w", True)
    c_all = c_g[:, 0, :]
    conv_w_full = _round_bf16(conv_w_g.transpose(1, 0, 2).reshape(CONV_W + 1, D))
    conv_w_full = jnp.broadcast_to(conv_w_full[:, None, :], (CONV_W + 1, SUBLANES, D))
    b_part = lax.dynamic_slice(b_ada, (0, me * ada_w), (1, ada_w))
    mod_g = _exchange([_mod_part(c_all, w_ada[0], b_part)], False, "gather_mod", True)[0]

    own_in, mod_g = lax.optimization_barrier((w_in[0].astype(BF16), mod_g))
    in_send, in_recv, in_sent, in_zones, tok_in = _exchange_start([own_in], False, "gather_w_in_start")
    mod = lax.dynamic_index_in_dim(mod_g, me, axis=1, keepdims=False).reshape(1, 3 * D)
    shift, scale, gate = mod[:, :D], mod[:, D:2 * D], mod[:, 2 * D:]
    positions = positions + tok_in[0:1, 0:1].astype(jnp.int32)

    cos, sin = _rope_tables(positions, S)
    tables = _decay_tables()
    proj, h = _inproj_fwd(x2, pre_norm_g, scale, shift, in_zones[0], in_recv, in_sent[0], tm=min(S, 1024))
    g_in, proj = lax.optimization_barrier((in_zones[0], proj))
    _, tok_sent = _exchange_wait(in_send, in_recv, in_sent, in_zones, h, False, "gather_w_in_sends_wait", n_waited=1)
    *rest_handles, tok_rest = _exchange_start(
        [(w_ret_out[0] + tok_sent[0:1, 0:1]).astype(BF16), w_conv_out[0].astype(BF16), w_out[0].astype(BF16)],
        False, "gather_w_rest_start")
    w_in_full = g_in.transpose(1, 0, 2).reshape(D, IN_W)
    p_a, o, states = _retention_fwd(proj, cos, sin, tables, cpb=8)
    p_b, a1 = _conv_fwd(proj, conv_w_full, conv_b + tok_rest[0:1, 0:1], conv_ln_g, conv_ln_b, tm=256)
    (g_ret, g_conv, g_out), _ = _exchange_wait(*rest_handles, p_b, False, "gather_w_rest_wait")
    w_ret_full, w_conv_full, w_out_full = g_ret.reshape(2 * D, D), g_conv.reshape(D, D), g_out.reshape(D, D)

    dout, d_g, dp_a, dp_b, merged, dy, dy_a, dy_b, small_a = _head(
        p_a, p_b, proj, x2, target, gate, post_norm_g, w_ret_full, w_conv_full, w_out_full, tm=256)
    parts_rest = [_matmul_tn(p_a, dy_a, "dw_ret_out", bn=D, tk=min(S, 2048)).astype(BF16).reshape(N_DEV, 2 * D // N_DEV, D),
                  _matmul_tn(p_b, dy_b, "dw_conv_out", bn=D, tk=min(S, 2048)).astype(BF16).reshape(N_DEV, D // N_DEV, D),
                  _matmul_tn(merged, dy, "dw_out", bn=D, tk=min(S, 2048)).astype(BF16).reshape(N_DEV, D // N_DEV, D)]
    *s_rest_handles, tok_s_rest = _exchange_start(parts_rest, True, "scatter_rest_start")
    d_c, small_b = _conv_bwd(dp_b, proj, a1, conv_w_full, conv_ln_g + tok_s_rest[0:1, 0:1], conv_ln_b, tm=256)
    d_r = _retention_bwd(dp_a, o, proj, states, cos, sin, tables, cpb=8)
    dw_r = _dw_in_retention(h, d_r, tk=min(S, 2048))
    dw_c = _matmul_tn(h, d_c, "dw_in_conv", bn=D, tk=min(S, 2048))
    dw_g = _matmul_tn(h, d_g, "dw_in_gate", bn=D, tk=min(S, 2048))
    dw_in = jnp.concatenate([*dw_r, dw_c, dw_g], axis=1)
    parts_in = dw_in.astype(BF16).reshape(D, N_DEV, W_SHARD).transpose(1, 0, 2)
    *s_in_handles, tok_s_in = _exchange_start([parts_in], True, "scatter_w_in_start")
    grad_x, small_c = _dh(d_r, d_c, d_g, w_in_full, x2, dout, pre_norm_g + tok_s_in[0:1, 0:1], scale, tm=256)

    (r_ret, r_conv, r_out), _ = _exchange_wait(*s_rest_handles, small_c, True, "scatter_rest_wait")
    (r_in,), _ = _exchange_wait(*s_in_handles, small_c, True, "scatter_w_in_wait")
    big = {"w_in": _adam_shard(r_in, w_in[0], m_w_in[0], v_w_in[0], "adam_w_in", 128),
           "w_ret_out": _adam_shard(r_ret, w_ret_out[0], m_w_ret_out[0], v_w_ret_out[0], "adam_w_ret_out", 128),
           "w_conv_out": _adam_shard(r_conv, w_conv_out[0], m_w_conv_out[0], v_w_conv_out[0], "adam_w_conv_out", 128),
           "w_out": _adam_shard(r_out, w_out[0], m_w_out[0], v_w_out[0], "adam_w_out", 128)}

    sm_g = _exchange([jnp.concatenate([small_a, small_b, small_c], axis=0)], False, "gather_small", True)[0]
    sm = _sum_devices(sm_g)
    row = lambda r: sm[r:r + 1]
    g8 = jnp.concatenate([row(ROW_SHIFT), row(ROW_SCALE), row(ROW_GATE), row(ROW_PRE), row(ROW_CONV_B),
                          row(ROW_LN_G), row(ROW_LN_B), row(ROW_POST)], axis=0)
    stack8 = lambda ba, pre, cb, lg, lb, post: jnp.concatenate([ba.reshape(3, D), pre, cb, lg, lb, post], axis=0)
    d8, m8, v8 = _adam_small(
        g8, stack8(b_ada, pre_norm_g, conv_b, conv_ln_g, conv_ln_b, post_norm_g),
        stack8(m_b_ada, m_pre_norm_g, m_conv_b, m_conv_ln_g, m_conv_ln_b, m_post_norm_g),
        stack8(v_b_ada, v_pre_norm_g, v_conv_b, v_conv_ln_g, v_conv_ln_b, v_post_norm_g), "adam_small")
    unstack8 = lambda a: {"b_ada": a[0:3].reshape(3 * D), "pre_norm_g": a[3], "conv_b": a[4],
                          "conv_ln_g": a[5], "conv_ln_b": a[6], "post_norm_g": a[7]}
    g_small, d_small, m_small, v_small = unstack8(g8), unstack8(d8), unstack8(m8), unstack8(v8)

    pad_row = lambda a: jnp.pad(a[0], ((0, 1), (0, 0)))
    g_cw = lax.dynamic_slice(sm[ROW_CONV_W:ROW_CONV_W + CONV_W + 1], (0, me * cw_w), (CONV_W + 1, cw_w))
    d_cw, m_cw, v_cw = _adam_small(g_cw, pad_row(conv_w), pad_row(m_conv_w), pad_row(v_conv_w), "adam_conv_w")

    dmod_all = jnp.concatenate([sm_g[:, ROW_SHIFT], sm_g[:, ROW_SCALE], sm_g[:, ROW_GATE]], axis=1)
    g_wa = _grad_w_ada(c_all.T, lax.dynamic_slice(dmod_all, (0, me * ada_w), (N_DEV, ada_w)))
    d_wa, m_wa, v_wa = _adam_small(g_wa, w_ada[0], m_w_ada[0], v_w_ada[0], "adam_w_ada")

    grads = {"w_ada": g_wa, "conv_w": g_cw[:CONV_W], **g_small, **{n: t[0] for n, t in big.items()}}
    deltas = {"w_ada": d_wa, "conv_w": d_cw[:CONV_W], **d_small, **{n: t[1] for n, t in big.items()}}
    new_m = {"w_ada": m_wa, "conv_w": m_cw[:CONV_W], **m_small, **{n: t[2] for n, t in big.items()}}
    new_v = {"w_ada": v_wa, "conv_w": v_cw[:CONV_W], **v_small, **{n: t[3] for n, t in big.items()}}
    order = ["w_ada", "b_ada", "pre_norm_g", "w_in", "conv_w", "conv_b", "conv_ln_g", "conv_ln_b", "w_ret_out",
             "w_conv_out", "w_out", "post_norm_g"]
    loss = sm[ROW_LOSS, 0]
    out = [loss, grad_x[None]]
    for group in (grads, deltas, new_m, new_v):
        out += [group[n][None] for n in order]
    return tuple(out)
```

```python
import jax
import jax.numpy as jnp
import numpy as np
from jax import lax
from jax.experimental import pallas as pl
from jax.experimental.pallas import tpu as pltpu

F32 = jnp.float32
BF16 = jnp.bfloat16
MESH = pl.DeviceIdType.MESH

N_DEV = 8
D = 1024
HEADS = 4
DK = 256
DV = 512
CHUNK = 128
CONV_W = 31
HALO = 32
IN_W = 11264
W_SHARD = IN_W // N_DEV
ROPE_BASE = 10000.0
GN_EPS = 1e-5
LN_EPS = 1e-5
RMS_EPS = 1e-6
ADAM_LR, ADAM_B1, ADAM_B2, ADAM_EPS, ADAM_WD, ADAM_STEP = 0.001, 0.9, 0.999, 1e-08, 0.01, 10

COL_V, COL_Z, COL_UVAL, COL_UGATE, COL_ZCONV, COL_GA, COL_GB = 2, 4, 6, 7, 8, 9, 10

VMEM_LIMIT = 56 * 1024 * 1024


def _params(**kw):
    return pltpu.CompilerParams(vmem_limit_bytes=VMEM_LIMIT, **kw)


def _sigmoid(z):
    return jax.nn.sigmoid(z)


def _round_bf16(a):
    return a.astype(BF16).astype(F32)


def _dot(a, b):
    return jnp.dot(a, b, preferred_element_type=F32)


def _dot_nt(a, b):
    return lax.dot_general(a, b, (((1,), (1,)), ((), ())), preferred_element_type=F32)


def _dot_tn(a, b):
    return lax.dot_general(a, b, (((0,), (0,)), ((), ())), preferred_element_type=F32)


def _my_place():
    return lax.axis_index("x"), lax.axis_index("y"), lax.axis_index("c")


def _peer(k):
    x, y, c = _my_place()
    px = lax.rem(x + ((k >> 2) & 1), 2)
    py = lax.rem(y + ((k >> 1) & 1), 2)
    pc = lax.rem(c + (k & 1), 2)
    return (px, py, pc), 4 * px + 2 * py + pc


def _exchange(arrs, scatter, name, in_vmem):
    n = len(arrs)
    flags = tuple(scatter) if isinstance(scatter, (tuple, list)) else (scatter,) * n

    def body(*refs):
        ins, outs = refs[:n], refs[n:2 * n]
        send_sems, recv_sems, local_sems = refs[2 * n:]
        x, y, c = _my_place()
        me = 4 * x + 2 * y + c
        copies = []
        for i in range(n):
            src = ins[i].at[me] if flags[i] else ins[i]
            cp = pltpu.make_async_copy(src, outs[i].at[me], local_sems.at[i])
            cp.start()
            copies.append(cp)
        for k in range(1, N_DEV):
            peer, peer_idx = _peer(k)
            for i in range(n):
                src = ins[i].at[peer_idx] if flags[i] else ins[i]
                cp = pltpu.make_async_remote_copy(
                    src_ref=src, dst_ref=outs[i].at[me], send_sem=send_sems.at[i, k - 1],
                    recv_sem=recv_sems.at[i, k - 1], device_id=peer, device_id_type=MESH)
                cp.start()
                copies.append(cp)
        for cp in copies:
            cp.wait()

    space = pltpu.VMEM if in_vmem else pl.ANY
    out_shape = [jax.ShapeDtypeStruct(a.shape if f else (N_DEV,) + a.shape, a.dtype) for a, f in zip(arrs, flags)]
    return pl.pallas_call(
        body, name=name, out_shape=out_shape,
        in_specs=[pl.BlockSpec(memory_space=space)] * n,
        out_specs=[pl.BlockSpec(memory_space=space)] * n,
        scratch_shapes=[pltpu.SemaphoreType.DMA((n, N_DEV - 1)), pltpu.SemaphoreType.DMA((n, N_DEV - 1)),
                        pltpu.SemaphoreType.DMA((n,))],
        compiler_params=_params(has_side_effects=True),
    )(*arrs)


def _remote_copies(ins, lands, send_sems, recv_sems, scatter, first=0):
    x, y, c = _my_place()
    me = 4 * x + 2 * y + c
    copies = []
    for i in range(len(ins)):
        for k in range(1, N_DEV):
            peer, peer_idx = _peer(k)
            sem = (N_DEV - 1) * (first + i) + k - 1
            copies.append(pltpu.make_async_remote_copy(
                src_ref=ins[i].at[peer_idx] if scatter else ins[i], dst_ref=lands[i].at[me],
                send_sem=send_sems.at[sem], recv_sem=recv_sems.at[sem], device_id=peer, device_id_type=MESH))
    return copies


_HBM = pl.BlockSpec(memory_space=pltpu.HBM)
_SEM = pl.BlockSpec(memory_space=pltpu.SEMAPHORE)


def _exchange_start(arrs, scatter, name):
    n = len(arrs)
    me = 4 * lax.axis_index("x") + 2 * lax.axis_index("y") + lax.axis_index("c")
    lands = []
    for a in arrs:
        own = lax.dynamic_index_in_dim(a, me, axis=0) if scatter else a[None]
        zone = lax.empty(a.shape if scatter else (N_DEV,) + a.shape, a.dtype)
        lands.append(lax.dynamic_update_slice(zone, own, (me,) + (0,) * (own.ndim - 1)))

    def body(*refs):
        ins, land_refs, send_sems, recv_sems, token = refs[:n], refs[n:2 * n], refs[2 * n], refs[2 * n + 1], refs[-1]
        for cp in _remote_copies(ins, land_refs, send_sems, recv_sems, scatter):
            cp.start()
        token[...] = jnp.zeros_like(token)

    n_sem = (N_DEV - 1) * n
    hbm = lambda a: pltpu.HBM(a.shape, a.dtype)
    outs = pl.pallas_call(
        body, name=name,
        out_shape=(pltpu.SemaphoreType.DMA((n_sem,)), pltpu.SemaphoreType.DMA((n_sem,)), *[hbm(a) for a in arrs],
                   *[hbm(a) for a in lands], jax.ShapeDtypeStruct((8, 128), F32)),
        in_specs=(_HBM,) * (2 * n), out_specs=(_SEM, _SEM) + (_HBM,) * (2 * n) + (pl.BlockSpec(memory_space=pltpu.VMEM),),
        input_output_aliases={i: 2 + i for i in range(2 * n)},
        compiler_params=pltpu.CompilerParams(has_side_effects=pltpu.SideEffectType.DATAFLOW_SIDE_EFFECTING),
    )(*[pltpu.with_memory_space_constraint(a, pltpu.HBM) for a in list(arrs) + lands])
    return outs[0], outs[1], list(outs[2:2 + n]), list(outs[2 + n:2 + 2 * n]), outs[-1]


def _exchange_wait(send_sems, recv_sems, sent, zones, after, scatter, name, n_waited=0):
    n = len(sent)
    m = n - n_waited

    def body(*refs):
        early, ins, land_refs = refs[:n_waited], refs[n_waited:n], refs[n:n + m]
        s_sems, r_sems, token = refs[n + m], refs[n + m + 1], refs[-1]
        token[...] = jnp.zeros_like(token)
        for i in range(n_waited):
            for k in range(1, N_DEV):
                pltpu.make_async_remote_copy(
                    src_ref=early[i], dst_ref=early[i], send_sem=s_sems.at[(N_DEV - 1) * i + k - 1],
                    recv_sem=s_sems.at[(N_DEV - 1) * i + k - 1], device_id=_peer(k)[0], device_id_type=MESH).wait_send()
        for cp in _remote_copies(ins, land_refs, s_sems, r_sems, scatter, first=n_waited):
            cp.wait_send()
            cp.wait_recv()

    thru = list(sent[n_waited:]) + list(zones[n_waited:])
    outs = pl.pallas_call(
        body, name=name,
        out_shape=tuple(pltpu.HBM(a.shape, a.dtype) for a in thru) + (jax.ShapeDtypeStruct((8, 128), F32),),
        in_specs=(_HBM,) * (n_waited + 2 * m) + (_SEM, _SEM, pl.BlockSpec(memory_space=pl.ANY)),
        out_specs=(_HBM,) * (2 * m) + (pl.BlockSpec(memory_space=pltpu.VMEM),),
        input_output_aliases={n_waited + i: i for i in range(2 * m)},
        compiler_params=pltpu.CompilerParams(has_side_effects=pltpu.SideEffectType.DATAFLOW_SIDE_EFFECTING),
    )(*sent[:n_waited], *thru, send_sems, recv_sems, after)
    return list(outs[m:2 * m]), outs[-1]


def _rope_tables(positions, S):
    tm = min(S, 1024)

    half = DK // 2
    inv_freq = (ROPE_BASE ** (-jnp.arange(half, dtype=F32) / half)).reshape(1, half)

    def body(pos_ref, f_ref, cos_ref, sin_ref):
        ang = pos_ref[...].astype(F32) * f_ref[...]
        cos_ref[...] = jnp.cos(ang)
        sin_ref[...] = jnp.sin(ang)

    return pl.pallas_call(
        body, name="rope_tables", grid=(S // tm,),
        in_specs=[pl.BlockSpec((tm, 1), lambda i: (i, 0)), pl.BlockSpec((1, half), lambda i: (0, 0))],
        out_specs=[pl.BlockSpec((tm, half), lambda i: (i, 0))] * 2,
        out_shape=[jax.ShapeDtypeStruct((S, half), F32)] * 2,
    )(positions.reshape(S, 1), inv_freq)


def _mod_part(c_all, w_ada, b_part):
    def body(c_ref, w_ref, b_ref, o_ref):
        o_ref[...] = _dot(c_ref[...].astype(BF16), w_ref[...].astype(BF16)) + b_ref[...]

    return pl.pallas_call(body, name="mod_part",
                          out_shape=jax.ShapeDtypeStruct((N_DEV, w_ada.shape[1]), F32))(c_all, w_ada, b_part)


def _sum_devices(g, name):
    def body(g_ref, o_ref):
        acc = g_ref[0]
        for d in range(1, N_DEV):
            acc = acc + g_ref[d]
        o_ref[...] = acc

    return pl.pallas_call(body, name=name, out_shape=jax.ShapeDtypeStruct(g.shape[1:], F32))(g)


def _grad_w_ada(c_t, dmod):
    def body(c_ref, d_ref, o_ref):
        acc = c_ref[:, 0:1] * d_ref[0:1, :]
        for b in range(1, N_DEV):
            acc = acc + c_ref[:, b:b + 1] * d_ref[b:b + 1, :]
        o_ref[...] = acc

    return pl.pallas_call(body, name="grad_w_ada",
                          out_shape=jax.ShapeDtypeStruct((c_t.shape[0], dmod.shape[1]), F32))(c_t, dmod)


def _adam_math(w, g, m, v):
    m = ADAM_B1 * m + (1.0 - ADAM_B1) * g
    v = ADAM_B2 * v + (1.0 - ADAM_B2) * (g * g)
    m_hat = m / (1.0 - ADAM_B1 ** ADAM_STEP)
    v_hat = v / (1.0 - ADAM_B2 ** ADAM_STEP)
    delta = -ADAM_LR * (m_hat / (jnp.sqrt(v_hat) + ADAM_EPS) + ADAM_WD * w)
    return delta, m, v


def _adam_small(g, w, m, v, name):
    def body(g_ref, w_ref, m_ref, v_ref, d_ref, nm_ref, nv_ref):
        d_ref[...], nm_ref[...], nv_ref[...] = _adam_math(w_ref[...], g_ref[...], m_ref[...], v_ref[...])

    return pl.pallas_call(body, name=name, out_shape=[jax.ShapeDtypeStruct(w.shape, F32)] * 3)(g, w, m, v)


def _adam_shard(parts, w, m, v, name, tr):
    R, L = w.shape

    def body(p_ref, w_ref, m_ref, v_ref, g_ref, d_ref, nm_ref, nv_ref):
        g = p_ref[0].astype(F32)
        for d in range(1, N_DEV):
            g = g + p_ref[d].astype(F32)
        g_ref[...] = g
        d_ref[...], nm_ref[...], nv_ref[...] = _adam_math(w_ref[...], g, m_ref[...], v_ref[...])

    blk = pl.BlockSpec((tr, L), lambda i: (i, 0))
    return pl.pallas_call(
        body, name=name, grid=(R // tr,),
        in_specs=[pl.BlockSpec((N_DEV, tr, L), lambda i: (0, i, 0)), blk, blk, blk],
        out_specs=[blk] * 4, out_shape=[jax.ShapeDtypeStruct((R, L), F32)] * 4,
        compiler_params=_params(),
    )(parts, w, m, v)


ARRIVAL = (0, 1, 2, 4, 3, 5, 6, 7)
LEAD = 2


def _inproj_fwd(x, g_pre, scale, shift, zone, recv_sems, sent, tm):
    S = x.shape[0]
    nt = S // tm
    me = 4 * lax.axis_index("x") + 2 * lax.axis_index("y") + lax.axis_index("c")
    order = jnp.bitwise_xor(me, jnp.asarray(ARRIVAL, jnp.int32))

    def body(order_ref, x_ref, g_ref, sc_ref, sh_ref, w_ref, sent_ref, sem_ref, proj_ref, h_ref, h_all):
        j, i = pl.program_id(0), pl.program_id(1)
        for row in range(1, N_DEV):
            @pl.when((j == row) & (i == 0))
            def _():
                k = ARRIVAL[row]
                pltpu.make_async_remote_copy(
                    src_ref=sent_ref, dst_ref=sent_ref, send_sem=sem_ref.at[k - 1],
                    recv_sem=sem_ref.at[k - 1], device_id=_peer(k)[0], device_id_type=MESH).wait_recv()

        rows = pl.ds(pl.multiple_of(jnp.maximum(i - LEAD, 0) * tm, tm), tm)

        @pl.when((j == 0) & (i >= LEAD))
        def _():
            xf = x_ref[...]
            r = lax.rsqrt(jnp.mean(xf * xf, axis=-1, keepdims=True) + RMS_EPS)
            h = ((xf * r * g_ref[...]) * (1.0 + sc_ref[...]) + sh_ref[...]).astype(BF16)
            h_all[rows, :] = h
            h_ref[...] = h

        @pl.when(i >= LEAD)
        def _():
            proj_ref[...] = _dot(h_all[rows, :], w_ref[...])

    tile = lambda j, i: jnp.maximum(i - LEAD, 0)
    first_pass = lambda j, i: jnp.where(j == 0, tile(j, i), nt - 1)
    vec = pl.BlockSpec((1, D), lambda j, i, o: (0, 0))
    any_space = pl.BlockSpec(memory_space=pl.ANY)
    return pl.pallas_call(
        body, name="inproj_fwd",
        grid_spec=pltpu.PrefetchScalarGridSpec(
            num_scalar_prefetch=1, grid=(N_DEV, nt + LEAD),
            in_specs=[pl.BlockSpec((tm, D), lambda j, i, o: (first_pass(j, i), 0)), vec, vec, vec,
                      pl.BlockSpec((None, D, W_SHARD),
                                   lambda j, i, o: (o[jnp.where(i >= LEAD, j, jnp.maximum(j - 1, 0))], 0, 0)),
                      any_space, _SEM],
            out_specs=[pl.BlockSpec((tm, W_SHARD), lambda j, i, o: (tile(j, i), o[j])),
                       pl.BlockSpec((tm, D), lambda j, i, o: (first_pass(j, i), 0))],
            scratch_shapes=[pltpu.VMEM((S, D), BF16)]),
        out_shape=[jax.ShapeDtypeStruct((S, IN_W), F32), jax.ShapeDtypeStruct((S, D), BF16)],
        compiler_params=_params(has_side_effects=pltpu.SideEffectType.DATAFLOW_SIDE_EFFECTING),
    )(order, x, g_pre, scale, shift, zone, sent, recv_sems)


def _decay_tables():
    log_g = jnp.log1p(-jnp.exp2(-5.0 - jnp.arange(HEADS, dtype=F32)))
    idx = jnp.arange(CHUNK, dtype=F32)
    diff = idx[:, None] - idx[None, :]
    causal = diff >= 0
    mask = jnp.where(causal, jnp.exp(log_g[:, None, None] * jnp.where(causal, diff, 0.0)), 0.0)
    xi = jnp.exp(log_g[:, None] * (idx + 1.0))[:, :, None]
    zeta = jnp.exp(log_g[:, None] * (CHUNK - 1.0 - idx))[:, :, None]
    g_chunk = jnp.broadcast_to(jnp.exp(log_g * CHUNK)[:, None, None], (HEADS, 1, 128))
    return mask, xi, zeta, g_chunk


def _rot(t, cos, sin):
    t1, t2 = t[:, :DK // 2], t[:, DK // 2:]
    return jnp.concatenate([t1 * cos - t2 * sin, t1 * sin + t2 * cos], axis=-1)


def _rot_back(t, cos, sin):
    t1, t2 = t[:, :DK // 2], t[:, DK // 2:]
    return jnp.concatenate([t1 * cos + t2 * sin, t2 * cos - t1 * sin], axis=-1)


def _silu(z):
    return z * _sigmoid(z)


def _retention_fwd(proj, cos, sin, tables, cpb):
    S = proj.shape[0]
    tb = cpb * CHUNK
    mask, xi, zeta, g_chunk = tables

    def body(q_ref, k_ref, v_ref, z_ref, cos_ref, sin_ref, mask_ref, xi_ref, zeta_ref, gc_ref,
             pa_ref, o_ref, st_ref, state):
        @pl.when(pl.program_id(1) == 0)
        def _():
            state[...] = jnp.zeros_like(state)

        m, xi_c, zeta_c, gc = mask_ref[...], xi_ref[...], zeta_ref[...], gc_ref[0:1, 0:1]
        for ci in range(cpb):
            rows = pl.ds(ci * CHUNK, CHUNK)
            cs, sn = cos_ref[rows, :], sin_ref[rows, :]
            qr = _rot(q_ref[rows, :], cs, sn)
            kr = _rot(k_ref[rows, :], cs, sn) * (DK ** -0.5)
            vb = v_ref[rows, :].astype(BF16)
            qb = qr.astype(BF16)
            st = state[...]
            stb = st.astype(BF16)
            st_ref[ci] = stb
            scores = (_dot_nt(qb, kr.astype(BF16)) * m).astype(BF16)
            o = _dot(scores, vb) + _dot(qb, stb) * xi_c
            state[...] = st * gc + _dot_tn((kr * zeta_c).astype(BF16), vb)
            o_ref[rows, :] = o
            mu = jnp.mean(o, axis=-1, keepdims=True)
            oc = o - mu
            var = jnp.mean(oc * oc, axis=-1, keepdims=True)
            pa_ref[rows, :] = (oc * lax.rsqrt(var + GN_EPS) * _silu(z_ref[rows, :])).astype(BF16)

    tab = lambda shape: pl.BlockSpec((None,) + shape, lambda h, n: (h, 0, 0))
    return pl.pallas_call(
        body, name="retention_fwd", grid=(HEADS, S // tb),
        in_specs=[pl.BlockSpec((tb, DK), lambda h, n: (n, h)),
                  pl.BlockSpec((tb, DK), lambda h, n: (n, HEADS + h)),
                  pl.BlockSpec((tb, DV), lambda h, n: (n, COL_V * 2 + h)),
                  pl.BlockSpec((tb, DV), lambda h, n: (n, COL_Z * 2 + h)),
                  pl.BlockSpec((tb, DK // 2), lambda h, n: (n, 0)),
                  pl.BlockSpec((tb, DK // 2), lambda h, n: (n, 0)),
                  tab((CHUNK, CHUNK)), tab((CHUNK, 1)), tab((CHUNK, 1)), tab((1, 128))],
        out_specs=[pl.BlockSpec((tb, DV), lambda h, n: (n, h)),
                   pl.BlockSpec((tb, DV), lambda h, n: (n, h)),
                   pl.BlockSpec((None, cpb, DK, DV), lambda h, n: (h, n, 0, 0))],
        out_shape=[jax.ShapeDtypeStruct((S, HEADS * DV), BF16), jax.ShapeDtypeStruct((S, HEADS * DV), F32),
                   jax.ShapeDtypeStruct((HEADS, S // CHUNK, DK, DV), BF16)],
        scratch_shapes=[pltpu.VMEM((DK, DV), F32)],
        compiler_params=_params(),
    )(proj, proj, proj, proj, cos, sin, mask, xi, zeta, g_chunk)


SUBLANES = 8
FWD_ROWS = 16
EW_ROWS = 16


def _shifted_copies(src, dst):
    rows = dst.shape[1]
    src[pl.ds(rows, SUBLANES), :] = jnp.zeros((SUBLANES, src.shape[1]), src.dtype)
    for s in range(SUBLANES):
        dst[s] = src[pl.ds(s, rows), :]


def _layernorm_stats(a):
    mu = jnp.mean(a, axis=-1, keepdims=True)
    ac = a - mu
    var = jnp.mean(ac * ac, axis=-1, keepdims=True)
    rstd = lax.rsqrt(var + LN_EPS)
    return ac * rstd, rstd


def _conv_fwd(proj, conv_w, conv_b, ln_g, ln_b, tm):
    S = proj.shape[0]
    hb = tm // HALO

    def body(uv_ref, ug_ref, uvh_ref, ugh_ref, z_ref, w_ref, b_ref, g_ref, bb_ref, pb_ref, a1_ref, ext, ex):
        halo = _round_bf16(uvh_ref[...] * _sigmoid(ugh_ref[...]))
        ext[pl.ds(0, HALO), :] = jnp.where(pl.program_id(0) == 0, 0.0, halo)
        ext[pl.ds(HALO, tm), :] = _round_bf16(uv_ref[...] * _sigmoid(ug_ref[...]))
        _shifted_copies(ext, ex)
        bias, g, bb = b_ref[...], g_ref[...], bb_ref[...]

        def rows_body(rb, carry):
            r0 = pl.multiple_of(rb * FWD_ROWS, FWD_ROWS)
            groups = range(FWD_ROWS // SUBLANES)
            acc = [jnp.zeros((SUBLANES, D), F32) for _ in groups]
            for j in range(CONV_W):
                a, s = divmod(HALO - (CONV_W - 1) + j, SUBLANES)
                wj = w_ref[j]
                for k in groups:
                    acc[k] = acc[k] + wj * ex[s, pl.ds(r0 + (a + k) * SUBLANES, SUBLANES), :]
            for k in groups:
                a1_ref[pl.ds(r0 + k * SUBLANES, SUBLANES), :] = acc[k] + bias
            return carry

        lax.fori_loop(0, tm // FWD_ROWS, rows_body, 0)
        xhat, _ = _layernorm_stats(a1_ref[...])
        pb_ref[...] = (_silu(xhat * g + bb) * _silu(z_ref[...])).astype(BF16)

    vec = pl.BlockSpec((1, D), lambda i: (0, 0))
    halo = lambda col: pl.BlockSpec((HALO, D), lambda i: (jnp.maximum(i * hb - 1, 0), col))
    return pl.pallas_call(
        body, name="conv_fwd", grid=(S // tm,),
        in_specs=[pl.BlockSpec((tm, D), lambda i: (i, COL_UVAL)), pl.BlockSpec((tm, D), lambda i: (i, COL_UGATE)),
                  halo(COL_UVAL), halo(COL_UGATE), pl.BlockSpec((tm, D), lambda i: (i, COL_ZCONV)),
                  pl.BlockSpec((CONV_W + 1, SUBLANES, D), lambda i: (0, 0, 0)), vec, vec, vec],
        out_specs=[pl.BlockSpec((tm, D), lambda i: (i, 0))] * 2,
        out_shape=[jax.ShapeDtypeStruct((S, D), BF16), jax.ShapeDtypeStruct((S, D), F32)],
        scratch_shapes=[pltpu.VMEM((HALO + tm + SUBLANES, D), F32), pltpu.VMEM((SUBLANES, HALO + tm, D), F32)],
        compiler_params=_params(),
    )(proj, proj, proj, proj, proj, conv_w, conv_b, ln_g, ln_b)


def _head(p_a, p_b, proj, x, target, gate, g_post, w_ret, w_conv, w_out, tm):
    S = x.shape[0]

    def body(pa_ref, pb_ref, ga_ref, gb_ref, x_ref, t_ref, gate_ref, gp_ref, wr_ref, wc_ref, wo_ref,
             dout_ref, dg_ref, dpa_ref, dpb_ref, mb_ref, dy_ref, dya_ref, dyb_ref, small_ref):
        @pl.when(pl.program_id(0) == 0)
        def _():
            small_ref[...] = jnp.zeros_like(small_ref)

        gate, gp = gate_ref[...], gp_ref[...]
        ya = _dot(pa_ref[...], wr_ref[...])
        yb = _dot(pb_ref[...], wc_ref[...])
        sa, sb = _sigmoid(ga_ref[...]), _sigmoid(gb_ref[...])
        mb = (sa * ya + sb * yb).astype(BF16)
        mb_ref[...] = mb
        y = _dot(mb, wo_ref[...])
        r = lax.rsqrt(jnp.mean(y * y, axis=-1, keepdims=True) + RMS_EPS)
        yhat = y * r
        yn = yhat * gp
        err = (x_ref[...] + gate * yn) - t_ref[...]
        small_ref[2:3, :] += 0.5 * jnp.sum(jnp.mean(err * err, axis=-1, keepdims=True))
        dout = err * (1.0 / D)
        dout_ref[...] = dout
        small_ref[0:1, :] += jnp.sum(dout * yn, axis=0, keepdims=True)
        dyn = dout * gate
        small_ref[1:2, :] += jnp.sum(dyn * yhat, axis=0, keepdims=True)
        dyhat = dyn * gp
        dy = (r * (dyhat - yhat * jnp.mean(dyhat * yhat, axis=-1, keepdims=True))).astype(BF16)
        dy_ref[...] = dy
        dmerged = _dot_nt(dy, wo_ref[...])
        dya = dmerged * sa
        dyb = dmerged * sb
        dg_ref[:, 0:D] = (dya * ya * (1.0 - sa)).astype(BF16)
        dg_ref[:, D:2 * D] = (dyb * yb * (1.0 - sb)).astype(BF16)
        dya, dyb = dya.astype(BF16), dyb.astype(BF16)
        dya_ref[...] = dya
        dyb_ref[...] = dyb
        dpa_ref[...] = _dot_nt(dya, wr_ref[...])
        dpb_ref[...] = _dot_nt(dyb, wc_ref[...])

    vec = pl.BlockSpec((1, D), lambda i: (0, 0))
    tile = lambda w, col=0: pl.BlockSpec((tm, w), lambda i: (i, col))
    whole = lambda a: pl.BlockSpec(a.shape, lambda i: (0, 0), pipeline_mode=pl.Buffered(1))
    act = lambda w: jax.ShapeDtypeStruct((S, w), BF16)
    return pl.pallas_call(
        body, name="head", grid=(S // tm,),
        in_specs=[tile(2 * D), tile(D), tile(D, COL_GA), tile(D, COL_GB), tile(D), tile(D), vec, vec,
                  whole(w_ret), whole(w_conv), whole(w_out)],
        out_specs=[tile(D), tile(2 * D), tile(2 * D), tile(D),
                   tile(D), tile(D), tile(D), tile(D), pl.BlockSpec((8, D), lambda i: (0, 0))],
        out_shape=[jax.ShapeDtypeStruct((S, D), F32),
                   act(2 * D), jax.ShapeDtypeStruct((S, 2 * D), F32), jax.ShapeDtypeStruct((S, D), F32),
                   act(D), act(D), act(D), act(D), jax.ShapeDtypeStruct((8, D), F32)],
        compiler_params=_params(),
    )(p_a, p_b, proj, proj, x, target, gate, g_post, w_ret, w_conv, w_out)


def _matmul_tn(a, b, name, bn, tk):
    S, M = a.shape
    N = b.shape[1]

    def body(a_ref, b_ref, o_ref):
        @pl.when(pl.program_id(1) == 0)
        def _():
            o_ref[...] = jnp.zeros_like(o_ref)

        o_ref[...] += _dot_tn(a_ref[...], b_ref[...])

    return pl.pallas_call(
        body, name=name, grid=(N // bn, S // tk),
        in_specs=[pl.BlockSpec((tk, M), lambda j, k: (k, 0)), pl.BlockSpec((tk, bn), lambda j, k: (k, j))],
        out_specs=pl.BlockSpec((M, bn), lambda j, k: (0, j)),
        out_shape=jax.ShapeDtypeStruct((M, N), F32),
        compiler_params=_params(),
    )(a, b)


SMALL_B_ROWS = 40


def _dsilu(z, s):
    return s * (1.0 + z * (1.0 - s))


def _conv_bwd(dp_b, proj, a1, conv_w, ln_g, ln_b, tm):
    S = proj.shape[0]
    nt = S // tm

    def body(dpb_ref, uv_ref, ug_ref, z_ref, a1_ref, w_ref, g_ref, bb_ref,
             dc_ref, small_ref, dext, dx, carry, dw_acc):
        @pl.when(pl.program_id(0) == 0)
        def _():
            small_ref[...] = jnp.zeros_like(small_ref)
            carry[...] = jnp.zeros_like(carry)

        xhat, rstd = _layernorm_stats(a1_ref[...])
        g = g_ref[...]
        a2 = xhat * g + bb_ref[...]
        s2 = _sigmoid(a2)
        zc = z_ref[...]
        sz = _sigmoid(zc)
        dpb = dpb_ref[...]
        dc_ref[:, 2 * D:3 * D] = (dpb * (a2 * s2) * _dsilu(zc, sz)).astype(BF16)
        da2 = dpb * (zc * sz) * _dsilu(a2, s2)
        small_ref[32:33, :] += jnp.sum(da2 * xhat, axis=0, keepdims=True)
        small_ref[33:34, :] += jnp.sum(da2, axis=0, keepdims=True)
        dxhat = da2 * g
        da1 = rstd * (dxhat - jnp.mean(dxhat, axis=-1, keepdims=True)
                      - xhat * jnp.mean(dxhat * xhat, axis=-1, keepdims=True))
        small_ref[31:32, :] += jnp.sum(da1, axis=0, keepdims=True)
        da1 = _round_bf16(da1)
        dext[pl.ds(0, tm), :] = da1
        dext[pl.ds(tm, HALO), :] = carry[...]
        carry[...] = da1[0:HALO, :]
        _shifted_copies(dext, dx)
        dw_acc[...] = jnp.zeros_like(dw_acc)

        def conv_body(rb, c):
            r0 = pl.multiple_of(rb * EW_ROWS, EW_ROWS)
            rows = pl.ds(r0, EW_ROWS)
            uv = uv_ref[rows, :]
            sg = _sigmoid(ug_ref[rows, :])
            a0 = _round_bf16(uv * sg)
            halves = []
            for k in range(EW_ROWS // SUBLANES):
                a0_k = a0[k * SUBLANES:(k + 1) * SUBLANES, :]
                acc = jnp.zeros((SUBLANES, D), F32)
                for j in range(CONV_W):
                    a, s = divmod(CONV_W - 1 - j, SUBLANES)
                    win = dx[s, pl.ds(r0 + (a + k) * SUBLANES, SUBLANES), :]
                    acc = acc + w_ref[j] * win
                    dw_acc[j] += a0_k * win
                halves.append(acc)
            da0 = jnp.concatenate(halves, axis=0)
            dc_ref[rows, 0:D] = (da0 * sg).astype(BF16)
            dc_ref[rows, D:2 * D] = (da0 * uv * sg * (1.0 - sg)).astype(BF16)
            return c

        lax.fori_loop(0, tm // EW_ROWS, conv_body, 0)
        small_ref[0:CONV_W + 1, :] += jnp.sum(dw_acc[...], axis=1)

    vec = pl.BlockSpec((1, D), lambda i: (0, 0))
    tile = lambda col: pl.BlockSpec((tm, D), lambda i: (nt - 1 - i, col))
    return pl.pallas_call(
        body, name="conv_bwd", grid=(nt,),
        in_specs=[tile(0), tile(COL_UVAL), tile(COL_UGATE), tile(COL_ZCONV), tile(0),
                  pl.BlockSpec((CONV_W + 1, SUBLANES, D), lambda i: (0, 0, 0)), vec, vec],
        out_specs=[pl.BlockSpec((tm, 3 * D), lambda i: (nt - 1 - i, 0)),
                   pl.BlockSpec((SMALL_B_ROWS, D), lambda i: (0, 0))],
        out_shape=[jax.ShapeDtypeStruct((S, 3 * D), BF16), jax.ShapeDtypeStruct((SMALL_B_ROWS, D), F32)],
        scratch_shapes=[pltpu.VMEM((tm + HALO + SUBLANES, D), F32), pltpu.VMEM((SUBLANES, tm + HALO, D), F32),
                        pltpu.VMEM((HALO, D), F32), pltpu.VMEM((CONV_W + 1, SUBLANES, D), F32)],
        compiler_params=_params(),
    )(dp_b, proj, proj, proj, a1, conv_w, ln_g, ln_b)


RET_W = 2 * DK + 2 * DV


def _retention_bwd(dp_a, o, proj, states, cos, sin, tables, cpb):
    S = proj.shape[0]
    tb = cpb * CHUNK
    nb = S // tb
    mask, xi, zeta, g_chunk = tables

    def body(dpa_ref, o_ref, q_ref, k_ref, v_ref, z_ref, st_ref, cos_ref, sin_ref, mask_ref, xi_ref, zeta_ref,
             gc_ref, out_ref, dstate):
        @pl.when(pl.program_id(1) == 0)
        def _():
            dstate[...] = jnp.zeros_like(dstate)

        m, xi_c, zeta_c, gc = mask_ref[...], xi_ref[...], zeta_ref[...], gc_ref[0:1, 0:1]
        for ci in reversed(range(cpb)):
            rows = pl.ds(ci * CHUNK, CHUNK)
            o = o_ref[rows, :]
            oc = o - jnp.mean(o, axis=-1, keepdims=True)
            rstd = lax.rsqrt(jnp.mean(oc * oc, axis=-1, keepdims=True) + GN_EPS)
            rhat = oc * rstd
            z = z_ref[rows, :]
            sz = _sigmoid(z)
            dpa = dpa_ref[rows, :]
            out_ref[rows, 2 * DK + DV:RET_W] = (dpa * rhat * _dsilu(z, sz)).astype(BF16)
            dret = dpa * (z * sz)
            d_o = rstd * (dret - jnp.mean(dret, axis=-1, keepdims=True)
                          - rhat * jnp.mean(dret * rhat, axis=-1, keepdims=True))
            cs, sn = cos_ref[rows, :], sin_ref[rows, :]
            qb = _rot(q_ref[rows, :], cs, sn).astype(BF16)
            kr = _rot(k_ref[rows, :], cs, sn) * (DK ** -0.5)
            kb = kr.astype(BF16)
            kz = (kr * zeta_c).astype(BF16)
            vb = v_ref[rows, :].astype(BF16)
            scores = (_dot_nt(qb, kb) * m).astype(BF16)
            dob = d_o.astype(BF16)
            dox = (d_o * xi_c).astype(BF16)
            dscores = (_dot_nt(dob, vb) * m).astype(BF16)
            dst = dstate[...]
            dstb = dst.astype(BF16)
            dqr = _dot(dscores, kb) + _dot_nt(dox, st_ref[ci])
            dkr = _dot_tn(dscores, qb) + _dot_nt(vb, dstb) * zeta_c
            dv = _dot_tn(scores, dob) + _dot(kz, dstb)
            dstate[...] = dst * gc + _dot_tn(qb, dox)
            out_ref[rows, 0:DK] = _rot_back(dqr, cs, sn).astype(BF16)
            out_ref[rows, DK:2 * DK] = (_rot_back(dkr, cs, sn) * (DK ** -0.5)).astype(BF16)
            out_ref[rows, 2 * DK:2 * DK + DV] = dv.astype(BF16)

    tab = lambda shape: pl.BlockSpec((None,) + shape, lambda h, n: (h, 0, 0))
    blk = lambda w, col0: pl.BlockSpec((tb, w), lambda h, n: (nb - 1 - n, col0 + h))
    return pl.pallas_call(
        body, name="retention_bwd", grid=(HEADS, nb),
        in_specs=[blk(DV, 0), blk(DV, 0), blk(DK, 0), blk(DK, HEADS), blk(DV, COL_V * 2), blk(DV, COL_Z * 2),
                  pl.BlockSpec((None, cpb, DK, DV), lambda h, n: (h, nb - 1 - n, 0, 0)),
                  pl.BlockSpec((tb, DK // 2), lambda h, n: (nb - 1 - n, 0)),
                  pl.BlockSpec((tb, DK // 2), lambda h, n: (nb - 1 - n, 0)),
                  tab((CHUNK, CHUNK)), tab((CHUNK, 1)), tab((CHUNK, 1)), tab((1, 128))],
        out_specs=blk(RET_W, 0),
        out_shape=jax.ShapeDtypeStruct((S, HEADS * RET_W), BF16),
        scratch_shapes=[pltpu.VMEM((DK, DV), F32)],
        compiler_params=_params(),
    )(dp_a, o, proj, proj, proj, proj, states, cos, sin, mask, xi, zeta, g_chunk)


def _dw_in_retention(h, d_r, tk):
    S = h.shape[0]

    def body(a_ref, b_ref, q_ref, k_ref, v_ref, z_ref):
        @pl.when(pl.program_id(1) == 0)
        def _():
            for ref in (q_ref, k_ref, v_ref, z_ref):
                ref[...] = jnp.zeros_like(ref)

        p = _dot_tn(a_ref[...], b_ref[...])
        q_ref[...] += p[:, 0:DK]
        k_ref[...] += p[:, DK:2 * DK]
        v_ref[...] += p[:, 2 * DK:2 * DK + DV]
        z_ref[...] += p[:, 2 * DK + DV:RET_W]

    col = lambda w: pl.BlockSpec((D, w), lambda j, k: (0, j))
    out = lambda w: jax.ShapeDtypeStruct((D, HEADS * w), F32)
    return pl.pallas_call(
        body, name="dw_in_ret", grid=(HEADS, S // tk),
        in_specs=[pl.BlockSpec((tk, D), lambda j, k: (k, 0)), pl.BlockSpec((tk, RET_W), lambda j, k: (k, j))],
        out_specs=[col(DK), col(DK), col(DV), col(DV)], out_shape=[out(DK), out(DK), out(DV), out(DV)],
        compiler_params=_params(),
    )(h, d_r)


def _dh(d_r, d_c, d_g, w_in, x, dout, g_pre, scale, tm):
    S = x.shape[0]

    def body(dr_ref, dc_ref, dg_ref, w_ref, x_ref, dout_ref, g_ref, sc_ref, gx_ref, small_ref):
        @pl.when(pl.program_id(0) == 0)
        def _():
            small_ref[...] = jnp.zeros_like(small_ref)

        dh = _dot_nt(dc_ref[...], w_ref[:, 6 * D:9 * D]) + _dot_nt(dg_ref[...], w_ref[:, 9 * D:11 * D])
        for h in range(HEADS):
            pieces = ((0, DK, h * DK), (DK, DK, D + h * DK), (2 * DK, DV, 2 * D + h * DV), (2 * DK + DV, DV, 4 * D + h * DV))
            for start, width, col in pieces:
                dh = dh + _dot_nt(dr_ref[:, h * RET_W + start:h * RET_W + start + width], w_ref[:, col:col + width])
        xf = x_ref[...]
        r = lax.rsqrt(jnp.mean(xf * xf, axis=-1, keepdims=True) + RMS_EPS)
        xhat = xf * r
        g, sc1 = g_ref[...], 1.0 + sc_ref[...]
        small_ref[0:1, :] += jnp.sum(dh, axis=0, keepdims=True)
        small_ref[1:2, :] += jnp.sum(dh * (xhat * g), axis=0, keepdims=True)
        small_ref[2:3, :] += jnp.sum(dh * sc1 * xhat, axis=0, keepdims=True)
        dxhat = dh * sc1 * g
        gx_ref[...] = dout_ref[...] + r * (dxhat - xhat * jnp.mean(dxhat * xhat, axis=-1, keepdims=True))

    vec = pl.BlockSpec((1, D), lambda i: (0, 0))
    tile = lambda w: pl.BlockSpec((tm, w), lambda i: (i, 0))
    whole = lambda a: pl.BlockSpec(a.shape, lambda i: (0, 0), pipeline_mode=pl.Buffered(1))
    return pl.pallas_call(
        body, name="dh", grid=(S // tm,),
        in_specs=[tile(d_r.shape[1]), tile(d_c.shape[1]), tile(d_g.shape[1]), whole(w_in), tile(D), tile(D), vec, vec],
        out_specs=[tile(D), pl.BlockSpec((8, D), lambda i: (0, 0))],
        out_shape=[jax.ShapeDtypeStruct((S, D), F32), jax.ShapeDtypeStruct((8, D), F32)],
        compiler_params=_params(),
    )(d_r, d_c, d_g, w_in, x, dout, g_pre, scale)


ROW_GATE, ROW_POST, ROW_LOSS, ROW_SHIFT, ROW_SCALE, ROW_PRE, ROW_CONV_B, ROW_LN_G, ROW_LN_B = range(9)


def kernel(x, c, positions, w_ada, b_ada, pre_norm_g, w_in, conv_w, conv_b, conv_ln_g, conv_ln_b, w_ret_out, w_conv_out, w_out, post_norm_g, loss_target, m_w_ada, m_b_ada, m_pre_norm_g, m_w_in, m_conv_w, m_conv_b, m_conv_ln_g, m_conv_ln_b, m_w_ret_out, m_w_conv_out, m_w_out, m_post_norm_g, v_w_ada, v_b_ada, v_pre_norm_g, v_w_in, v_conv_w, v_conv_b, v_conv_ln_g, v_conv_ln_b, v_w_ret_out, v_w_conv_out, v_w_out, v_post_norm_g):
    S = x.shape[1]
    me = 4 * lax.axis_index("x") + 2 * lax.axis_index("y") + lax.axis_index("c")
    x2, target = x[0], loss_target[0]
    ada_w = w_ada.shape[2]
    cw_w = conv_w.shape[2]

    c_g, conv_w_g = _exchange([jnp.pad(c, ((0, 7), (0, 0))), jnp.pad(conv_w[0], ((0, 1), (0, 0)))],
                              False, "gather_c_conv_w", True)
    c_all = c_g[:, 0, :]
    conv_w_full = _round_bf16(conv_w_g.transpose(1, 0, 2).reshape(CONV_W + 1, D))
    conv_w_full = jnp.broadcast_to(conv_w_full[:, None, :], (CONV_W + 1, SUBLANES, D))
    b_part = lax.dynamic_slice(b_ada, (0, me * ada_w), (1, ada_w))
    mod_g = _exchange([_mod_part(c_all, w_ada[0], b_part)], False, "gather_mod", True)[0]

    own_in, mod_g = lax.optimization_barrier((w_in[0].astype(BF16), mod_g))
    in_send, in_recv, in_sent, in_zones, tok_in = _exchange_start([own_in], False, "gather_w_in_start")
    mod = lax.dynamic_index_in_dim(mod_g, me, axis=1, keepdims=False).reshape(1, 3 * D)
    shift, scale, gate = mod[:, :D], mod[:, D:2 * D], mod[:, 2 * D:]
    positions = positions + tok_in[0:1, 0:1].astype(jnp.int32)

    cos, sin = _rope_tables(positions, S)
    tables = _decay_tables()
    proj, h = _inproj_fwd(x2, pre_norm_g, scale, shift, in_zones[0], in_recv, in_sent[0], tm=min(S, 1024))
    g_in, proj = lax.optimization_barrier((in_zones[0], proj))
    _, tok_sent = _exchange_wait(in_send, in_recv, in_sent, in_zones, h, False, "gather_w_in_sends_wait", n_waited=1)
    *rest_handles, tok_rest = _exchange_start(
        [(w_ret_out[0] + tok_sent[0:1, 0:1]).astype(BF16), w_conv_out[0].astype(BF16), w_out[0].astype(BF16)],
        False, "gather_w_rest_start")
    w_in_full = g_in.transpose(1, 0, 2).reshape(D, IN_W)
    p_a, o, states = _retention_fwd(proj, cos, sin, tables, cpb=min(16, S // (2 * CHUNK)))
    p_b, a1 = _conv_fwd(proj, conv_w_full, conv_b + tok_rest[0:1, 0:1], conv_ln_g, conv_ln_b, tm=256)
    (g_ret, g_conv, g_out), _ = _exchange_wait(*rest_handles, p_b, False, "gather_w_rest_wait")
    w_ret_full, w_conv_full, w_out_full = g_ret.reshape(2 * D, D), g_conv.reshape(D, D), g_out.reshape(D, D)

    dout, d_g, dp_a, dp_b, merged, dy, dy_a, dy_b, small_a = _head(
        p_a, p_b, proj, x2, target, gate, post_norm_g, w_ret_full, w_conv_full, w_out_full, tm=256)
    parts_rest = [_matmul_tn(p_a, dy_a, "dw_ret_out", bn=D, tk=min(S, 2048)).astype(BF16).reshape(N_DEV, 2 * D // N_DEV, D),
                  _matmul_tn(p_b, dy_b, "dw_conv_out", bn=D, tk=min(S, 2048)).astype(BF16).reshape(N_DEV, D // N_DEV, D),
                  _matmul_tn(merged, dy, "dw_out", bn=D, tk=min(S, 2048)).astype(BF16).reshape(N_DEV, D // N_DEV, D)]
    *s_rest_handles, tok_s_rest = _exchange_start(parts_rest, True, "scatter_rest_start")
    d_c, small_b = _conv_bwd(dp_b, proj, a1, conv_w_full, conv_ln_g + tok_s_rest[0:1, 0:1], conv_ln_b, tm=256)
    d_r = _retention_bwd(dp_a, o, proj, states, cos, sin, tables, cpb=8)
    dw_r = _dw_in_retention(h, d_r, tk=min(S, 2048))
    dw_c = _matmul_tn(h, d_c, "dw_in_conv", bn=D, tk=min(S, 2048))
    dw_g = _matmul_tn(h, d_g, "dw_in_gate", bn=D, tk=min(S, 2048))
    dw_in = jnp.concatenate([*dw_r, dw_c, dw_g], axis=1)
    parts_in = dw_in.astype(BF16).reshape(D, N_DEV, W_SHARD).transpose(1, 0, 2)
    *s_in_handles, tok_s_in = _exchange_start([parts_in], True, "scatter_w_in_start")
    grad_x, small_c = _dh(d_r, d_c, d_g, w_in_full, x2, dout, pre_norm_g + tok_s_in[0:1, 0:1], scale, tm=256)

    (r_ret, r_conv, r_out), _ = _exchange_wait(*s_rest_handles, small_c, True, "scatter_rest_wait")
    (r_in,), _ = _exchange_wait(*s_in_handles, small_c, True, "scatter_w_in_wait")
    big = {"w_in": _adam_shard(r_in, w_in[0], m_w_in[0], v_w_in[0], "adam_w_in", 128),
           "w_ret_out": _adam_shard(r_ret, w_ret_out[0], m_w_ret_out[0], v_w_ret_out[0], "adam_w_ret_out", 128),
           "w_conv_out": _adam_shard(r_conv, w_conv_out[0], m_w_conv_out[0], v_w_conv_out[0], "adam_w_conv_out", 128),
           "w_out": _adam_shard(r_out, w_out[0], m_w_out[0], v_w_out[0], "adam_w_out", 128)}

    rows = jnp.concatenate([small_a[0:3], small_c[0:3], small_b[CONV_W:CONV_W + 3], jnp.zeros((7, D), F32)], axis=0)
    cw_parts = small_b[0:CONV_W + 1].reshape(CONV_W + 1, N_DEV, cw_w).transpose(1, 0, 2)
    sm_g, cw_g = _exchange([rows, cw_parts], (False, True), "gather_small", True)
    sm = _sum_devices(sm_g, "sum_small")
    row = lambda r: sm[r:r + 1]
    g8 = jnp.concatenate([row(ROW_SHIFT), row(ROW_SCALE), row(ROW_GATE), row(ROW_PRE), row(ROW_CONV_B),
                          row(ROW_LN_G), row(ROW_LN_B), row(ROW_POST)], axis=0)
    stack8 = lambda ba, pre, cb, lg, lb, post: jnp.concatenate([ba.reshape(3, D), pre, cb, lg, lb, post], axis=0)
    d8, m8, v8 = _adam_small(
        g8, stack8(b_ada, pre_norm_g, conv_b, conv_ln_g, conv_ln_b, post_norm_g),
        stack8(m_b_ada, m_pre_norm_g, m_conv_b, m_conv_ln_g, m_conv_ln_b, m_post_norm_g),
        stack8(v_b_ada, v_pre_norm_g, v_conv_b, v_conv_ln_g, v_conv_ln_b, v_post_norm_g), "adam_small")
    unstack8 = lambda a: {"b_ada": a[0:3].reshape(3 * D), "pre_norm_g": a[3], "conv_b": a[4],
                          "conv_ln_g": a[5], "conv_ln_b": a[6], "post_norm_g": a[7]}
    g_small, d_small, m_small, v_small = unstack8(g8), unstack8(d8), unstack8(m8), unstack8(v8)

    pad_row = lambda a: jnp.pad(a[0], ((0, 1), (0, 0)))
    g_cw = _sum_devices(cw_g, "sum_conv_w")
    d_cw, m_cw, v_cw = _adam_small(g_cw, pad_row(conv_w), pad_row(m_conv_w), pad_row(v_conv_w), "adam_conv_w")

    dmod_all = jnp.concatenate([sm_g[:, ROW_SHIFT], sm_g[:, ROW_SCALE], sm_g[:, ROW_GATE]], axis=1)
    g_wa = _grad_w_ada(c_all.T, lax.dynamic_slice(dmod_all, (0, me * ada_w), (N_DEV, ada_w)))
    d_wa, m_wa, v_wa = _adam_small(g_wa, w_ada[0], m_w_ada[0], v_w_ada[0], "adam_w_ada")

    grads = {"w_ada": g_wa, "conv_w": g_cw[:CONV_W], **g_small, **{n: t[0] for n, t in big.items()}}
    deltas = {"w_ada": d_wa, "conv_w": d_cw[:CONV_W], **d_small, **{n: t[1] for n, t in big.items()}}
    new_m = {"w_ada": m_wa, "conv_w": m_cw[:CONV_W], **m_small, **{n: t[2] for n, t in big.items()}}
    new_v = {"w_ada": v_wa, "conv_w": v_cw[:CONV_W], **v_small, **{n: t[3] for n, t in big.items()}}
    order = ["w_ada", "b_ada", "pre_norm_g", "w_in", "conv_w", "conv_b", "conv_ln_g", "conv_ln_b", "w_ret_out",
             "w_conv_out", "w_out", "post_norm_g"]
    loss = sm[ROW_LOSS, 0]
    out = [loss, grad_x[None]]
    for group in (grads, deltas, new_m, new_v):
        out += [group[n][None] for n in order]
    return tuple(out)
```

```python
import jax
import jax.numpy as jnp
import numpy as np
from jax import lax
from jax.experimental import pallas as pl
from jax.experimental.pallas import tpu as pltpu

F32 = jnp.float32
BF16 = jnp.bfloat16
MESH = pl.DeviceIdType.MESH

N_DEV = 8
D = 1024
HEADS = 4
DK = 256
DV = 512
CHUNK = 128
CONV_W = 31
HALO = 32
IN_W = 11264
W_SHARD = IN_W // N_DEV
ROPE_BASE = 10000.0
GN_EPS = 1e-5
LN_EPS = 1e-5
RMS_EPS = 1e-6
ADAM_LR, ADAM_B1, ADAM_B2, ADAM_EPS, ADAM_WD, ADAM_STEP = 0.001, 0.9, 0.999, 1e-08, 0.01, 10

COL_V, COL_Z, COL_UVAL, COL_UGATE, COL_ZCONV, COL_GA, COL_GB = 2, 4, 6, 7, 8, 9, 10

VMEM_LIMIT = 56 * 1024 * 1024


def _params(**kw):
    return pltpu.CompilerParams(vmem_limit_bytes=VMEM_LIMIT, **kw)


def _sigmoid(z):
    return jax.nn.sigmoid(z)


def _round_bf16(a):
    return a.astype(BF16).astype(F32)


def _dot(a, b):
    return jnp.dot(a, b, preferred_element_type=F32)


def _dot_nt(a, b):
    return lax.dot_general(a, b, (((1,), (1,)), ((), ())), preferred_element_type=F32)


def _dot_tn(a, b):
    return lax.dot_general(a, b, (((0,), (0,)), ((), ())), preferred_element_type=F32)


def _my_place():
    return lax.axis_index("x"), lax.axis_index("y"), lax.axis_index("c")


def _peer(k):
    x, y, c = _my_place()
    px = lax.rem(x + ((k >> 2) & 1), 2)
    py = lax.rem(y + ((k >> 1) & 1), 2)
    pc = lax.rem(c + (k & 1), 2)
    return (px, py, pc), 4 * px + 2 * py + pc


def _exchange(arrs, scatter, name, in_vmem):
    n = len(arrs)
    flags = tuple(scatter) if isinstance(scatter, (tuple, list)) else (scatter,) * n

    def body(*refs):
        ins, outs = refs[:n], refs[n:2 * n]
        send_sems, recv_sems, local_sems = refs[2 * n:]
        x, y, c = _my_place()
        me = 4 * x + 2 * y + c
        copies = []
        for i in range(n):
            src = ins[i].at[me] if flags[i] else ins[i]
            cp = pltpu.make_async_copy(src, outs[i].at[me], local_sems.at[i])
            cp.start()
            copies.append(cp)
        for k in range(1, N_DEV):
            peer, peer_idx = _peer(k)
            for i in range(n):
                src = ins[i].at[peer_idx] if flags[i] else ins[i]
                cp = pltpu.make_async_remote_copy(
                    src_ref=src, dst_ref=outs[i].at[me], send_sem=send_sems.at[i, k - 1],
                    recv_sem=recv_sems.at[i, k - 1], device_id=peer, device_id_type=MESH)
                cp.start()
                copies.append(cp)
        for cp in copies:
            cp.wait()

    space = pltpu.VMEM if in_vmem else pl.ANY
    out_shape = [jax.ShapeDtypeStruct(a.shape if f else (N_DEV,) + a.shape, a.dtype) for a, f in zip(arrs, flags)]
    return pl.pallas_call(
        body, name=name, out_shape=out_shape,
        in_specs=[pl.BlockSpec(memory_space=space)] * n,
        out_specs=[pl.BlockSpec(memory_space=space)] * n,
        scratch_shapes=[pltpu.SemaphoreType.DMA((n, N_DEV - 1)), pltpu.SemaphoreType.DMA((n, N_DEV - 1)),
                        pltpu.SemaphoreType.DMA((n,))],
        compiler_params=_params(has_side_effects=True),
    )(*arrs)


def _remote_copies(ins, lands, send_sems, recv_sems, scatter, first=0):
    x, y, c = _my_place()
    me = 4 * x + 2 * y + c
    copies = []
    for i in range(len(ins)):
        for k in range(1, N_DEV):
            peer, peer_idx = _peer(k)
            sem = (N_DEV - 1) * (first + i) + k - 1
            copies.append(pltpu.make_async_remote_copy(
                src_ref=ins[i].at[peer_idx] if scatter else ins[i], dst_ref=lands[i].at[me],
                send_sem=send_sems.at[sem], recv_sem=recv_sems.at[sem], device_id=peer, device_id_type=MESH))
    return copies


_HBM = pl.BlockSpec(memory_space=pltpu.HBM)
_SEM = pl.BlockSpec(memory_space=pltpu.SEMAPHORE)


def _exchange_start(arrs, scatter, name):
    n = len(arrs)
    me = 4 * lax.axis_index("x") + 2 * lax.axis_index("y") + lax.axis_index("c")
    lands = []
    for a in arrs:
        own = lax.dynamic_index_in_dim(a, me, axis=0) if scatter else a[None]
        zone = lax.empty(a.shape if scatter else (N_DEV,) + a.shape, a.dtype)
        lands.append(lax.dynamic_update_slice(zone, own, (me,) + (0,) * (own.ndim - 1)))

    def body(*refs):
        ins, land_refs, send_sems, recv_sems, token = refs[:n], refs[n:2 * n], refs[2 * n], refs[2 * n + 1], refs[-1]
        for cp in _remote_copies(ins, land_refs, send_sems, recv_sems, scatter):
            cp.start()
        token[...] = jnp.zeros_like(token)

    n_sem = (N_DEV - 1) * n
    hbm = lambda a: pltpu.HBM(a.shape, a.dtype)
    outs = pl.pallas_call(
        body, name=name,
        out_shape=(pltpu.SemaphoreType.DMA((n_sem,)), pltpu.SemaphoreType.DMA((n_sem,)), *[hbm(a) for a in arrs],
                   *[hbm(a) for a in lands], jax.ShapeDtypeStruct((8, 128), F32)),
        in_specs=(_HBM,) * (2 * n), out_specs=(_SEM, _SEM) + (_HBM,) * (2 * n) + (pl.BlockSpec(memory_space=pltpu.VMEM),),
        input_output_aliases={i: 2 + i for i in range(2 * n)},
        compiler_params=pltpu.CompilerParams(has_side_effects=pltpu.SideEffectType.DATAFLOW_SIDE_EFFECTING),
    )(*[pltpu.with_memory_space_constraint(a, pltpu.HBM) for a in list(arrs) + lands])
    return outs[0], outs[1], list(outs[2:2 + n]), list(outs[2 + n:2 + 2 * n]), outs[-1]


def _exchange_wait(send_sems, recv_sems, sent, zones, after, scatter, name, n_waited=0):
    n = len(sent)
    m = n - n_waited

    def body(*refs):
        early, ins, land_refs = refs[:n_waited], refs[n_waited:n], refs[n:n + m]
        s_sems, r_sems, token = refs[n + m], refs[n + m + 1], refs[-1]
        token[...] = jnp.zeros_like(token)
        for i in range(n_waited):
            for k in range(1, N_DEV):
                pltpu.make_async_remote_copy(
                    src_ref=early[i], dst_ref=early[i], send_sem=s_sems.at[(N_DEV - 1) * i + k - 1],
                    recv_sem=s_sems.at[(N_DEV - 1) * i + k - 1], device_id=_peer(k)[0], device_id_type=MESH).wait_send()
        for cp in _remote_copies(ins, land_refs, s_sems, r_sems, scatter, first=n_waited):
            cp.wait_send()
            cp.wait_recv()

    thru = list(sent[n_waited:]) + list(zones[n_waited:])
    outs = pl.pallas_call(
        body, name=name,
        out_shape=tuple(pltpu.HBM(a.shape, a.dtype) for a in thru) + (jax.ShapeDtypeStruct((8, 128), F32),),
        in_specs=(_HBM,) * (n_waited + 2 * m) + (_SEM, _SEM, pl.BlockSpec(memory_space=pl.ANY)),
        out_specs=(_HBM,) * (2 * m) + (pl.BlockSpec(memory_space=pltpu.VMEM),),
        input_output_aliases={n_waited + i: i for i in range(2 * m)},
        compiler_params=pltpu.CompilerParams(has_side_effects=pltpu.SideEffectType.DATAFLOW_SIDE_EFFECTING),
    )(*sent[:n_waited], *thru, send_sems, recv_sems, after)
    return list(outs[m:2 * m]), outs[-1]


def _rope_tables(positions, S):
    tm = min(S, 1024)

    half = DK // 2
    inv_freq = (ROPE_BASE ** (-jnp.arange(half, dtype=F32) / half)).reshape(1, half)

    def body(pos_ref, f_ref, cos_ref, sin_ref):
        ang = pos_ref[...].astype(F32) * f_ref[...]
        cos_ref[...] = jnp.cos(ang)
        sin_ref[...] = jnp.sin(ang)

    return pl.pallas_call(
        body, name="rope_tables", grid=(S // tm,),
        in_specs=[pl.BlockSpec((tm, 1), lambda i: (i, 0)), pl.BlockSpec((1, half), lambda i: (0, 0))],
        out_specs=[pl.BlockSpec((tm, half), lambda i: (i, 0))] * 2,
        out_shape=[jax.ShapeDtypeStruct((S, half), F32)] * 2,
    )(positions.reshape(S, 1), inv_freq)


def _mod_part(c_all, w_ada, b_part):
    def body(c_ref, w_ref, b_ref, o_ref):
        o_ref[...] = _dot(c_ref[...].astype(BF16), w_ref[...].astype(BF16)) + b_ref[...]

    return pl.pallas_call(body, name="mod_part",
                          out_shape=jax.ShapeDtypeStruct((N_DEV, w_ada.shape[1]), F32))(c_all, w_ada, b_part)


def _sum_devices(g, name):
    def body(g_ref, o_ref):
        acc = g_ref[0]
        for d in range(1, N_DEV):
            acc = acc + g_ref[d]
        o_ref[...] = acc

    return pl.pallas_call(body, name=name, out_shape=jax.ShapeDtypeStruct(g.shape[1:], F32))(g)


def _grad_w_ada(c_t, dmod):
    def body(c_ref, d_ref, o_ref):
        acc = c_ref[:, 0:1] * d_ref[0:1, :]
        for b in range(1, N_DEV):
            acc = acc + c_ref[:, b:b + 1] * d_ref[b:b + 1, :]
        o_ref[...] = acc

    return pl.pallas_call(body, name="grad_w_ada",
                          out_shape=jax.ShapeDtypeStruct((c_t.shape[0], dmod.shape[1]), F32))(c_t, dmod)


def _adam_math(w, g, m, v):
    m = ADAM_B1 * m + (1.0 - ADAM_B1) * g
    v = ADAM_B2 * v + (1.0 - ADAM_B2) * (g * g)
    m_hat = m / (1.0 - ADAM_B1 ** ADAM_STEP)
    v_hat = v / (1.0 - ADAM_B2 ** ADAM_STEP)
    delta = -ADAM_LR * (m_hat / (jnp.sqrt(v_hat) + ADAM_EPS) + ADAM_WD * w)
    return delta, m, v


def _adam_small(g, w, m, v, name):
    def body(g_ref, w_ref, m_ref, v_ref, d_ref, nm_ref, nv_ref):
        d_ref[...], nm_ref[...], nv_ref[...] = _adam_math(w_ref[...], g_ref[...], m_ref[...], v_ref[...])

    return pl.pallas_call(body, name=name, out_shape=[jax.ShapeDtypeStruct(w.shape, F32)] * 3)(g, w, m, v)


def _adam_shard(parts, w, m, v, name, tr):
    R, L = w.shape

    def body(p_ref, w_ref, m_ref, v_ref, g_ref, d_ref, nm_ref, nv_ref):
        g = p_ref[0].astype(F32)
        for d in range(1, N_DEV):
            g = g + p_ref[d].astype(F32)
        g_ref[...] = g
        d_ref[...], nm_ref[...], nv_ref[...] = _adam_math(w_ref[...], g, m_ref[...], v_ref[...])

    blk = pl.BlockSpec((tr, L), lambda i: (i, 0))
    return pl.pallas_call(
        body, name=name, grid=(R // tr,),
        in_specs=[pl.BlockSpec((N_DEV, tr, L), lambda i: (0, i, 0)), blk, blk, blk],
        out_specs=[blk] * 4, out_shape=[jax.ShapeDtypeStruct((R, L), F32)] * 4,
        compiler_params=_params(),
    )(parts, w, m, v)


ARRIVAL = (0, 1, 2, 4, 3, 5, 6, 7)
LEAD = 2


def _inproj_fwd(x, g_pre, scale, shift, zone, recv_sems, sent, tm):
    S = x.shape[0]
    nt = S // tm
    me = 4 * lax.axis_index("x") + 2 * lax.axis_index("y") + lax.axis_index("c")
    order = jnp.bitwise_xor(me, jnp.asarray(ARRIVAL, jnp.int32))

    def body(order_ref, x_ref, g_ref, sc_ref, sh_ref, w_ref, sent_ref, sem_ref, proj_ref, h_ref, h_all):
        j, i = pl.program_id(0), pl.program_id(1)
        for row in range(1, N_DEV):
            @pl.when((j == row) & (i == 0))
            def _():
                k = ARRIVAL[row]
                pltpu.make_async_remote_copy(
                    src_ref=sent_ref, dst_ref=sent_ref, send_sem=sem_ref.at[k - 1],
                    recv_sem=sem_ref.at[k - 1], device_id=_peer(k)[0], device_id_type=MESH).wait_recv()

        rows = pl.ds(pl.multiple_of(jnp.maximum(i - LEAD, 0) * tm, tm), tm)

        @pl.when((j == 0) & (i >= LEAD))
        def _():
            xf = x_ref[...]
            r = lax.rsqrt(jnp.mean(xf * xf, axis=-1, keepdims=True) + RMS_EPS)
            h = ((xf * r * g_ref[...]) * (1.0 + sc_ref[...]) + sh_ref[...]).astype(BF16)
            h_all[rows, :] = h
            h_ref[...] = h

        @pl.when(i >= LEAD)
        def _():
            proj_ref[...] = _dot(h_all[rows, :], w_ref[...])

    tile = lambda j, i: jnp.maximum(i - LEAD, 0)
    first_pass = lambda j, i: jnp.where(j == 0, tile(j, i), nt - 1)
    vec = pl.BlockSpec((1, D), lambda j, i, o: (0, 0))
    any_space = pl.BlockSpec(memory_space=pl.ANY)
    return pl.pallas_call(
        body, name="inproj_fwd",
        grid_spec=pltpu.PrefetchScalarGridSpec(
            num_scalar_prefetch=1, grid=(N_DEV, nt + LEAD),
            in_specs=[pl.BlockSpec((tm, D), lambda j, i, o: (first_pass(j, i), 0)), vec, vec, vec,
                      pl.BlockSpec((None, D, W_SHARD),
                                   lambda j, i, o: (o[jnp.where(i >= LEAD, j, jnp.maximum(j - 1, 0))], 0, 0)),
                      any_space, _SEM],
            out_specs=[pl.BlockSpec((tm, W_SHARD), lambda j, i, o: (tile(j, i), o[j])),
                       pl.BlockSpec((tm, D), lambda j, i, o: (first_pass(j, i), 0))],
            scratch_shapes=[pltpu.VMEM((S, D), BF16)]),
        out_shape=[jax.ShapeDtypeStruct((S, IN_W), F32), jax.ShapeDtypeStruct((S, D), BF16)],
        compiler_params=_params(has_side_effects=pltpu.SideEffectType.DATAFLOW_SIDE_EFFECTING),
    )(order, x, g_pre, scale, shift, zone, sent, recv_sems)


def _decay_tables():
    log_g = jnp.log1p(-jnp.exp2(-5.0 - jnp.arange(HEADS, dtype=F32)))
    idx = jnp.arange(CHUNK, dtype=F32)
    diff = idx[:, None] - idx[None, :]
    causal = diff >= 0
    mask = jnp.where(causal, jnp.exp(log_g[:, None, None] * jnp.where(causal, diff, 0.0)), 0.0)
    xi = jnp.exp(log_g[:, None] * (idx + 1.0))[:, :, None]
    zeta = jnp.exp(log_g[:, None] * (CHUNK - 1.0 - idx))[:, :, None]
    g_chunk = jnp.broadcast_to(jnp.exp(log_g * CHUNK)[:, None, None], (HEADS, 1, 128))
    return mask, xi, zeta, g_chunk


def _rot(t, cos, sin):
    t1, t2 = t[:, :DK // 2], t[:, DK // 2:]
    return jnp.concatenate([t1 * cos - t2 * sin, t1 * sin + t2 * cos], axis=-1)


def _rot_back(t, cos, sin):
    t1, t2 = t[:, :DK // 2], t[:, DK // 2:]
    return jnp.concatenate([t1 * cos + t2 * sin, t2 * cos - t1 * sin], axis=-1)


def _silu(z):
    return z * _sigmoid(z)


def _retention_fwd(proj, cos, sin, tables, cpb):
    S = proj.shape[0]
    tb = cpb * CHUNK
    mask, xi, zeta, g_chunk = tables

    def body(q_ref, k_ref, v_ref, z_ref, cos_ref, sin_ref, mask_ref, xi_ref, zeta_ref, gc_ref,
             pa_ref, o_ref, st_ref, rot_ref, state):
        @pl.when(pl.program_id(1) == 0)
        def _():
            state[...] = jnp.zeros_like(state)

        m, xi_c, zeta_c, gc = mask_ref[...], xi_ref[...], zeta_ref[...], gc_ref[0:1, 0:1]
        for ci in range(cpb):
            rows = pl.ds(ci * CHUNK, CHUNK)
            cs, sn = cos_ref[rows, :], sin_ref[rows, :]
            qr = _rot(q_ref[rows, :], cs, sn)
            kr = _rot(k_ref[rows, :], cs, sn) * (DK ** -0.5)
            vb = v_ref[rows, :].astype(BF16)
            qb, kb, kz = qr.astype(BF16), kr.astype(BF16), (kr * zeta_c).astype(BF16)
            rot_ref[rows, 0:DK] = qb
            rot_ref[rows, DK:2 * DK] = kb
            rot_ref[rows, 2 * DK:3 * DK] = kz
            st = state[...]
            stb = st.astype(BF16)
            st_ref[ci] = stb
            scores = (_dot_nt(qb, kb) * m).astype(BF16)
            o = _dot(scores, vb) + _dot(qb, stb) * xi_c
            state[...] = st * gc + _dot_tn(kz, vb)
            o_ref[rows, :] = o
            mu = jnp.mean(o, axis=-1, keepdims=True)
            oc = o - mu
            var = jnp.mean(oc * oc, axis=-1, keepdims=True)
            pa_ref[rows, :] = (oc * lax.rsqrt(var + GN_EPS) * _silu(z_ref[rows, :])).astype(BF16)

    tab = lambda shape: pl.BlockSpec((None,) + shape, lambda h, n: (h, 0, 0))
    return pl.pallas_call(
        body, name="retention_fwd", grid=(HEADS, S // tb),
        in_specs=[pl.BlockSpec((tb, DK), lambda h, n: (n, h)),
                  pl.BlockSpec((tb, DK), lambda h, n: (n, HEADS + h)),
                  pl.BlockSpec((tb, DV), lambda h, n: (n, COL_V * 2 + h)),
                  pl.BlockSpec((tb, DV), lambda h, n: (n, COL_Z * 2 + h)),
                  pl.BlockSpec((tb, DK // 2), lambda h, n: (n, 0)),
                  pl.BlockSpec((tb, DK // 2), lambda h, n: (n, 0)),
                  tab((CHUNK, CHUNK)), tab((CHUNK, 1)), tab((CHUNK, 1)), tab((1, 128))],
        out_specs=[pl.BlockSpec((tb, DV), lambda h, n: (n, h)),
                   pl.BlockSpec((tb, DV), lambda h, n: (n, h)),
                   pl.BlockSpec((None, cpb, DK, DV), lambda h, n: (h, n, 0, 0)),
                   pl.BlockSpec((tb, 3 * DK), lambda h, n: (n, h))],
        out_shape=[jax.ShapeDtypeStruct((S, HEADS * DV), BF16), jax.ShapeDtypeStruct((S, HEADS * DV), F32),
                   jax.ShapeDtypeStruct((HEADS, S // CHUNK, DK, DV), BF16),
                   jax.ShapeDtypeStruct((S, HEADS * 3 * DK), BF16)],
        scratch_shapes=[pltpu.VMEM((DK, DV), F32)],
        compiler_params=_params(),
    )(proj, proj, proj, proj, cos, sin, mask, xi, zeta, g_chunk)


SUBLANES = 8
FWD_ROWS = 16
EW_ROWS = 16


def _shifted_copies(src, dst):
    rows = dst.shape[1]
    src[pl.ds(rows, SUBLANES), :] = jnp.zeros((SUBLANES, src.shape[1]), src.dtype)
    for s in range(SUBLANES):
        dst[s] = src[pl.ds(s, rows), :]


def _layernorm_stats(a):
    mu = jnp.mean(a, axis=-1, keepdims=True)
    ac = a - mu
    var = jnp.mean(ac * ac, axis=-1, keepdims=True)
    rstd = lax.rsqrt(var + LN_EPS)
    return ac * rstd, rstd


def _conv_fwd(proj, conv_w, conv_b, ln_g, ln_b, tm):
    S = proj.shape[0]
    hb = tm // HALO

    def body(uv_ref, ug_ref, uvh_ref, ugh_ref, z_ref, w_ref, b_ref, g_ref, bb_ref, pb_ref, a1_ref, ext, ex):
        halo = _round_bf16(uvh_ref[...] * _sigmoid(ugh_ref[...]))
        ext[pl.ds(0, HALO), :] = jnp.where(pl.program_id(0) == 0, 0.0, halo)
        ext[pl.ds(HALO, tm), :] = _round_bf16(uv_ref[...] * _sigmoid(ug_ref[...]))
        _shifted_copies(ext, ex)
        bias, g, bb = b_ref[...], g_ref[...], bb_ref[...]

        def rows_body(rb, carry):
            r0 = pl.multiple_of(rb * FWD_ROWS, FWD_ROWS)
            groups = range(FWD_ROWS // SUBLANES)
            acc = [jnp.zeros((SUBLANES, D), F32) for _ in groups]
            for j in range(CONV_W):
                a, s = divmod(HALO - (CONV_W - 1) + j, SUBLANES)
                wj = w_ref[j]
                for k in groups:
                    acc[k] = acc[k] + wj * ex[s, pl.ds(r0 + (a + k) * SUBLANES, SUBLANES), :]
            for k in groups:
                a1_ref[pl.ds(r0 + k * SUBLANES, SUBLANES), :] = acc[k] + bias
            return carry

        lax.fori_loop(0, tm // FWD_ROWS, rows_body, 0)
        xhat, _ = _layernorm_stats(a1_ref[...])
        pb_ref[...] = (_silu(xhat * g + bb) * _silu(z_ref[...])).astype(BF16)

    vec = pl.BlockSpec((1, D), lambda i: (0, 0))
    halo = lambda col: pl.BlockSpec((HALO, D), lambda i: (jnp.maximum(i * hb - 1, 0), col))
    return pl.pallas_call(
        body, name="conv_fwd", grid=(S // tm,),
        in_specs=[pl.BlockSpec((tm, D), lambda i: (i, COL_UVAL)), pl.BlockSpec((tm, D), lambda i: (i, COL_UGATE)),
                  halo(COL_UVAL), halo(COL_UGATE), pl.BlockSpec((tm, D), lambda i: (i, COL_ZCONV)),
                  pl.BlockSpec((CONV_W + 1, SUBLANES, D), lambda i: (0, 0, 0)), vec, vec, vec],
        out_specs=[pl.BlockSpec((tm, D), lambda i: (i, 0))] * 2,
        out_shape=[jax.ShapeDtypeStruct((S, D), BF16), jax.ShapeDtypeStruct((S, D), F32)],
        scratch_shapes=[pltpu.VMEM((HALO + tm + SUBLANES, D), F32), pltpu.VMEM((SUBLANES, HALO + tm, D), F32)],
        compiler_params=_params(),
    )(proj, proj, proj, proj, proj, conv_w, conv_b, ln_g, ln_b)


def _head(p_a, p_b, proj, x, target, gate, g_post, w_ret, w_conv, w_out, tm):
    S = x.shape[0]

    def body(pa_ref, pb_ref, ga_ref, gb_ref, x_ref, t_ref, gate_ref, gp_ref, wr_ref, wc_ref, wo_ref,
             dout_ref, dg_ref, dpa_ref, dpb_ref, mb_ref, dy_ref, dya_ref, dyb_ref, small_ref):
        @pl.when(pl.program_id(0) == 0)
        def _():
            small_ref[...] = jnp.zeros_like(small_ref)

        gate, gp = gate_ref[...], gp_ref[...]
        ya = _dot(pa_ref[...], wr_ref[...])
        yb = _dot(pb_ref[...], wc_ref[...])
        sa, sb = _sigmoid(ga_ref[...]), _sigmoid(gb_ref[...])
        mb = (sa * ya + sb * yb).astype(BF16)
        mb_ref[...] = mb
        y = _dot(mb, wo_ref[...])
        r = lax.rsqrt(jnp.mean(y * y, axis=-1, keepdims=True) + RMS_EPS)
        yhat = y * r
        yn = yhat * gp
        err = (x_ref[...] + gate * yn) - t_ref[...]
        small_ref[2:3, :] += 0.5 * jnp.sum(jnp.mean(err * err, axis=-1, keepdims=True))
        dout = err * (1.0 / D)
        dout_ref[...] = dout
        small_ref[0:1, :] += jnp.sum(dout * yn, axis=0, keepdims=True)
        dyn = dout * gate
        small_ref[1:2, :] += jnp.sum(dyn * yhat, axis=0, keepdims=True)
        dyhat = dyn * gp
        dy = (r * (dyhat - yhat * jnp.mean(dyhat * yhat, axis=-1, keepdims=True))).astype(BF16)
        dy_ref[...] = dy
        dmerged = _dot_nt(dy, wo_ref[...])
        dya = dmerged * sa
        dyb = dmerged * sb
        dg_ref[:, 0:D] = (dya * ya * (1.0 - sa)).astype(BF16)
        dg_ref[:, D:2 * D] = (dyb * yb * (1.0 - sb)).astype(BF16)
        dya, dyb = dya.astype(BF16), dyb.astype(BF16)
        dya_ref[...] = dya
        dyb_ref[...] = dyb
        dpa_ref[...] = _dot_nt(dya, wr_ref[...])
        dpb_ref[...] = _dot_nt(dyb, wc_ref[...])

    vec = pl.BlockSpec((1, D), lambda i: (0, 0))
    tile = lambda w, col=0: pl.BlockSpec((tm, w), lambda i: (i, col))
    whole = lambda a: pl.BlockSpec(a.shape, lambda i: (0, 0), pipeline_mode=pl.Buffered(1))
    act = lambda w: jax.ShapeDtypeStruct((S, w), BF16)
    return pl.pallas_call(
        body, name="head", grid=(S // tm,),
        in_specs=[tile(2 * D), tile(D), tile(D, COL_GA), tile(D, COL_GB), tile(D), tile(D), vec, vec,
                  whole(w_ret), whole(w_conv), whole(w_out)],
        out_specs=[tile(D), tile(2 * D), tile(2 * D), tile(D),
                   tile(D), tile(D), tile(D), tile(D), pl.BlockSpec((8, D), lambda i: (0, 0))],
        out_shape=[jax.ShapeDtypeStruct((S, D), F32),
                   act(2 * D), jax.ShapeDtypeStruct((S, 2 * D), F32), jax.ShapeDtypeStruct((S, D), F32),
                   act(D), act(D), act(D), act(D), jax.ShapeDtypeStruct((8, D), F32)],
        compiler_params=_params(),
    )(p_a, p_b, proj, proj, x, target, gate, g_post, w_ret, w_conv, w_out)


def _matmul_tn(a, b, name, bn, tk):
    S, M = a.shape
    N = b.shape[1]

    def body(a_ref, b_ref, o_ref):
        @pl.when(pl.program_id(1) == 0)
        def _():
            o_ref[...] = jnp.zeros_like(o_ref)

        o_ref[...] += _dot_tn(a_ref[...], b_ref[...])

    return pl.pallas_call(
        body, name=name, grid=(N // bn, S // tk),
        in_specs=[pl.BlockSpec((tk, M), lambda j, k: (k, 0)), pl.BlockSpec((tk, bn), lambda j, k: (k, j))],
        out_specs=pl.BlockSpec((M, bn), lambda j, k: (0, j)),
        out_shape=jax.ShapeDtypeStruct((M, N), F32),
        compiler_params=_params(),
    )(a, b)


SMALL_B_ROWS = 40


def _dsilu(z, s):
    return s * (1.0 + z * (1.0 - s))


def _conv_bwd(dp_b, proj, a1, conv_w, ln_g, ln_b, tm):
    S = proj.shape[0]
    nt = S // tm

    def body(dpb_ref, uv_ref, ug_ref, z_ref, a1_ref, w_ref, g_ref, bb_ref,
             dc_ref, small_ref, dext, dx, carry, dw_acc):
        @pl.when(pl.program_id(0) == 0)
        def _():
            small_ref[...] = jnp.zeros_like(small_ref)
            carry[...] = jnp.zeros_like(carry)

        xhat, rstd = _layernorm_stats(a1_ref[...])
        g = g_ref[...]
        a2 = xhat * g + bb_ref[...]
        s2 = _sigmoid(a2)
        zc = z_ref[...]
        sz = _sigmoid(zc)
        dpb = dpb_ref[...]
        dc_ref[:, 2 * D:3 * D] = (dpb * (a2 * s2) * _dsilu(zc, sz)).astype(BF16)
        da2 = dpb * (zc * sz) * _dsilu(a2, s2)
        small_ref[32:33, :] += jnp.sum(da2 * xhat, axis=0, keepdims=True)
        small_ref[33:34, :] += jnp.sum(da2, axis=0, keepdims=True)
        dxhat = da2 * g
        da1 = rstd * (dxhat - jnp.mean(dxhat, axis=-1, keepdims=True)
                      - xhat * jnp.mean(dxhat * xhat, axis=-1, keepdims=True))
        small_ref[31:32, :] += jnp.sum(da1, axis=0, keepdims=True)
        da1 = _round_bf16(da1)
        dext[pl.ds(0, tm), :] = da1
        dext[pl.ds(tm, HALO), :] = carry[...]
        carry[...] = da1[0:HALO, :]
        _shifted_copies(dext, dx)
        dw_acc[...] = jnp.zeros_like(dw_acc)

        def conv_body(rb, c):
            r0 = pl.multiple_of(rb * EW_ROWS, EW_ROWS)
            rows = pl.ds(r0, EW_ROWS)
            uv = uv_ref[rows, :]
            sg = _sigmoid(ug_ref[rows, :])
            a0 = _round_bf16(uv * sg)
            halves = []
            for k in range(EW_ROWS // SUBLANES):
                a0_k = a0[k * SUBLANES:(k + 1) * SUBLANES, :]
                acc = jnp.zeros((SUBLANES, D), F32)
                for j in range(CONV_W):
                    a, s = divmod(CONV_W - 1 - j, SUBLANES)
                    win = dx[s, pl.ds(r0 + (a + k) * SUBLANES, SUBLANES), :]
                    acc = acc + w_ref[j] * win
                    dw_acc[j] += a0_k * win
                halves.append(acc)
            da0 = jnp.concatenate(halves, axis=0)
            dc_ref[rows, 0:D] = (da0 * sg).astype(BF16)
            dc_ref[rows, D:2 * D] = (da0 * uv * sg * (1.0 - sg)).astype(BF16)
            return c

        lax.fori_loop(0, tm // EW_ROWS, conv_body, 0)
        small_ref[0:CONV_W + 1, :] += jnp.sum(dw_acc[...], axis=1)

    vec = pl.BlockSpec((1, D), lambda i: (0, 0))
    tile = lambda col: pl.BlockSpec((tm, D), lambda i: (nt - 1 - i, col))
    return pl.pallas_call(
        body, name="conv_bwd", grid=(nt,),
        in_specs=[tile(0), tile(COL_UVAL), tile(COL_UGATE), tile(COL_ZCONV), tile(0),
                  pl.BlockSpec((CONV_W + 1, SUBLANES, D), lambda i: (0, 0, 0)), vec, vec],
        out_specs=[pl.BlockSpec((tm, 3 * D), lambda i: (nt - 1 - i, 0)),
                   pl.BlockSpec((SMALL_B_ROWS, D), lambda i: (0, 0))],
        out_shape=[jax.ShapeDtypeStruct((S, 3 * D), BF16), jax.ShapeDtypeStruct((SMALL_B_ROWS, D), F32)],
        scratch_shapes=[pltpu.VMEM((tm + HALO + SUBLANES, D), F32), pltpu.VMEM((SUBLANES, tm + HALO, D), F32),
                        pltpu.VMEM((HALO, D), F32), pltpu.VMEM((CONV_W + 1, SUBLANES, D), F32)],
        compiler_params=_params(),
    )(dp_b, proj, proj, proj, a1, conv_w, ln_g, ln_b)


RET_W = 2 * DK + 2 * DV


def _retention_bwd(dp_a, o, proj, rot, states, cos, sin, tables, cpb):
    S = proj.shape[0]
    tb = cpb * CHUNK
    nb = S // tb
    mask, xi, zeta, g_chunk = tables

    def body(dpa_ref, o_ref, rot_ref, v_ref, z_ref, st_ref, cos_ref, sin_ref, mask_ref, xi_ref, zeta_ref,
             gc_ref, out_ref, dstate):
        @pl.when(pl.program_id(1) == 0)
        def _():
            dstate[...] = jnp.zeros_like(dstate)

        m, xi_c, zeta_c, gc = mask_ref[...], xi_ref[...], zeta_ref[...], gc_ref[0:1, 0:1]
        for ci in reversed(range(cpb)):
            rows = pl.ds(ci * CHUNK, CHUNK)
            o = o_ref[rows, :]
            oc = o - jnp.mean(o, axis=-1, keepdims=True)
            rstd = lax.rsqrt(jnp.mean(oc * oc, axis=-1, keepdims=True) + GN_EPS)
            rhat = oc * rstd
            z = z_ref[rows, :]
            sz = _sigmoid(z)
            dpa = dpa_ref[rows, :]
            out_ref[rows, 2 * DK + DV:RET_W] = (dpa * rhat * _dsilu(z, sz)).astype(BF16)
            dret = dpa * (z * sz)
            d_o = rstd * (dret - jnp.mean(dret, axis=-1, keepdims=True)
                          - rhat * jnp.mean(dret * rhat, axis=-1, keepdims=True))
            cs, sn = cos_ref[rows, :], sin_ref[rows, :]
            qb, kb, kz = rot_ref[rows, 0:DK], rot_ref[rows, DK:2 * DK], rot_ref[rows, 2 * DK:3 * DK]
            vb = v_ref[rows, :].astype(BF16)
            scores = (_dot_nt(qb, kb) * m).astype(BF16)
            dob = d_o.astype(BF16)
            dox = (d_o * xi_c).astype(BF16)
            dscores = (_dot_nt(dob, vb) * m).astype(BF16)
            dst = dstate[...]
            dstb = dst.astype(BF16)
            dqr = _dot(dscores, kb) + _dot_nt(dox, st_ref[ci])
            dkr = _dot_tn(dscores, qb) + _dot_nt(vb, dstb) * zeta_c
            dv = _dot_tn(scores, dob) + _dot(kz, dstb)
            dstate[...] = dst * gc + _dot_tn(qb, dox)
            out_ref[rows, 0:DK] = _rot_back(dqr, cs, sn).astype(BF16)
            out_ref[rows, DK:2 * DK] = (_rot_back(dkr, cs, sn) * (DK ** -0.5)).astype(BF16)
            out_ref[rows, 2 * DK:2 * DK + DV] = dv.astype(BF16)

    tab = lambda shape: pl.BlockSpec((None,) + shape, lambda h, n: (h, 0, 0))
    blk = lambda w, col0: pl.BlockSpec((tb, w), lambda h, n: (nb - 1 - n, col0 + h))
    return pl.pallas_call(
        body, name="retention_bwd", grid=(HEADS, nb),
        in_specs=[blk(DV, 0), blk(DV, 0), blk(3 * DK, 0), blk(DV, COL_V * 2), blk(DV, COL_Z * 2),
                  pl.BlockSpec((None, cpb, DK, DV), lambda h, n: (h, nb - 1 - n, 0, 0)),
                  pl.BlockSpec((tb, DK // 2), lambda h, n: (nb - 1 - n, 0)),
                  pl.BlockSpec((tb, DK // 2), lambda h, n: (nb - 1 - n, 0)),
                  tab((CHUNK, CHUNK)), tab((CHUNK, 1)), tab((CHUNK, 1)), tab((1, 128))],
        out_specs=blk(RET_W, 0),
        out_shape=jax.ShapeDtypeStruct((S, HEADS * RET_W), BF16),
        scratch_shapes=[pltpu.VMEM((DK, DV), F32)],
        compiler_params=_params(),
    )(dp_a, o, rot, proj, proj, states, cos, sin, mask, xi, zeta, g_chunk)


def _dw_in_retention(h, d_r, tk):
    S = h.shape[0]

    def body(a_ref, b_ref, q_ref, k_ref, v_ref, z_ref):
        @pl.when(pl.program_id(1) == 0)
        def _():
            for ref in (q_ref, k_ref, v_ref, z_ref):
                ref[...] = jnp.zeros_like(ref)

        p = _dot_tn(a_ref[...], b_ref[...])
        q_ref[...] += p[:, 0:DK]
        k_ref[...] += p[:, DK:2 * DK]
        v_ref[...] += p[:, 2 * DK:2 * DK + DV]
        z_ref[...] += p[:, 2 * DK + DV:RET_W]

    col = lambda w: pl.BlockSpec((D, w), lambda j, k: (0, j))
    out = lambda w: jax.ShapeDtypeStruct((D, HEADS * w), F32)
    return pl.pallas_call(
        body, name="dw_in_ret", grid=(HEADS, S // tk),
        in_specs=[pl.BlockSpec((tk, D), lambda j, k: (k, 0)), pl.BlockSpec((tk, RET_W), lambda j, k: (k, j))],
        out_specs=[col(DK), col(DK), col(DV), col(DV)], out_shape=[out(DK), out(DK), out(DV), out(DV)],
        compiler_params=_params(),
    )(h, d_r)


def _dh(d_r, d_c, d_g, w_in, x, dout, g_pre, scale, tm):
    S = x.shape[0]

    def body(dr_ref, dc_ref, dg_ref, w_ref, x_ref, dout_ref, g_ref, sc_ref, gx_ref, small_ref):
        @pl.when(pl.program_id(0) == 0)
        def _():
            small_ref[...] = jnp.zeros_like(small_ref)

        dh = _dot_nt(dc_ref[...], w_ref[:, 6 * D:9 * D]) + _dot_nt(dg_ref[...], w_ref[:, 9 * D:11 * D])
        for h in range(HEADS):
            pieces = ((0, DK, h * DK), (DK, DK, D + h * DK), (2 * DK, DV, 2 * D + h * DV), (2 * DK + DV, DV, 4 * D + h * DV))
            for start, width, col in pieces:
                dh = dh + _dot_nt(dr_ref[:, h * RET_W + start:h * RET_W + start + width], w_ref[:, col:col + width])
        xf = x_ref[...]
        r = lax.rsqrt(jnp.mean(xf * xf, axis=-1, keepdims=True) + RMS_EPS)
        xhat = xf * r
        g, sc1 = g_ref[...], 1.0 + sc_ref[...]
        small_ref[0:1, :] += jnp.sum(dh, axis=0, keepdims=True)
        small_ref[1:2, :] += jnp.sum(dh * (xhat * g), axis=0, keepdims=True)
        small_ref[2:3, :] += jnp.sum(dh * sc1 * xhat, axis=0, keepdims=True)
        dxhat = dh * sc1 * g
        gx_ref[...] = dout_ref[...] + r * (dxhat - xhat * jnp.mean(dxhat * xhat, axis=-1, keepdims=True))

    vec = pl.BlockSpec((1, D), lambda i: (0, 0))
    tile = lambda w: pl.BlockSpec((tm, w), lambda i: (i, 0))
    whole = lambda a: pl.BlockSpec(a.shape, lambda i: (0, 0), pipeline_mode=pl.Buffered(1))
    return pl.pallas_call(
        body, name="dh", grid=(S // tm,),
        in_specs=[tile(d_r.shape[1]), tile(d_c.shape[1]), tile(d_g.shape[1]), whole(w_in), tile(D), tile(D), vec, vec],
        out_specs=[tile(D), pl.BlockSpec((8, D), lambda i: (0, 0))],
        out_shape=[jax.ShapeDtypeStruct((S, D), F32), jax.ShapeDtypeStruct((8, D), F32)],
        compiler_params=_params(),
    )(d_r, d_c, d_g, w_in, x, dout, g_pre, scale)


ROW_GATE, ROW_POST, ROW_LOSS, ROW_SHIFT, ROW_SCALE, ROW_PRE, ROW_CONV_B, ROW_LN_G, ROW_LN_B = range(9)


def kernel(x, c, positions, w_ada, b_ada, pre_norm_g, w_in, conv_w, conv_b, conv_ln_g, conv_ln_b, w_ret_out, w_conv_out, w_out, post_norm_g, loss_target, m_w_ada, m_b_ada, m_pre_norm_g, m_w_in, m_conv_w, m_conv_b, m_conv_ln_g, m_conv_ln_b, m_w_ret_out, m_w_conv_out, m_w_out, m_post_norm_g, v_w_ada, v_b_ada, v_pre_norm_g, v_w_in, v_conv_w, v_conv_b, v_conv_ln_g, v_conv_ln_b, v_w_ret_out, v_w_conv_out, v_w_out, v_post_norm_g):
    S = x.shape[1]
    me = 4 * lax.axis_index("x") + 2 * lax.axis_index("y") + lax.axis_index("c")
    x2, target = x[0], loss_target[0]
    ada_w = w_ada.shape[2]
    cw_w = conv_w.shape[2]

    c_g, conv_w_g = _exchange([jnp.pad(c, ((0, 7), (0, 0))), jnp.pad(conv_w[0], ((0, 1), (0, 0)))],
                              False, "gather_c_conv_w", True)
    c_all = c_g[:, 0, :]
    conv_w_full = _round_bf16(conv_w_g.transpose(1, 0, 2).reshape(CONV_W + 1, D))
    conv_w_full = jnp.broadcast_to(conv_w_full[:, None, :], (CONV_W + 1, SUBLANES, D))
    b_part = lax.dynamic_slice(b_ada, (0, me * ada_w), (1, ada_w))
    mod_g = _exchange([_mod_part(c_all, w_ada[0], b_part)], False, "gather_mod", True)[0]

    own_in, mod_g = lax.optimization_barrier((w_in[0].astype(BF16), mod_g))
    in_send, in_recv, in_sent, in_zones, tok_in = _exchange_start([own_in], False, "gather_w_in_start")
    mod = lax.dynamic_index_in_dim(mod_g, me, axis=1, keepdims=False).reshape(1, 3 * D)
    shift, scale, gate = mod[:, :D], mod[:, D:2 * D], mod[:, 2 * D:]
    positions = positions + tok_in[0:1, 0:1].astype(jnp.int32)

    cos, sin = _rope_tables(positions, S)
    tables = _decay_tables()
    proj, h = _inproj_fwd(x2, pre_norm_g, scale, shift, in_zones[0], in_recv, in_sent[0], tm=min(S, 1024))
    g_in, proj = lax.optimization_barrier((in_zones[0], proj))
    _, tok_sent = _exchange_wait(in_send, in_recv, in_sent, in_zones, h, False, "gather_w_in_sends_wait", n_waited=1)
    *rest_handles, tok_rest = _exchange_start(
        [(w_ret_out[0] + tok_sent[0:1, 0:1]).astype(BF16), w_conv_out[0].astype(BF16), w_out[0].astype(BF16)],
        False, "gather_w_rest_start")
    w_in_full = g_in.transpose(1, 0, 2).reshape(D, IN_W)
    p_a, o, states, rot = _retention_fwd(proj, cos, sin, tables, cpb=8)
    p_b, a1 = _conv_fwd(proj, conv_w_full, conv_b + tok_rest[0:1, 0:1], conv_ln_g, conv_ln_b, tm=256)
    (g_ret, g_conv, g_out), _ = _exchange_wait(*rest_handles, p_b, False, "gather_w_rest_wait")
    w_ret_full, w_conv_full, w_out_full = g_ret.reshape(2 * D, D), g_conv.reshape(D, D), g_out.reshape(D, D)

    dout, d_g, dp_a, dp_b, merged, dy, dy_a, dy_b, small_a = _head(
        p_a, p_b, proj, x2, target, gate, post_norm_g, w_ret_full, w_conv_full, w_out_full, tm=256)
    parts_rest = [_matmul_tn(p_a, dy_a, "dw_ret_out", bn=D, tk=min(S, 2048)).astype(BF16).reshape(N_DEV, 2 * D // N_DEV, D),
                  _matmul_tn(p_b, dy_b, "dw_conv_out", bn=D, tk=min(S, 2048)).astype(BF16).reshape(N_DEV, D // N_DEV, D),
                  _matmul_tn(merged, dy, "dw_out", bn=D, tk=min(S, 2048)).astype(BF16).reshape(N_DEV, D // N_DEV, D)]
    *s_rest_handles, tok_s_rest = _exchange_start(parts_rest, True, "scatter_rest_start")
    d_c, small_b = _conv_bwd(dp_b, proj, a1, conv_w_full, conv_ln_g + tok_s_rest[0:1, 0:1], conv_ln_b, tm=256)
    d_r = _retention_bwd(dp_a, o, proj, rot, states, cos, sin, tables, cpb=8)
    dw_r = _dw_in_retention(h, d_r, tk=min(S, 2048))
    dw_c = _matmul_tn(h, d_c, "dw_in_conv", bn=D, tk=min(S, 2048))
    dw_g = _matmul_tn(h, d_g, "dw_in_gate", bn=D, tk=min(S, 2048))
    dw_in = jnp.concatenate([*dw_r, dw_c, dw_g], axis=1)
    parts_in = dw_in.astype(BF16).reshape(D, N_DEV, W_SHARD).transpose(1, 0, 2)
    *s_in_handles, tok_s_in = _exchange_start([parts_in], True, "scatter_w_in_start")
    grad_x, small_c = _dh(d_r, d_c, d_g, w_in_full, x2, dout, pre_norm_g + tok_s_in[0:1, 0:1], scale, tm=256)

    (r_ret, r_conv, r_out), _ = _exchange_wait(*s_rest_handles, small_c, True, "scatter_rest_wait")
    (r_in,), _ = _exchange_wait(*s_in_handles, small_c, True, "scatter_w_in_wait")
    big = {"w_in": _adam_shard(r_in, w_in[0], m_w_in[0], v_w_in[0], "adam_w_in", 128),
           "w_ret_out": _adam_shard(r_ret, w_ret_out[0], m_w_ret_out[0], v_w_ret_out[0], "adam_w_ret_out", 128),
           "w_conv_out": _adam_shard(r_conv, w_conv_out[0], m_w_conv_out[0], v_w_conv_out[0], "adam_w_conv_out", 128),
           "w_out": _adam_shard(r_out, w_out[0], m_w_out[0], v_w_out[0], "adam_w_out", 128)}

    rows = jnp.concatenate([small_a[0:3], small_c[0:3], small_b[CONV_W:CONV_W + 3], jnp.zeros((7, D), F32)], axis=0)
    cw_parts = small_b[0:CONV_W + 1].reshape(CONV_W + 1, N_DEV, cw_w).transpose(1, 0, 2)
    sm_g, cw_g = _exchange([rows, cw_parts], (False, True), "gather_small", True)
    sm = _sum_devices(sm_g, "sum_small")
    row = lambda r: sm[r:r + 1]
    g8 = jnp.concatenate([row(ROW_SHIFT), row(ROW_SCALE), row(ROW_GATE), row(ROW_PRE), row(ROW_CONV_B),
                          row(ROW_LN_G), row(ROW_LN_B), row(ROW_POST)], axis=0)
    stack8 = lambda ba, pre, cb, lg, lb, post: jnp.concatenate([ba.reshape(3, D), pre, cb, lg, lb, post], axis=0)
    d8, m8, v8 = _adam_small(
        g8, stack8(b_ada, pre_norm_g, conv_b, conv_ln_g, conv_ln_b, post_norm_g),
        stack8(m_b_ada, m_pre_norm_g, m_conv_b, m_conv_ln_g, m_conv_ln_b, m_post_norm_g),
        stack8(v_b_ada, v_pre_norm_g, v_conv_b, v_conv_ln_g, v_conv_ln_b, v_post_norm_g), "adam_small")
    unstack8 = lambda a: {"b_ada": a[0:3].reshape(3 * D), "pre_norm_g": a[3], "conv_b": a[4],
                          "conv_ln_g": a[5], "conv_ln_b": a[6], "post_norm_g": a[7]}
    g_small, d_small, m_small, v_small = unstack8(g8), unstack8(d8), unstack8(m8), unstack8(v8)

    pad_row = lambda a: jnp.pad(a[0], ((0, 1), (0, 0)))
    g_cw = _sum_devices(cw_g, "sum_conv_w")
    d_cw, m_cw, v_cw = _adam_small(g_cw, pad_row(conv_w), pad_row(m_conv_w), pad_row(v_conv_w), "adam_conv_w")

    dmod_all = jnp.concatenate([sm_g[:, ROW_SHIFT], sm_g[:, ROW_SCALE], sm_g[:, ROW_GATE]], axis=1)
    g_wa = _grad_w_ada(c_all.T, lax.dynamic_slice(dmod_all, (0, me * ada_w), (N_DEV, ada_w)))
    d_wa, m_wa, v_wa = _adam_small(g_wa, w_ada[0], m_w_ada[0], v_w_ada[0], "adam_w_ada")

    grads = {"w_ada": g_wa, "conv_w": g_cw[:CONV_W], **g_small, **{n: t[0] for n, t in big.items()}}
    deltas = {"w_ada": d_wa, "conv_w": d_cw[:CONV_W], **d_small, **{n: t[1] for n, t in big.items()}}
    new_m = {"w_ada": m_wa, "conv_w": m_cw[:CONV_W], **m_small, **{n: t[2] for n, t in big.items()}}
    new_v = {"w_ada": v_wa, "conv_w": v_cw[:CONV_W], **v_small, **{n: t[3] for n, t in big.items()}}
    order = ["w_ada", "b_ada", "pre_norm_g", "w_in", "conv_w", "conv_b", "conv_ln_g", "conv_ln_b", "w_ret_out",
             "w_conv_out", "w_out", "post_norm_g"]
    loss = sm[ROW_LOSS, 0]
    out = [loss, grad_x[None]]
    for group in (grads, deltas, new_m, new_v):
        out += [group[n][None] for n in order]
    return tuple(out)
```
